```python
import math
import jax, jax.numpy as jnp
from jax import lax
import numpy as np

D_MODEL = 1024
BATCH = 8
SEQ = 2048
DEPTH = 4
DEC_BATCH = 32
DEC_SEQ = 8
PAST_LEN = 8192
PAGE_SIZE = 128

MIX_W = D_MODEL
SSD_W = MIX_W // 2
SSD_HD = 64
SSD_H = SSD_W // SSD_HD
SSD_N = 128
SSD_G = 2
SSD_CONV = 4
SSD_CHUNK = 128
SSD_CONV_DIM = SSD_W + 2 * SSD_G * SSD_N
POOL_W = MIX_W // 4
POOL_WINDOWS = (2, 4, 8, 16)
POOL_GC = POOL_W // len(POOL_WINDOWS)
POOL_KEEP = max(POOL_WINDOWS) - 1
NSA_W = MIX_W // 4
NSA_HD = 64
NSA_H = NSA_W // NSA_HD
CMP_STRIDE = 16
CMP_LEN = 2 * CMP_STRIDE
CMP_HID = 2 * NSA_HD
SLC_BLOCK = 64
N_SELECT = 16
WINDOW = 512
Q_BLOCK = 128
ROPE_DIM = NSA_HD // 4
ROPE_THETA = 500000.0
D_FF = 4 * D_MODEL
FFN_CONV = 3
RMS_EPS = 1e-6
OFF_Z = SSD_W
OFF_XBC = OFF_Z + SSD_CONV_DIM
OFF_DT = OFF_XBC + SSD_H
OFF_POOL = OFF_DT + POOL_W
OFF_Q = OFF_POOL + NSA_W
OFF_KV = OFF_Q + 6 * NSA_HD
IN_W = OFF_KV + 3 * NSA_H
IN_SPLITS = (OFF_Z, OFF_XBC, OFF_DT, OFF_POOL, OFF_Q, OFF_KV)

kernel_name = 'hymba_ssd_pool_nsa_decode_step'


def rmsnorm(x, w):
    xf = x.astype(jnp.float32)
    y = xf * lax.rsqrt(jnp.mean(xf * xf, axis=-1, keepdims=True) + RMS_EPS)
    return (y * w.astype(jnp.float32)).astype(x.dtype)


def rope(x, pos):
    half = ROPE_DIM // 2
    inv = 1.0 / (ROPE_THETA ** (jnp.arange(half, dtype=jnp.float32) / half))
    ang = pos.astype(jnp.float32)[:, None] * inv
    shape = (ang.shape[0],) + (1,) * (x.ndim - 3) + (half,)
    cos = jnp.cos(ang).reshape(shape)
    sin = jnp.sin(ang).reshape(shape)
    xf = x.astype(jnp.float32)
    x1 = xf[..., :half]
    x2 = xf[..., half:ROPE_DIM]
    out = jnp.concatenate([x1 * cos - x2 * sin, x2 * cos + x1 * sin, xf[..., ROPE_DIM:]], axis=-1)
    return out.astype(x.dtype)


def causal_dwconv(u, prev, w, b):
    k_w = w.shape[0]
    t = u.shape[1]
    ext = jnp.concatenate([prev, u], axis=1)
    out = ext[:, :t] * w[0]
    for k in range(1, k_w):
        out = out + ext[:, k:k + t] * w[k]
    return out + b, ext[:, t:]


def masked_softmax(s, mask):
    s = jnp.where(mask, s, -jnp.inf)
    m = jnp.max(s, axis=-1, keepdims=True)
    m = jnp.where(jnp.isfinite(m), m, 0.0)
    e = jnp.where(mask, jnp.exp(s - m), 0.0)
    return e / jnp.maximum(jnp.sum(e, axis=-1, keepdims=True), 1e-30)


def segsum(a):
    t = a.shape[-1]
    x = jnp.broadcast_to(a[..., :, None], a.shape + (t,))
    s = jnp.cumsum(jnp.where(jnp.tril(jnp.ones((t, t), bool), -1), x, 0.0), axis=-2)
    return jnp.where(jnp.tril(jnp.ones((t, t), bool)), s, -jnp.inf)


def ssd_chunked(x, dt, a, bg, cg, h0):
    b, t, h, p = x.shape
    q = min(SSD_CHUNK, t)
    nc = -(-t // q)
    pad = nc * q - t
    def padt(v):
        return jnp.pad(v, ((0, 0), (0, pad)) + ((0, 0),) * (v.ndim - 2))
    x, dt, bg, cg = padt(x), padt(dt), padt(bg), padt(cg)
    rep = h // SSD_G
    bh = jnp.repeat(bg, rep, axis=2).reshape(b, nc, q, h, SSD_N)
    ch = jnp.repeat(cg, rep, axis=2).reshape(b, nc, q, h, SSD_N)
    xc = (x * dt[..., None]).reshape(b, nc, q, h, p)
    ac = (dt * a).reshape(b, nc, q, h).transpose(0, 3, 1, 2)
    a_cum = jnp.cumsum(ac, axis=-1)
    decay_in = jnp.exp(segsum(ac))
    scores = jnp.einsum('bclhn,bcshn->bhcls', ch, bh) * decay_in
    y_diag = jnp.einsum('bhcls,bcshp->bclhp', scores, xc)
    decay_to_end = jnp.exp(a_cum[..., -1:] - a_cum)
    chunk_states = jnp.einsum('bclhn,bhcl,bclhp->bchpn', bh, decay_to_end, xc)
    states = jnp.concatenate([h0[:, None], chunk_states], axis=1)
    chunk_decay = jnp.exp(segsum(jnp.pad(a_cum[..., -1], ((0, 0), (0, 0), (1, 0)))))
    states = jnp.einsum('bhzc,bchpn->bzhpn', chunk_decay, states)
    y_off = jnp.einsum('bclhn,bchpn,bhcl->bclhp', ch, states[:, :-1], jnp.exp(a_cum))
    y = (y_diag + y_off).reshape(b, nc * q, h, p)[:, :t]
    return y, states[:, -1]


def ssd_mixer(z, xbc, dt_raw, conv_prev, ssm_prev, conv_w, conv_b, dt_bias, a_log, d_skip, norm_w):
    f32 = jnp.float32
    b, t, _ = z.shape
    xbc, conv_new = causal_dwconv(xbc, conv_prev, conv_w, conv_b)
    xbc = jax.nn.silu(xbc)
    xs, bm, cm = jnp.split(xbc, (SSD_W, SSD_W + SSD_G * SSD_N), axis=-1)
    xh = xs.reshape(b, t, SSD_H, SSD_HD).astype(f32)
    dt = jax.nn.softplus(dt_raw.astype(f32) + dt_bias.astype(f32))
    a = -jnp.exp(a_log.astype(f32))
    y, ssm_new = ssd_chunked(xh, dt, a, bm.reshape(b, t, SSD_G, SSD_N).astype(f32),
                             cm.reshape(b, t, SSD_G, SSD_N).astype(f32), ssm_prev.astype(f32))
    y = y + d_skip.astype(f32)[:, None] * xh
    y = y.reshape(b, t, SSD_W) * jax.nn.silu(z.astype(f32))
    y = rmsnorm(y, norm_w)
    return y.astype(z.dtype), conv_new, ssm_new.astype(ssm_prev.dtype)


def pool_mixer(u, prev, pos0, w_grp, scale):
    b, t, c = u.shape
    ext = jnp.concatenate([prev, u], axis=1).astype(jnp.float32)
    cs = jnp.concatenate([jnp.zeros((b, 1, c), jnp.float32), jnp.cumsum(ext, axis=1)], axis=1)
    pos = pos0 + jnp.arange(t)
    outs = []
    for g, w in enumerate(POOL_WINDOWS):
        sl = slice(g * POOL_GC, (g + 1) * POOL_GC)
        s = cs[:, POOL_KEEP + 1:POOL_KEEP + 1 + t, sl] - cs[:, POOL_KEEP + 1 - w:POOL_KEEP + 1 - w + t, sl]
        cnt = jnp.minimum(pos + 1, w).astype(jnp.float32)
        outs.append(s / cnt[None, :, None])
    pooled = jnp.concatenate(outs, axis=-1) - u.astype(jnp.float32)
    y = jnp.einsum('btgc,gcd->btgd', pooled.reshape(b, t, len(POOL_WINDOWS), POOL_GC), w_grp.astype(jnp.float32))
    y = y.reshape(b, t, c) * scale.astype(jnp.float32)
    return y.astype(u.dtype), ext[:, -POOL_KEEP:].astype(u.dtype)


def compress_blocks(rows, pe, w1, w2):
    b, l, d = rows.shape
    n_ch = -(-l // CMP_STRIDE)
    rows = jnp.pad(rows, ((0, 0), (0, n_ch * CMP_STRIDE - l), (0, 0)))
    ch = rows.reshape(b, n_ch, CMP_STRIDE, d)
    h = (jnp.einsum('bnld,lde->bne', ch[:, :-1] + pe[:CMP_STRIDE], w1[:CMP_STRIDE])
         + jnp.einsum('bnld,lde->bne', ch[:, 1:] + pe[CMP_STRIDE:], w1[CMP_STRIDE:]))
    return jax.nn.gelu(h) @ w2


def nsa_attend(q, gates, full, win_ext, pos0, pe_k, pe_v, w1_k, w1_v, w2_k, w2_v):
    f32 = jnp.float32
    b, t, h, d = q.shape
    q = q.astype(f32) * (d ** -0.5)
    gates = gates.astype(f32)
    full = full.astype(f32)
    win_ext = win_ext.astype(f32)
    lk = full.shape[1]
    ck = compress_blocks(full[:, :, 0], pe_k.astype(f32), w1_k.astype(f32), w2_k.astype(f32))
    cv = compress_blocks(full[:, :, 1], pe_v.astype(f32), w1_v.astype(f32), w2_v.astype(f32))
    n_cmp = ck.shape[1]
    cmp_end = jnp.arange(n_cmp) * CMP_STRIDE + (CMP_LEN - 1)
    n_slc = -(-lk // SLC_BLOCK)
    pad_k = n_slc * SLC_BLOCK - lk
    ks = jnp.pad(full[:, :, 2], ((0, 0), (0, pad_k), (0, 0)))
    vs = jnp.pad(full[:, :, 3], ((0, 0), (0, pad_k), (0, 0)))
    c_start = jnp.arange(n_cmp)[:, None] * CMP_STRIDE
    s_start = jnp.arange(n_slc)[None, :] * SLC_BLOCK
    overlap = ((c_start < s_start + SLC_BLOCK) & (c_start + CMP_LEN > s_start)).astype(f32)
    n_sel = min(N_SELECT, n_slc)
    blk = jnp.arange(n_slc)
    qb_len = min(Q_BLOCK, t)
    n_qb = -(-t // qb_len)
    pad_q = n_qb * qb_len - t
    qp = jnp.pad(q, ((0, 0), (0, pad_q), (0, 0), (0, 0)))
    gp = jnp.pad(gates, ((0, 0), (0, pad_q), (0, 0), (0, 0)))
    wk = jnp.pad(win_ext[:, :, 0], ((0, 0), (0, pad_q), (0, 0)))
    wv = jnp.pad(win_ext[:, :, 1], ((0, 0), (0, pad_q), (0, 0)))
    band = WINDOW + qb_len

    def query_block(args):
        qb, gb, s = args
        tpos = pos0 + s + jnp.arange(qb_len)
        sc = jnp.einsum('bqhd,bnd->bhqn', qb, ck)
        pc = masked_softmax(sc, (cmp_end[None, :] <= tpos[:, None])[None, None])
        o_cmp = jnp.einsum('bhqn,bnd->bqhd', pc, cv)
        imp = jnp.einsum('bhqn,nj->bqj', pc, overlap)
        cur = (tpos // SLC_BLOCK)[:, None]
        forced = (blk[None] == 0) | (blk[None] == cur) | (blk[None] == cur - 1)
        imp = jnp.where(forced[None], jnp.inf, imp)
        imp = jnp.where((blk[None] * SLC_BLOCK <= tpos[:, None])[None], imp, -jnp.inf)
        top_v, top_i = lax.top_k(imp, n_sel)
        tok = (top_i[..., None] * SLC_BLOCK + jnp.arange(SLC_BLOCK)).reshape(b, qb_len, n_sel * SLC_BLOCK)
        kg = jax.vmap(lambda kk, ii: kk[ii])(ks, tok)
        vg = jax.vmap(lambda vv, ii: vv[ii])(vs, tok)
        smask = (tok <= tpos[None, :, None]) & jnp.repeat(top_v > -jnp.inf, SLC_BLOCK, axis=-1)
        ss = jnp.einsum('bqhd,bqsd->bhqs', qb, kg)
        ps = masked_softmax(ss, smask[:, None])
        o_slc = jnp.einsum('bhqs,bqsd->bqhd', ps, vg)
        kw = lax.dynamic_slice_in_dim(wk, s, band, axis=1)
        vw = lax.dynamic_slice_in_dim(wv, s, band, axis=1)
        kpos = pos0 - WINDOW + s + jnp.arange(band)
        wmask = (kpos[None] >= 0) & (kpos[None] <= tpos[:, None]) & (kpos[None] > tpos[:, None] - WINDOW)
        sw = jnp.einsum('bqhd,bkd->bhqk', qb, kw)
        pw = masked_softmax(sw, wmask[None, None])
        o_win = jnp.einsum('bhqk,bkd->bqhd', pw, vw)
        return gb[..., 0:1] * o_cmp + gb[..., 1:2] * o_slc + gb[..., 2:3] * o_win

    qs = qp.reshape(b, n_qb, qb_len, h, d).swapaxes(0, 1)
    gs = gp.reshape(b, n_qb, qb_len, h, 3).swapaxes(0, 1)
    starts = jnp.arange(n_qb, dtype=jnp.int32) * qb_len
    out = lax.map(query_block, (qs, gs, starts))
    return out.swapaxes(0, 1).reshape(b, n_qb * qb_len, h, d)[:, :t]


def conv_ffn(h, prev, w_gate, w_val, conv_w, conv_b, w_down):
    gc, new_prev = causal_dwconv(h @ w_gate, prev, conv_w, conv_b)
    act = jax.nn.gelu(gc, approximate=True) * (h @ w_val)
    return act @ w_down, new_prev


def hybrid_layer(x, pos0, nsa_past, win_prefix, win_keep, conv_prev, ssm_prev, pool_prev, ffn_prev, p):
    b, t, _ = x.shape
    pos = pos0 + jnp.arange(t, dtype=jnp.int32)
    h = rmsnorm(x, p['norm_mix_pre'])
    proj = h @ p['w_in']
    z, xbc, dt_raw, u_pool, q, kv, g = jnp.split(proj, IN_SPLITS, axis=-1)
    y_ssd, conv_new, ssm_new = ssd_mixer(z, xbc, dt_raw, conv_prev, ssm_prev, p['ssd_conv_w'], p['ssd_conv_b'],
                                         p['ssd_dt_bias'], p['ssd_a_log'], p['ssd_d'], p['ssd_norm'])
    y_pool, pool_new = pool_mixer(u_pool, pool_prev, pos0, p['pool_w'], p['pool_scale'])
    q = rope(q.reshape(b, t, NSA_H, NSA_HD), pos)
    kv = kv.reshape(b, t, 6, NSA_HD)
    rows = jnp.stack([rope(kv[:, :, 0::2], pos), kv[:, :, 1::2]], axis=3).reshape(b, t, 6, NSA_HD)
    cache_rows = rows[:, :, :4]
    win_ext = jnp.concatenate([win_prefix, rows[:, :, 4:]], axis=1)
    full = cache_rows if nsa_past is None else jnp.concatenate([nsa_past, cache_rows], axis=1)
    gates = jax.nn.sigmoid(g.reshape(b, t, NSA_H, 3))
    y_nsa = nsa_attend(q, gates, full, win_ext, pos0, p['nsa_pe_k'], p['nsa_pe_v'], p['nsa_w1_k'],
                       p['nsa_w1_v'], p['nsa_w2_k'], p['nsa_w2_v']).astype(x.dtype)
    new_win = win_ext[:, -win_keep:]
    mix = jnp.concatenate([y_ssd, y_pool, y_nsa.reshape(b, t, NSA_W)], axis=-1) @ p['w_out']
    x = x + rmsnorm(mix, p['norm_mix_post'])
    f, ffn_new = conv_ffn(rmsnorm(x, p['norm_ffn_pre']), ffn_prev, p['ffn_w_gate'], p['ffn_w_val'],
                          p['ffn_conv_w'], p['ffn_conv_b'], p['ffn_w_down'])
    x = x + rmsnorm(f, p['norm_ffn_post'])
    return x, (cache_rows, new_win, conv_new, ssm_new, pool_new, ffn_new)


def setup_inputs(seed: int = 0) -> dict:
    key = jax.random.key(seed)
    keys = iter(jax.random.split(key, 48))
    f32 = jnp.float32

    def nrm(shape, scale):
        return jax.random.normal(next(keys), shape, f32) * scale

    def gain(shape):
        return 1.0 + nrm(shape, 0.05)

    n_pages = PAST_LEN // PAGE_SIZE
    n_phys = (5 * DEC_BATCH * n_pages + 3) // 4
    win_buf = min(WINDOW, PAST_LEN)
    perm = jax.random.permutation(next(keys), n_phys)
    page_table = perm[:DEC_BATCH * n_pages].reshape(DEC_BATCH, n_pages).astype(jnp.int32)
    u = jax.random.uniform(next(keys), (DEPTH, SSD_H), f32)
    dt0 = jnp.exp(u * (math.log(0.1) - math.log(0.001)) + math.log(0.001))
    ssd_dt_bias = dt0 + jnp.log(-jnp.expm1(-dt0))
    ssd_a_log = jnp.log(jax.random.uniform(next(keys), (DEPTH, SSD_H), f32, 1.0, 16.0))
    return {
        'x_prompt': nrm((BATCH, SEQ, D_MODEL), 1.0),
        'x_sample': nrm((DEC_BATCH, DEC_SEQ, D_MODEL), 1.0),
        'cache_nsa_kv': nrm((DEPTH, n_phys, PAGE_SIZE, 4, NSA_HD), 1.0),
        'page_table': page_table,
        'state_nsa_win': nrm((DEPTH, DEC_BATCH, win_buf, 2, NSA_HD), 1.0),
        'state_ssd_conv': nrm((DEPTH, DEC_BATCH, SSD_CONV - 1, SSD_CONV_DIM), 1.0),
        'state_ssm': nrm((DEPTH, DEC_BATCH, SSD_H, SSD_HD, SSD_N), 0.1),
        'state_pool': nrm((DEPTH, DEC_BATCH, POOL_KEEP, POOL_W), 1.0),
        'state_ffn_conv': nrm((DEPTH, DEC_BATCH, FFN_CONV - 1, D_FF), 1.0),
        'norm_mix_pre': gain((DEPTH, D_MODEL)),
        'w_in': nrm((DEPTH, D_MODEL, IN_W), D_MODEL ** -0.5),
        'ssd_conv_w': nrm((DEPTH, SSD_CONV, SSD_CONV_DIM), SSD_CONV ** -0.5),
        'ssd_conv_b': nrm((DEPTH, SSD_CONV_DIM), 0.02),
        'ssd_dt_bias': ssd_dt_bias,
        'ssd_a_log': ssd_a_log,
        'ssd_d': gain((DEPTH, SSD_H)),
        'ssd_norm': gain((DEPTH, SSD_W)),
        'pool_w': nrm((DEPTH, len(POOL_WINDOWS), POOL_GC, POOL_GC), POOL_GC ** -0.5),
        'pool_scale': 1.0 + nrm((DEPTH, POOL_W), 0.1),
        'nsa_pe_k': nrm((DEPTH, CMP_LEN, NSA_HD), 0.1),
        'nsa_pe_v': nrm((DEPTH, CMP_LEN, NSA_HD), 0.1),
        'nsa_w1_k': nrm((DEPTH, CMP_LEN, NSA_HD, CMP_HID), (CMP_LEN * NSA_HD) ** -0.5),
        'nsa_w1_v': nrm((DEPTH, CMP_LEN, NSA_HD, CMP_HID), (CMP_LEN * NSA_HD) ** -0.5),
        'nsa_w2_k': nrm((DEPTH, CMP_HID, NSA_HD), CMP_HID ** -0.5),
        'nsa_w2_v': nrm((DEPTH, CMP_HID, NSA_HD), CMP_HID ** -0.5),
        'w_out': nrm((DEPTH, MIX_W, D_MODEL), MIX_W ** -0.5),
        'norm_mix_post': gain((DEPTH, D_MODEL)),
        'norm_ffn_pre': gain((DEPTH, D_MODEL)),
        'ffn_w_gate': nrm((DEPTH, D_MODEL, D_FF), D_MODEL ** -0.5),
        'ffn_w_val': nrm((DEPTH, D_MODEL, D_FF), D_MODEL ** -0.5),
        'ffn_conv_w': nrm((DEPTH, FFN_CONV, D_FF), FFN_CONV ** -0.5),
        'ffn_conv_b': nrm((DEPTH, D_FF), 0.02),
        'ffn_w_down': nrm((DEPTH, D_FF, D_MODEL), D_FF ** -0.5),
        'norm_ffn_post': gain((DEPTH, D_MODEL)),
    }


def reference(x_prompt, x_sample, cache_nsa_kv, page_table, state_nsa_win, state_ssd_conv, state_ssm,
              state_pool, state_ffn_conv, norm_mix_pre, w_in, ssd_conv_w, ssd_conv_b, ssd_dt_bias, ssd_a_log,
              ssd_d, ssd_norm, pool_w, pool_scale, nsa_pe_k, nsa_pe_v, nsa_w1_k, nsa_w1_v, nsa_w2_k, nsa_w2_v,
              w_out, norm_mix_post, norm_ffn_pre, ffn_w_gate, ffn_w_val, ffn_conv_w, ffn_conv_b, ffn_w_down,
              norm_ffn_post):
    params = dict(norm_mix_pre=norm_mix_pre, w_in=w_in, ssd_conv_w=ssd_conv_w, ssd_conv_b=ssd_conv_b,
                  ssd_dt_bias=ssd_dt_bias, ssd_a_log=ssd_a_log, ssd_d=ssd_d, ssd_norm=ssd_norm, pool_w=pool_w,
                  pool_scale=pool_scale, nsa_pe_k=nsa_pe_k, nsa_pe_v=nsa_pe_v, nsa_w1_k=nsa_w1_k,
                  nsa_w1_v=nsa_w1_v, nsa_w2_k=nsa_w2_k, nsa_w2_v=nsa_w2_v, w_out=w_out,
                  norm_mix_post=norm_mix_post, norm_ffn_pre=norm_ffn_pre, ffn_w_gate=ffn_w_gate,
                  ffn_w_val=ffn_w_val, ffn_conv_w=ffn_conv_w, ffn_conv_b=ffn_conv_b, ffn_w_down=ffn_w_down,
                  norm_ffn_post=norm_ffn_post)
    bp, tp, _ = x_prompt.shape
    bs = x_sample.shape[0]
    past_len = page_table.shape[1] * PAGE_SIZE
    win_buf = state_nsa_win.shape[2]
    dty = x_prompt.dtype
    win0 = jnp.zeros((bp, WINDOW, 2, NSA_HD), dty)
    conv0 = jnp.zeros((bp, SSD_CONV - 1, SSD_CONV_DIM), dty)
    ssm0 = jnp.zeros((bp, SSD_H, SSD_HD, SSD_N), dty)
    pool0 = jnp.zeros((bp, POOL_KEEP, POOL_W), dty)
    ffn0 = jnp.zeros((bp, FFN_CONV - 1, D_FF), dty)
    h_p = x_prompt
    h_s = x_sample
    outs_p = []
    outs_s = []
    for l in range(DEPTH):
        p = {k: v[l] for k, v in params.items()}
        h_p, st_p = hybrid_layer(h_p, 0, None, win0, min(WINDOW, tp), conv0, ssm0, pool0, ffn0, p)
        past = cache_nsa_kv[l][page_table].reshape(bs, past_len, 4, NSA_HD)
        prefix = jnp.pad(state_nsa_win[l], ((0, 0), (WINDOW - win_buf, 0), (0, 0), (0, 0)))
        h_s, st_s = hybrid_layer(h_s, past_len, past, prefix, win_buf, state_ssd_conv[l], state_ssm[l],
                                 state_pool[l], state_ffn_conv[l], p)
        outs_p.append(st_p)
        outs_s.append(st_s)

    def stk(outs, i):
        return jnp.stack([o[i] for o in outs])

    return (h_p, h_s, stk(outs_p, 0), stk(outs_s, 0), stk(outs_p, 1), stk(outs_s, 1), stk(outs_p, 2),
            stk(outs_s, 2), stk(outs_p, 3), stk(outs_s, 3), stk(outs_p, 4), stk(outs_s, 4), stk(outs_p, 5),
            stk(outs_s, 5))
```

```python
import functools
import math

import jax
import jax.numpy as jnp
from jax import lax
from jax.experimental import pallas as pl
from jax.experimental.pallas import tpu as pltpu

F32 = jnp.float32
BF16 = jnp.bfloat16
HIGHEST = lax.Precision.HIGHEST

D_MODEL = 1024
PAGE_SIZE = 128
SSD_W = 512
SSD_HD = 64
SSD_H = 8
SSD_N = 128
SSD_G = 2
SSD_CONV = 4
SSD_CONV_DIM = 1024
POOL_W = 256
POOL_WINDOWS = (2, 4, 8, 16)
POOL_GC = 64
POOL_KEEP = 15
NSA_W = 256
NSA_HD = 64
NSA_H = 4
CMP_STRIDE = 16
CMP_LEN = 32
CMP_HID = 128
SLC_BLOCK = 64
N_SELECT = 16
WINDOW = 512
ROPE_DIM = 16
ROPE_THETA = 500000.0
D_FF = 4096
FFN_CONV = 3
RMS_EPS = 1e-6
OFF_Z = 512
OFF_XBC = 1536
OFF_DT = 1544
OFF_POOL = 1800
OFF_Q = 2056
OFF_KV = 2440
IN_W = 2452
IN_W_PAD = 2560
CHUNK = 128
NEG = -1e30
VMEM_LIMIT = 56 * 1024 * 1024


def _cparams(sem):
    return pltpu.CompilerParams(dimension_semantics=sem, vmem_limit_bytes=VMEM_LIMIT)


def _rmsnorm(x, w):
    return x * lax.rsqrt(jnp.mean(x * x, axis=-1, keepdims=True) + RMS_EPS) * w


def _silu(x):
    return x / (1.0 + jnp.exp(-x))


def _gelu_tanh(x):
    return 0.5 * x * (1.0 + jnp.tanh(0.7978845608028654 * (x + 0.044715 * (x * x * x))))


def _dot(a, b):
    return jnp.dot(a, b, preferred_element_type=F32)


def _dot_nt(a, b):
    return lax.dot_general(a, b, (((1,), (1,)), ((), ())), preferred_element_type=F32)


def _dot_tn(a, b):
    return lax.dot_general(a, b, (((0,), (0,)), ((), ())), preferred_element_type=F32)


def _in_proj_body(x_ref, nw_ref, w_ref, tab_ref, z_ref, xbc_ref, pool_ref, q_ref, kvc_ref, kvw_ref, dtg_ref):
    tm = x_ref.shape[0]
    h = _rmsnorm(x_ref[...], nw_ref[...]).astype(BF16)

    def proj(a, b):
        return _dot(h, w_ref[0, :, a:b])

    z_ref[...] = proj(0, 512)
    xbc_ref[...] = proj(512, 1536)
    pool_ref[...] = proj(1536, 1792)
    dtg_ref[...] = proj(2432, 2560)
    lane = lax.broadcasted_iota(jnp.int32, (tm, 128), 1)
    first = (lane % NSA_HD) < (ROPE_DIM // 2)

    def rope(x, c, s):
        outs = []
        for j in range(x.shape[1] // 128):
            xj = x[:, j * 128:(j + 1) * 128]
            rot = jnp.where(first, pltpu.roll(xj, 128 - ROPE_DIM // 2, 1), pltpu.roll(xj, ROPE_DIM // 2, 1))
            outs.append(xj * c + rot * s)
        return outs

    q = rope(proj(1792, 2048), tab_ref[:, 0:128], tab_ref[:, 128:256])
    q_ref[:, 0:128] = q[0]
    q_ref[:, 128:256] = q[1]
    kv = rope(proj(2048, 2432), tab_ref[:, 256:384], tab_ref[:, 384:512])
    kvc_ref[:, 0:128] = kv[0]
    kvc_ref[:, 128:256] = kv[1]
    kvw_ref[...] = kv[2]


def _in_proj(x2d, nw, w_in_p, layer, tab, tm):
    m = x2d.shape[0]
    tab_blocks = tab.shape[0] // tm
    widths = (512, 1024, 256, 256, 256, 128, 128)
    return pl.pallas_call(
        _in_proj_body,
        grid=(m // tm,),
        in_specs=[
            pl.BlockSpec((tm, D_MODEL), lambda i: (i, 0)),
            pl.BlockSpec((1, D_MODEL), lambda i: (0, 0)),
            pl.BlockSpec((1, D_MODEL, IN_W_PAD), lambda i: (layer, 0, 0)),
            pl.BlockSpec((tm, 512), lambda i: (i % tab_blocks, 0)),
        ],
        out_specs=[pl.BlockSpec((tm, w), lambda i: (i, 0)) for w in widths],
        out_shape=[jax.ShapeDtypeStruct((m, w), F32) for w in widths],
        compiler_params=_cparams(("arbitrary",)),
        name="in_proj",
    )(x2d, nw, w_in_p, tab)


def _rope_table(pos):
    half = ROPE_DIM // 2
    inv = 1.0 / (ROPE_THETA ** (jnp.arange(half, dtype=F32) / half))
    ang = pos.astype(F32)[:, None] * inv
    cos, sin = jnp.cos(ang), jnp.sin(ang)
    t = pos.shape[0]
    c_rot = jnp.concatenate([cos, cos, jnp.ones((t, NSA_HD - ROPE_DIM), F32)], axis=1)
    s_rot = jnp.concatenate([-sin, sin, jnp.zeros((t, NSA_HD - ROPE_DIM), F32)], axis=1)
    one, zero = jnp.ones((t, NSA_HD), F32), jnp.zeros((t, NSA_HD), F32)
    return jnp.concatenate([c_rot, c_rot, s_rot, s_rot, c_rot, one, s_rot, zero], axis=1)


def _ssd_pool_body(lr, nc, pos0, xbc_ref, z_ref, dtg_ref, u_ref, cprev_ref, sprev_ref, pprev_ref,
                   cw_ref, cb_ref, dtb_ref, alog_ref, dsk_ref, nrm_ref, pw_ref, ps_ref,
                   yssd_ref, ypool_ref, snew_ref, ext_ref, state_ref, pext_ref):
    c = pl.program_id(1)
    L = CHUNK

    @pl.when(c == 0)
    def _():
        ext_ref[0:8, :] = cprev_ref[0]
        pext_ref[0:16, :] = pprev_ref[0]
        state_ref[...] = sprev_ref[0]

    if nc > 1:
        @pl.when(c > 0)
        def _():
            ext_ref[0:8, :] = ext_ref[L:L + 8, :]
            pext_ref[0:16, :] = pext_ref[L:L + 16, :]

    ext_ref[8:8 + lr, :] = xbc_ref[...]
    pext_ref[16:16 + lr, :] = u_ref[...]
    if lr < L:
        ext_ref[8 + lr:8 + L, :] = jnp.zeros((L - lr, SSD_CONV_DIM), F32)
        pext_ref[16 + lr:16 + L, :] = jnp.zeros((L - lr, POOL_W), F32)

    cw = cw_ref[...]
    conv = (ext_ref[8:8 + L, :] * cw[3:4, :] + ext_ref[7:7 + L, :] * cw[2:3, :]
            + ext_ref[6:6 + L, :] * cw[1:2, :] + ext_ref[5:5 + L, :] * cw[0:1, :] + cb_ref[...])
    xbc = _silu(conv)
    xs = xbc[:, 0:SSD_W]
    bmat = [xbc[:, SSD_W + g * SSD_N:SSD_W + (g + 1) * SSD_N].astype(BF16) for g in range(SSD_G)]
    cmat = [xbc[:, SSD_W + (SSD_G + g) * SSD_N:SSD_W + (SSD_G + g + 1) * SSD_N].astype(BF16)
            for g in range(SSD_G)]

    row = lax.broadcasted_iota(jnp.int32, (L, L), 0)
    col = lax.broadcasted_iota(jnp.int32, (L, L), 1)
    dtraw = jnp.zeros((L, 128), F32)
    if lr == L:
        dtraw = dtg_ref[...]
    else:
        dtraw = jnp.concatenate([dtg_ref[...], jnp.zeros((L - lr, 128), F32)], axis=0)
    xdt = dtraw + dtb_ref[...]
    dt = jnp.maximum(xdt, 0.0) + jnp.log1p(jnp.exp(-jnp.abs(xdt)))
    dt = jnp.where((col < SSD_H) & (row < lr), dt, 0.0)
    da = dt * (-jnp.exp(alog_ref[...]))
    tri = row >= col
    acum = jnp.dot(tri.astype(F32), da, precision=HIGHEST, preferred_element_type=F32)
    arow = acum.T
    cb_scores = [_dot_nt(cmat[g], bmat[g]) for g in range(SSD_G)]
    dsk = dsk_ref[...]
    ys = []
    for h in range(SSD_H):
        g = h // (SSD_H // SSD_G)
        ac = acum[:, h:h + 1]
        ar = arow[h:h + 1, :]
        decay = jnp.exp(jnp.where(tri, ac - ar, NEG))
        xh = xs[:, h * SSD_HD:(h + 1) * SSD_HD]
        xdt_h = xh * dt[:, h:h + 1]
        st = state_ref[h]
        y = _dot((cb_scores[g] * decay).astype(BF16), xdt_h.astype(BF16))
        y = y + _dot_nt(cmat[g], st.astype(BF16)) * jnp.exp(ac)
        alast = acum[L - 1:L, h:h + 1]
        xw = (xdt_h * jnp.exp(alast - ac)).astype(BF16)
        state_ref[h] = jnp.exp(alast) * st + _dot_tn(xw, bmat[g])
        ys.append(y + dsk[:, h * SSD_HD:(h + 1) * SSD_HD] * xh)
    y = jnp.concatenate(ys, axis=1)
    if lr < L:
        y = y[0:lr]
    y = y * _silu(z_ref[...])
    yssd_ref[...] = _rmsnorm(y, nrm_ref[...])

    @pl.when(c == nc - 1)
    def _():
        snew_ref[0] = state_ref[...]

    e1 = pext_ref[...]
    s2 = e1 + pltpu.roll(e1, 1, 0)
    s4 = s2 + pltpu.roll(s2, 2, 0)
    s8 = s4 + pltpu.roll(s4, 4, 0)
    s16 = s8 + pltpu.roll(s8, 8, 0)
    prow = lax.broadcasted_iota(jnp.int32, (L, POOL_W), 0)
    plane = lax.broadcasted_iota(jnp.int32, (L, POOL_W), 1)
    pos = pos0 + c * L + prow + 1
    pooled = jnp.zeros((L, POOL_W), F32)
    for gi, (w, s) in enumerate(zip(POOL_WINDOWS, (s2, s4, s8, s16))):
        cnt = jnp.minimum(pos, w).astype(F32)
        grp = (plane >= gi * POOL_GC) & (plane < (gi + 1) * POOL_GC)
        pooled = jnp.where(grp, s[16:16 + L] / cnt, pooled)
    pooled = pooled - e1[16:16 + L]
    yp = _dot(pooled.astype(BF16), pw_ref[...]) * ps_ref[...]
    ypool_ref[...] = yp[0:lr] if lr < L else yp


def _ssd_pool(xbc, z, dtg, u, cprev, sprev, pprev, wts, nb, nc, lr, pos0):
    m = xbc.shape[0]
    cw, cb, dtb, alog, dsk, nrm, pw, ps = wts

    def tile(w):
        return pl.BlockSpec((lr, w), lambda b, c: (b * nc + c, 0))

    def full2(a):
        return pl.BlockSpec(a.shape, lambda b, c: (0, 0))

    return pl.pallas_call(
        functools.partial(_ssd_pool_body, lr, nc, pos0),
        grid=(nb, nc),
        in_specs=[
            tile(SSD_CONV_DIM), tile(SSD_W), tile(128), tile(POOL_W),
            pl.BlockSpec((1, 8, SSD_CONV_DIM), lambda b, c: (b, 0, 0)),
            pl.BlockSpec((1, SSD_H, SSD_HD, SSD_N), lambda b, c: (b, 0, 0, 0)),
            pl.BlockSpec((1, 16, POOL_W), lambda b, c: (b, 0, 0)),
            full2(cw), full2(cb), full2(dtb), full2(alog), full2(dsk), full2(nrm), full2(pw), full2(ps),
        ],
        out_specs=[tile(SSD_W), tile(POOL_W),
                   pl.BlockSpec((1, SSD_H, SSD_HD, SSD_N), lambda b, c: (b, 0, 0, 0))],
        out_shape=[jax.ShapeDtypeStruct((m, SSD_W), F32), jax.ShapeDtypeStruct((m, POOL_W), F32),
                   jax.ShapeDtypeStruct((nb, SSD_H, SSD_HD, SSD_N), F32)],
        scratch_shapes=[pltpu.VMEM((CHUNK + 8, SSD_CONV_DIM), F32), pltpu.VMEM((SSD_H, SSD_HD, SSD_N), F32),
                        pltpu.VMEM((CHUNK + 16, POOL_W), F32)],
        compiler_params=_cparams(("arbitrary", "arbitrary")),
        name="ssd_pool",
    )(xbc, z, dtg, u, cprev, sprev, pprev, cw, cb, dtb, alog, dsk, nrm, pw, ps)


def _compress(load_rows, n_rows, w1_ref, pe_ref, w2_ref, pq_ref):
    acc = jnp.zeros((n_rows, 4 * CMP_HID), F32)
    cacc = jnp.zeros((8, 4 * CMP_HID), F32)
    for l in range(CMP_STRIDE):
        w = w1_ref[l]
        acc = acc + _dot(load_rows(l).astype(BF16), w)
        cacc = cacc + _dot(pe_ref[l].astype(BF16), w)
    pq_ref[0:n_rows, :] = acc
    pq_ref[n_rows:n_rows + 8, :] = jnp.zeros((8, 4 * CMP_HID), F32)
    const = cacc[0:1, 0:2 * CMP_HID] + cacc[1:2, 2 * CMP_HID:4 * CMP_HID]
    hid = pq_ref[0:n_rows, 0:2 * CMP_HID] + pq_ref[1:n_rows + 1, 2 * CMP_HID:4 * CMP_HID] + const
    return _dot(_gelu_tanh(hid).astype(BF16), w2_ref[...])


def _flash_step(carry, sc, mask, v):
    m, l, acc = carry
    m_new = jnp.maximum(m, jnp.max(jnp.where(mask, sc, NEG), axis=1, keepdims=True))
    p = jnp.where(mask, jnp.exp(sc - m_new), 0.0)
    alpha = jnp.exp(m - m_new)
    l = alpha * l + jnp.sum(p, axis=1, keepdims=True)
    acc = alpha * acc + _dot(p.astype(BF16), v)
    return m_new, l, acc


def _flash_init(rows):
    return jnp.full((rows, 1), NEG, F32), jnp.zeros((rows, 1), F32), jnp.zeros((rows, NSA_HD), F32)


def _flash_out(carry):
    _, l, acc = carry
    return acc / jnp.maximum(l, 1e-30)


def _stack_heads(x):
    return jnp.concatenate([x[:, h * NSA_HD:(h + 1) * NSA_HD] for h in range(NSA_H)], axis=0)


def _gated_mix(dtg, o_cmp, o_slc, o_win, r):
    sig = 1.0 / (1.0 + jnp.exp(-dtg))
    outs = []
    for h in range(NSA_H):
        b = SSD_H + 3 * h
        sl = slice(h * r, (h + 1) * r)
        outs.append(sig[:, b:b + 1] * o_cmp[sl] + sig[:, b + 1:b + 2] * o_slc[sl] + sig[:, b + 2:b + 3] * o_win[sl])
    return jnp.concatenate(outs, axis=1)


def _nsa_cmp_body(kv_ref, w1_ref, pe_ref, w2_ref, ckv_ref, pq_ref):
    n_ch = kv_ref.shape[0] // CMP_STRIDE
    ckv_ref[...] = _compress(lambda l: kv_ref[pl.ds(l, n_ch, stride=CMP_STRIDE), :], n_ch, w1_ref, pe_ref, w2_ref,
                             pq_ref)


def _nsa_cmp(kvc, w1, pe, w2, nb, t):
    n_ch = t // CMP_STRIDE
    return pl.pallas_call(
        _nsa_cmp_body,
        grid=(nb,),
        in_specs=[
            pl.BlockSpec((t, 128), lambda b: (b, 0)),
            pl.BlockSpec(w1.shape, lambda b: (0, 0, 0)),
            pl.BlockSpec(pe.shape, lambda b: (0, 0, 0)),
            pl.BlockSpec(w2.shape, lambda b: (0, 0)),
        ],
        out_specs=pl.BlockSpec((n_ch, 128), lambda b: (b, 0)),
        out_shape=jax.ShapeDtypeStruct((nb * n_ch, 128), F32),
        scratch_shapes=[pltpu.VMEM((n_ch + 8, 4 * CMP_HID), F32)],
        compiler_params=_cparams(("arbitrary",)),
        name="nsa_cmp",
    )(kvc, w1, pe, w2)


def _nsa_attn_body(t, q_ref, dtg_ref, ckv_ref, kvs_ref, kvw_ref, y_ref, selexp_ref):
    s = pl.program_id(1)
    qb = CHUNK
    rows = NSA_H * qb
    n_kc = t // qb
    n_cmp = t // CMP_STRIDE - 1
    n_slc = t // SLC_BLOCK
    nb_pad = -(-n_slc // 8) * 8
    n_sel = min(N_SELECT, n_slc)
    qs = _stack_heads(q_ref[...] * (NSA_HD ** -0.5)).astype(BF16)
    lane = lax.broadcasted_iota(jnp.int32, (rows, 128), 1)
    tpos = s * qb + lax.broadcasted_iota(jnp.int32, (rows, 128), 0) % qb

    ckv = ckv_ref[...]
    ck = ckv[:, 0:NSA_HD].astype(BF16)
    cv = ckv[:, NSA_HD:2 * NSA_HD].astype(BF16)
    sc = _dot_nt(qs, ck)
    cmask = (lane * CMP_STRIDE + (CMP_LEN - 1) <= tpos) & (lane < n_cmp)
    mx = jnp.max(jnp.where(cmask, sc, NEG), axis=1, keepdims=True)
    e = jnp.where(cmask, jnp.exp(sc - mx), 0.0)
    pc = e / jnp.maximum(jnp.sum(e, axis=1, keepdims=True), 1e-30)
    o_cmp = _dot(pc.astype(BF16), cv)
    pcsum = pc[0:qb] + pc[qb:2 * qb] + pc[2 * qb:3 * qb] + pc[3 * qb:4 * qb]

    jrow = lax.broadcasted_iota(jnp.int32, (nb_pad, 128), 0)
    ncol = lax.broadcasted_iota(jnp.int32, (nb_pad, 128), 1)
    ov_t = ((ncol * CMP_STRIDE < (jrow + 1) * SLC_BLOCK) & (ncol * CMP_STRIDE + CMP_LEN > jrow * SLC_BLOCK)
            & (ncol < n_cmp)).astype(F32)
    imp = lax.dot_general(ov_t, pcsum, (((1,), (1,)), ((), ())), precision=HIGHEST, preferred_element_type=F32)
    tq = s * qb + ncol
    cur = tq // SLC_BLOCK
    forced = (jrow == 0) | (jrow == cur) | (jrow == cur - 1)
    imp = jnp.where(forced, jnp.inf, imp)
    imp = jnp.where((jrow * SLC_BLOCK <= tq) & (jrow < n_slc), imp, -jnp.inf)
    rank = jnp.zeros((nb_pad, 128), F32)
    for i in range(n_slc):
        ri = imp[i:i + 1, :]
        before = (ri > imp) | ((ri == imp) & (jrow > i))
        rank = rank + before.astype(F32)
    sel_t = ((rank < n_sel) & (imp > -jnp.inf)).astype(F32)
    if nb_pad < 128:
        sel_t = jnp.concatenate([sel_t, jnp.zeros((128 - nb_pad, 128), F32)], axis=0)
    sel = sel_t.T.astype(BF16)
    erow = lax.broadcasted_iota(jnp.int32, (128, 128), 0)
    ecol = lax.broadcasted_iota(jnp.int32, (128, 128), 1)
    for c in range(n_kc):
        expand = (erow == (c * qb + ecol) // SLC_BLOCK).astype(BF16)
        selexp_ref[c] = _dot(sel, expand)

    def slc_step(c, carry):
        off = pl.multiple_of(c * qb, qb)
        k = kvs_ref[pl.ds(off, qb), 0:NSA_HD].astype(BF16)
        v = kvs_ref[pl.ds(off, qb), NSA_HD:2 * NSA_HD].astype(BF16)
        selc = selexp_ref[c]
        sel4 = jnp.concatenate([selc] * NSA_H, axis=0)
        mask = (sel4 > 0.5) & (c * qb + lane <= tpos)
        return _flash_step(carry, _dot_nt(qs, k), mask, v)

    o_slc = _flash_out(lax.fori_loop(0, s + 1, slc_step, _flash_init(rows)))

    carry = _flash_init(rows)
    for j in range(WINDOW // qb + 1):
        kc = s - WINDOW // qb + j
        off = pl.multiple_of(jnp.maximum(kc, 0) * qb, qb)
        k = kvw_ref[pl.ds(off, qb), 0:NSA_HD].astype(BF16)
        v = kvw_ref[pl.ds(off, qb), NSA_HD:2 * NSA_HD].astype(BF16)
        kpos = kc * qb + lane
        mask = (kpos >= 0) & (kpos <= tpos) & (kpos > tpos - WINDOW)
        carry = _flash_step(carry, _dot_nt(qs, k), mask, v)
    o_win = _flash_out(carry)

    y_ref[...] = _gated_mix(dtg_ref[...], o_cmp, o_slc, o_win, qb)


def _nsa_attn(q, dtg, ckv, kvc, kvw, nb, t):
    nq = t // CHUNK
    n_ch = t // CMP_STRIDE
    return pl.pallas_call(
        functools.partial(_nsa_attn_body, t),
        grid=(nb, nq),
        in_specs=[
            pl.BlockSpec((CHUNK, NSA_W), lambda b, s: (b * nq + s, 0)),
            pl.BlockSpec((CHUNK, 128), lambda b, s: (b * nq + s, 0)),
            pl.BlockSpec((n_ch, 128), lambda b, s: (b, 0)),
            pl.BlockSpec((t, 128), lambda b, s: (b, 1)),
            pl.BlockSpec((t, 128), lambda b, s: (b, 0)),
        ],
        out_specs=pl.BlockSpec((CHUNK, NSA_W), lambda b, s: (b * nq + s, 0)),
        out_shape=jax.ShapeDtypeStruct((nb * t, NSA_W), F32),
        scratch_shapes=[pltpu.VMEM((nq, CHUNK, CHUNK), F32)],
        compiler_params=_cparams(("arbitrary", "arbitrary")),
        name="nsa_attn",
    )(q, dtg, ckv, kvc, kvw)


def _nsa_dec_body(layer, n_phys, n_pages, nb, t, pt_ref, cache_ref, q_ref, dtg_ref, kvn_ref, kwn_ref, win_ref,
                  w1_ref, pe_ref, w2_ref, exp_ref, y_ref, xc_ref, xs_ref, pq_ref, wk_ref, sem):
    b = pl.program_id(0)
    past = n_pages * PAGE_SIZE
    lk = past + t
    n_ch = -(-lk // CMP_STRIDE)
    n_cmp = n_ch - 1
    nch_pad = xc_ref.shape[1] // CMP_STRIDE
    xrows = xc_ref.shape[1]
    n_slc = -(-lk // SLC_BLOCK)
    nblk = exp_ref.shape[0]
    n_sel = min(N_SELECT, n_slc)
    slot = b % 2

    def page_copies(bb, sl, p):
        phys = pt_ref[bb, p] + layer * n_phys
        dst = pl.ds(p * PAGE_SIZE, PAGE_SIZE)
        return (pltpu.make_async_copy(cache_ref.at[phys, :, pl.ds(0, 128)], xc_ref.at[sl, dst, :], sem.at[sl]),
                pltpu.make_async_copy(cache_ref.at[phys, :, pl.ds(128, 128)], xs_ref.at[sl, dst, :], sem.at[sl]))

    def start_all(bb, sl):
        def body(p, carry):
            for cp in page_copies(bb, sl, p):
                cp.start()
            return carry
        lax.fori_loop(0, n_pages, body, 0)

    @pl.when(b == 0)
    def _():
        start_all(0, 0)

    @pl.when(b + 1 < nb)
    def _():
        start_all(b + 1, 1 - slot)

    def wait_body(p, carry):
        for cp in page_copies(b, slot, p):
            cp.wait()
        return carry
    lax.fori_loop(0, n_pages, wait_body, 0)

    xc_ref[slot, past:past + t, :] = kvn_ref[:, 0:128]
    xs_ref[slot, past:past + t, :] = kvn_ref[:, 128:256]
    xc_ref[slot, past + t:xrows, :] = jnp.zeros((xrows - past - t, 128), F32)
    xs_ref[slot, past + t:xrows, :] = jnp.zeros((xrows - past - t, 128), F32)

    rows = NSA_H * t
    qs = _stack_heads(q_ref[...] * (NSA_HD ** -0.5)).astype(BF16)
    ti = lax.broadcasted_iota(jnp.int32, (rows, 1), 0) % t
    tpos = past + ti

    ckv = _compress(lambda l: xc_ref[slot, pl.ds(l, nch_pad, stride=CMP_STRIDE), :], nch_pad, w1_ref, pe_ref,
                    w2_ref, pq_ref)
    nc_use = (n_cmp + 127) // 128 * 128
    ck = ckv[0:nc_use, 0:NSA_HD].astype(BF16)
    cv = ckv[0:nc_use, NSA_HD:2 * NSA_HD].astype(BF16)
    sc = _dot_nt(qs, ck)
    ncol = lax.broadcasted_iota(jnp.int32, (rows, nc_use), 1)
    cmask = (ncol * CMP_STRIDE + (CMP_LEN - 1) <= tpos) & (ncol < n_cmp)
    mx = jnp.max(jnp.where(cmask, sc, NEG), axis=1, keepdims=True)
    e = jnp.where(cmask, jnp.exp(sc - mx), 0.0)
    pc = e / jnp.maximum(jnp.sum(e, axis=1, keepdims=True), 1e-30)
    o_cmp = _dot(pc.astype(BF16), cv)
    pcsum = pc[0:t] + pc[t:2 * t] + pc[2 * t:3 * t] + pc[3 * t:4 * t]

    nrow = lax.broadcasted_iota(jnp.int32, (nc_use, nblk), 0)
    jcol = lax.broadcasted_iota(jnp.int32, (nc_use, nblk), 1)
    ov = ((nrow * CMP_STRIDE < (jcol + 1) * SLC_BLOCK) & (nrow * CMP_STRIDE + CMP_LEN > jcol * SLC_BLOCK)
          & (nrow < n_cmp)).astype(F32)
    imp = jnp.dot(pcsum, ov, precision=HIGHEST, preferred_element_type=F32)
    jl = lax.broadcasted_iota(jnp.int32, (t, nblk), 1)
    tq = past + lax.broadcasted_iota(jnp.int32, (t, nblk), 0)
    cur = tq // SLC_BLOCK
    forced = (jl == 0) | (jl == cur) | (jl == cur - 1)
    imp = jnp.where(forced, jnp.inf, imp)
    imp = jnp.where((jl * SLC_BLOCK <= tq) & (jl < n_slc), imp, -jnp.inf)
    imp_t = jnp.concatenate([imp, jnp.full((128 - t, nblk), -jnp.inf, F32)], axis=0).T
    nb_rows = -(-n_slc // 8) * 8
    irow = lax.broadcasted_iota(jnp.int32, (nb_rows, nblk), 0)
    jlane = lax.broadcasted_iota(jnp.int32, (nb_rows, nblk), 1)
    ranks = []
    for qi in range(t):
        colv = imp_t[0:nb_rows, qi:qi + 1]
        rowv = imp[qi:qi + 1, :]
        before = (colv > rowv) | ((colv == rowv) & (irow < jlane))
        ranks.append(jnp.sum(before.astype(F32), axis=0, keepdims=True))
    rank = jnp.concatenate(ranks, axis=0)
    sel = ((rank < n_sel) & (imp > -jnp.inf)).astype(BF16)
    selexp = _dot(sel, exp_ref[...])
    sel4 = jnp.concatenate([selexp] * NSA_H, axis=0)

    k = xs_ref[slot, :, 0:NSA_HD].astype(BF16)
    v = xs_ref[slot, :, NSA_HD:2 * NSA_HD].astype(BF16)
    kpos = lax.broadcasted_iota(jnp.int32, (rows, xrows), 1)
    o_slc = _flash_out(_flash_step(_flash_init(rows), _dot_nt(qs, k), (sel4 > 0.5) & (kpos <= tpos), v))

    wrows = wk_ref.shape[0]
    wk_ref[0:WINDOW, :] = win_ref[0]
    wk_ref[WINDOW:WINDOW + t, :] = kwn_ref[...]
    wk_ref[WINDOW + t:wrows, :] = jnp.zeros((wrows - WINDOW - t, 2 * NSA_HD), F32)
    kw = wk_ref[:, 0:NSA_HD].astype(BF16)
    vw = wk_ref[:, NSA_HD:2 * NSA_HD].astype(BF16)
    wpos = past - WINDOW + lax.broadcasted_iota(jnp.int32, (rows, wrows), 1)
    wmask = (wpos >= 0) & (wpos <= tpos) & (wpos > tpos - WINDOW)
    o_win = _flash_out(_flash_step(_flash_init(rows), _dot_nt(qs, kw), wmask, vw))

    y_ref[...] = _gated_mix(dtg_ref[...], o_cmp, o_slc, o_win, t)


def _nsa_dec(page_table, cache, q, dtg, kvc, kvw, win, w1, pe, w2, expand, layer, n_phys, nb, t):
    n_pages = page_table.shape[1]
    xrows = expand.shape[1]
    wrows = -(-(WINDOW + t) // 128) * 128
    grid_spec = pltpu.PrefetchScalarGridSpec(
        num_scalar_prefetch=1,
        grid=(nb,),
        in_specs=[
            pl.BlockSpec(memory_space=pl.ANY),
            pl.BlockSpec((t, NSA_W), lambda b, pt: (b, 0)),
            pl.BlockSpec((t, 128), lambda b, pt: (b, 0)),
            pl.BlockSpec((t, 4 * NSA_HD), lambda b, pt: (b, 0)),
            pl.BlockSpec((t, 2 * NSA_HD), lambda b, pt: (b, 0)),
            pl.BlockSpec((1, WINDOW, 2 * NSA_HD), lambda b, pt: (layer * nb + b, 0, 0)),
            pl.BlockSpec(w1.shape, lambda b, pt: (0, 0, 0)),
            pl.BlockSpec(pe.shape, lambda b, pt: (0, 0, 0)),
            pl.BlockSpec(w2.shape, lambda b, pt: (0, 0)),
            pl.BlockSpec(expand.shape, lambda b, pt: (0, 0)),
        ],
        out_specs=pl.BlockSpec((t, NSA_W), lambda b, pt: (b, 0)),
        scratch_shapes=[
            pltpu.VMEM((2, xrows, 128), F32),
            pltpu.VMEM((2, xrows, 128), F32),
            pltpu.VMEM((xrows // CMP_STRIDE + 8, 4 * CMP_HID), F32),
            pltpu.VMEM((wrows, 2 * NSA_HD), F32),
            pltpu.SemaphoreType.DMA((2,)),
        ],
    )
    return pl.pallas_call(
        functools.partial(_nsa_dec_body, layer, n_phys, n_pages, nb, t),
        grid_spec=grid_spec,
        out_shape=jax.ShapeDtypeStruct((nb * t, NSA_W), F32),
        compiler_params=_cparams(("arbitrary",)),
        name="nsa_dec",
    )(page_table, cache, q, dtg, kvc, kvw, win, w1, pe, w2, expand)


def _out_proj_body(ys_ref, yp_ref, yn_ref, x_ref, w_ref, npost_ref, nffn_ref, x1_ref, h2_ref):
    mix = (_dot(ys_ref[...].astype(BF16), w_ref[0, 0:SSD_W, :])
           + _dot(yp_ref[...].astype(BF16), w_ref[0, SSD_W:SSD_W + POOL_W, :])
           + _dot(yn_ref[...].astype(BF16), w_ref[0, SSD_W + POOL_W:D_MODEL, :]))
    x1 = x_ref[...] + _rmsnorm(mix, npost_ref[...])
    x1_ref[...] = x1
    h2_ref[...] = _rmsnorm(x1, nffn_ref[...]).astype(BF16)


def _out_proj(ys, yp, yn, x2d, w_out, layer, npost, nffn, tm):
    m = x2d.shape[0]

    def tile(w):
        return pl.BlockSpec((tm, w), lambda i: (i, 0))

    vec = pl.BlockSpec((1, D_MODEL), lambda i: (0, 0))
    return pl.pallas_call(
        _out_proj_body,
        grid=(m // tm,),
        in_specs=[tile(SSD_W), tile(POOL_W), tile(NSA_W), tile(D_MODEL),
                  pl.BlockSpec((1, D_MODEL, D_MODEL), lambda i: (layer, 0, 0)), vec, vec],
        out_specs=[tile(D_MODEL), tile(D_MODEL)],
        out_shape=[jax.ShapeDtypeStruct((m, D_MODEL), F32), jax.ShapeDtypeStruct((m, D_MODEL), BF16)],
        compiler_params=_cparams(("arbitrary",)),
        name="out_proj",
    )(ys, yp, yn, x2d, w_out, npost, nffn)


def _ffn_body(carry_mode, tiles_per_seq, t, *refs):
    if carry_mode:
        (h2_ref, x1_ref, wg_ref, wv_ref, wd_ref, cw_ref, cb_ref, npost_ref,
         x2_ref, gsave_ref, acc_ref, gext_ref, carry_ref) = refs
    else:
        (h2_ref, x1_ref, wg_ref, wv_ref, wd_ref, cw_ref, cb_ref, npost_ref, p1_ref, p2_ref,
         x2_ref, gsave_ref, acc_ref) = refs
    m = pl.program_id(0)
    f = pl.program_id(1)
    nf = pl.num_programs(1)
    tm = h2_ref.shape[0]
    tf = wg_ref.shape[2]
    h2 = h2_ref[...]
    g = _dot(h2, wg_ref[0])
    v = _dot(h2, wv_ref[0])
    if carry_mode:
        first = (m % tiles_per_seq) == 0
        gext_ref[8:8 + tm, :] = g

        @pl.when(first)
        def _():
            gext_ref[0:8, :] = jnp.zeros((8, tf), F32)

        @pl.when(jnp.logical_not(first))
        def _():
            gext_ref[0:8, :] = carry_ref[f]

        carry_ref[f] = g[tm - 8:tm]
        gsave_ref[0] = g[tm - 8:tm]
        g1 = gext_ref[7:7 + tm, :]
        g2 = gext_ref[6:6 + tm, :]
    else:
        r = lax.broadcasted_iota(jnp.int32, (tm, tf), 0) % t
        g1 = jnp.where(r >= 1, pltpu.roll(g, 1, 0), p1_ref[...])
        g2 = jnp.where(r >= 2, pltpu.roll(g, 2, 0), p2_ref[...])
        gsave_ref[...] = g
    cw = cw_ref[...]
    gc = g * cw[2:3, :] + g1 * cw[1:2, :] + g2 * cw[0:1, :] + cb_ref[...]
    act = (_gelu_tanh(gc) * v).astype(BF16)
    contrib = _dot(act, wd_ref[0])

    @pl.when(f == 0)
    def _():
        acc_ref[...] = contrib

    @pl.when(f > 0)
    def _():
        acc_ref[...] = acc_ref[...] + contrib

    @pl.when(f == nf - 1)
    def _():
        x2_ref[...] = x1_ref[...] + _rmsnorm(acc_ref[...], npost_ref[...])


def _ffn(h2, x1, wg, wv, wd, layer, cw, cb, npost, tm, tf, carry_mode, tiles_per_seq, t, prev_rows=None):
    m = h2.shape[0]
    nm, nf = m // tm, D_FF // tf
    in_specs = [
        pl.BlockSpec((tm, D_MODEL), lambda i, j: (i, 0)),
        pl.BlockSpec((tm, D_MODEL), lambda i, j: (i, 0)),
        pl.BlockSpec((1, D_MODEL, tf), lambda i, j: (layer, 0, j)),
        pl.BlockSpec((1, D_MODEL, tf), lambda i, j: (layer, 0, j)),
        pl.BlockSpec((1, tf, D_MODEL), lambda i, j: (layer, j, 0)),
        pl.BlockSpec((FFN_CONV, tf), lambda i, j: (0, j)),
        pl.BlockSpec((1, tf), lambda i, j: (0, j)),
        pl.BlockSpec((1, D_MODEL), lambda i, j: (0, 0)),
    ]
    args = [h2, x1, wg, wv, wd, cw, cb, npost]
    scratch = [pltpu.VMEM((tm, D_MODEL), F32)]
    if carry_mode:
        gsave_shape = jax.ShapeDtypeStruct((nm, 8, D_FF), F32)
        gsave_spec = pl.BlockSpec((1, 8, tf), lambda i, j: (i, 0, j))
        scratch += [pltpu.VMEM((tm + 8, tf), F32), pltpu.VMEM((nf, 8, tf), F32)]
    else:
        gsave_shape = jax.ShapeDtypeStruct((m, D_FF), F32)
        gsave_spec = pl.BlockSpec((tm, tf), lambda i, j: (i, j))
        in_specs += [pl.BlockSpec((tm, tf), lambda i, j: (i, j))] * 2
        args += list(prev_rows)
    return pl.pallas_call(
        functools.partial(_ffn_body, carry_mode, tiles_per_seq, t),
        grid=(nm, nf),
        in_specs=in_specs,
        out_specs=[pl.BlockSpec((tm, D_MODEL), lambda i, j: (i, 0)), gsave_spec],
        out_shape=[jax.ShapeDtypeStruct((m, D_MODEL), F32), gsave_shape],
        scratch_shapes=scratch,
        compiler_params=_cparams(("arbitrary", "arbitrary")),
        name="ffn",
    )(*args)


def _prep_params(w_in, ssd_conv_w, ssd_conv_b, ssd_dt_bias, ssd_a_log, ssd_d, ssd_norm, pool_w, pool_scale,
                 nsa_pe_k, nsa_pe_v, nsa_w1_k, nsa_w1_v, nsa_w2_k, nsa_w2_v, w_out, ffn_w_gate, ffn_w_val,
                 ffn_w_down):
    depth = w_in.shape[0]
    w_in_p = jnp.concatenate(
        [w_in[..., 0:OFF_XBC], w_in[..., OFF_DT:OFF_KV], w_in[..., OFF_XBC:OFF_DT], w_in[..., OFF_KV:IN_W],
         jnp.zeros((depth, D_MODEL, IN_W_PAD - IN_W), F32)], axis=-1).astype(BF16)
    pad_h = ((0, 0), (0, 128 - SSD_H))
    dtb = jnp.pad(ssd_dt_bias, pad_h)[:, None, :]
    alog = jnp.pad(ssd_a_log, pad_h)[:, None, :]
    dsk = jnp.repeat(ssd_d, SSD_HD, axis=1)[:, None, :]
    n_g = len(POOL_WINDOWS)
    eye = jnp.eye(n_g, dtype=F32)
    pw_bd = (pool_w[:, :, :, None, :] * eye[None, :, None, :, None]).reshape(depth, POOL_W, POOL_W).astype(BF16)
    zk = jnp.zeros((depth, CMP_STRIDE, NSA_HD, CMP_HID), F32)
    top = jnp.concatenate([nsa_w1_k[:, :CMP_STRIDE], zk, nsa_w1_k[:, CMP_STRIDE:], zk], axis=-1)
    bot = jnp.concatenate([zk, nsa_w1_v[:, :CMP_STRIDE], zk, nsa_w1_v[:, CMP_STRIDE:]], axis=-1)
    w1 = jnp.concatenate([top, bot], axis=2).astype(BF16)
    pe_a = jnp.concatenate([nsa_pe_k[:, :CMP_STRIDE], nsa_pe_v[:, :CMP_STRIDE]], axis=-1)
    pe_b = jnp.concatenate([nsa_pe_k[:, CMP_STRIDE:], nsa_pe_v[:, CMP_STRIDE:]], axis=-1)
    pe = jnp.concatenate([pe_a[:, :, None], pe_b[:, :, None], jnp.zeros((depth, CMP_STRIDE, 6, 128), F32)], axis=2)
    zw = jnp.zeros((depth, CMP_HID, NSA_HD), F32)
    w2 = jnp.concatenate([jnp.concatenate([nsa_w2_k, zw], axis=-1), jnp.concatenate([zw, nsa_w2_v], axis=-1)],
                         axis=1).astype(BF16)
    return dict(w_in=w_in_p, cw=ssd_conv_w, cb=ssd_conv_b[:, None, :], dtb=dtb, alog=alog, dsk=dsk,
                nrm=ssd_norm[:, None, :], pw=pw_bd, ps=pool_scale[:, None, :], w1=w1, pe=pe, w2=w2,
                w_out=w_out.astype(BF16), wg=ffn_w_gate.astype(BF16), wv=ffn_w_val.astype(BF16),
                wd=ffn_w_down.astype(BF16))


def _mixer_weights(p, l):
    return (p['cw'][l], p['cb'][l], p['dtb'][l], p['alog'][l], p['dsk'][l], p['nrm'][l], p['pw'][l], p['ps'][l])


def kernel(x_prompt, x_sample, cache_nsa_kv, page_table, state_nsa_win, state_ssd_conv, state_ssm, state_pool, state_ffn_conv, norm_mix_pre, w_in, ssd_conv_w, ssd_conv_b, ssd_dt_bias, ssd_a_log, ssd_d, ssd_norm, pool_w, pool_scale, nsa_pe_k, nsa_pe_v, nsa_w1_k, nsa_w1_v, nsa_w2_k, nsa_w2_v, w_out, norm_mix_post, norm_ffn_pre, ffn_w_gate, ffn_w_val, ffn_conv_w, ffn_conv_b, ffn_w_down, norm_ffn_post):
    depth = w_in.shape[0]
    bp, tp, _ = x_prompt.shape
    bs, ts, _ = x_sample.shape
    n_phys = cache_nsa_kv.shape[1]
    n_pages = page_table.shape[1]
    past = n_pages * PAGE_SIZE
    assert tp % 512 == 0 and tp >= WINDOW and ts == 8 and state_nsa_win.shape[2] == WINDOW
    p = _prep_params(w_in, ssd_conv_w, ssd_conv_b, ssd_dt_bias, ssd_a_log, ssd_d, ssd_norm, pool_w, pool_scale,
                     nsa_pe_k, nsa_pe_v, nsa_w1_k, nsa_w1_v, nsa_w2_k, nsa_w2_v, w_out, ffn_w_gate, ffn_w_val,
                     ffn_w_down)
    ms = bs * ts
    tab_p = _rope_table(jnp.arange(tp, dtype=jnp.int32))
    tab_s = jnp.tile(_rope_table(past + jnp.arange(ts, dtype=jnp.int32)), (bs, 1))
    cache = cache_nsa_kv.reshape(depth * n_phys, PAGE_SIZE, 4 * NSA_HD)
    win_state = state_nsa_win.reshape(depth * bs, WINDOW, 2 * NSA_HD)
    lk = past + ts
    xrows = -(-(-(-lk // CMP_STRIDE)) // 8) * 8 * CMP_STRIDE
    xrows = -(-xrows // 128) * 128
    nblk = -(-(xrows // SLC_BLOCK) // 128) * 128
    expand = (jnp.arange(nblk, dtype=jnp.int32)[:, None]
              == jnp.arange(xrows, dtype=jnp.int32)[None, :] // SLC_BLOCK).astype(BF16)

    tm_p = 512
    tm_f = 1024 if tp % 1024 == 0 else 512
    tf = 512
    nc_p = tp // CHUNK
    zeros_c = jnp.zeros((bp, 8, SSD_CONV_DIM), F32)
    zeros_s = jnp.zeros((bp, SSD_H, SSD_HD, SSD_N), F32)
    zeros_p = jnp.zeros((bp, 16, POOL_W), F32)

    xp = x_prompt.reshape(bp * tp, D_MODEL)
    xs = x_sample.reshape(ms, D_MODEL)
    outs_p, outs_s = [], []
    for l in range(depth):
        nw = norm_mix_pre[l][None]
        npost = norm_mix_post[l][None]
        nffn = norm_ffn_pre[l][None]
        nfpost = norm_ffn_post[l][None]
        mw = _mixer_weights(p, l)
        fcw, fcb = ffn_conv_w[l], ffn_conv_b[l][None]

        z, xbc, u, q, kvc, kvw, dtg = _in_proj(xp, nw, p['w_in'], l, tab_p, tm_p)
        y_ssd, y_pool, ssm_new = _ssd_pool(xbc, z, dtg, u, zeros_c, zeros_s, zeros_p, mw, bp, nc_p, CHUNK, 0)
        ckv = _nsa_cmp(kvc, p['w1'][l], p['pe'][l], p['w2'][l], bp, tp)
        y_nsa = _nsa_attn(q, dtg, ckv, kvc, kvw, bp, tp)
        x1, h2 = _out_proj(y_ssd, y_pool, y_nsa, xp, p['w_out'], l, npost, nffn, tm_p)
        xp, gsave = _ffn(h2, x1, p['wg'], p['wv'], p['wd'], l, fcw, fcb, nfpost, tm_f, tf, True, tp // tm_f, tp)
        outs_p.append((
            kvc.reshape(bp, tp, 4, NSA_HD),
            kvw.reshape(bp, tp, 2, NSA_HD)[:, tp - WINDOW:],
            xbc.reshape(bp, tp, SSD_CONV_DIM)[:, tp - (SSD_CONV - 1):],
            ssm_new,
            u.reshape(bp, tp, POOL_W)[:, tp - POOL_KEEP:],
            gsave.reshape(bp, tp // tm_f, 8, D_FF)[:, -1, 8 - (FFN_CONV - 1):],
        ))

        z, xbc, u, q, kvc, kvw, dtg = _in_proj(xs, nw, p['w_in'], l, tab_s, ms)
        cprev = jnp.pad(state_ssd_conv[l], ((0, 0), (8 - (SSD_CONV - 1), 0), (0, 0)))
        pprev = jnp.pad(state_pool[l], ((0, 0), (16 - POOL_KEEP, 0), (0, 0)))
        y_ssd, y_pool, ssm_new = _ssd_pool(xbc, z, dtg, u, cprev, state_ssm[l], pprev, mw, bs, 1, ts, past)
        y_nsa = _nsa_dec(page_table, cache, q, dtg, kvc, kvw, win_state, p['w1'][l], p['pe'][l], p['w2'][l],
                         expand, l, n_phys, bs, ts)
        x1, h2 = _out_proj(y_ssd, y_pool, y_nsa, xs, p['w_out'], l, npost, nffn, ms)
        fprev = state_ffn_conv[l]
        zrow = jnp.zeros((bs, ts - 1, D_FF), F32)
        p1 = jnp.concatenate([fprev[:, 1:2], zrow], axis=1).reshape(ms, D_FF)
        p2 = jnp.concatenate([fprev, zrow[:, 1:]], axis=1).reshape(ms, D_FF)
        xs, gsave = _ffn(h2, x1, p['wg'], p['wv'], p['wd'], l, fcw, fcb, nfpost, ms, tf, False, 1, ts, (p1, p2))
        kvw3 = kvw.reshape(bs, ts, 2, NSA_HD)
        xbc3 = xbc.reshape(bs, ts, SSD_CONV_DIM)
        u3 = u.reshape(bs, ts, POOL_W)
        g3 = gsave.reshape(bs, ts, D_FF)
        outs_s.append((
            kvc.reshape(bs, ts, 4, NSA_HD),
            jnp.concatenate([state_nsa_win[l], kvw3], axis=1)[:, -WINDOW:],
            jnp.concatenate([state_ssd_conv[l], xbc3], axis=1)[:, -(SSD_CONV - 1):],
            ssm_new,
            jnp.concatenate([state_pool[l], u3], axis=1)[:, -POOL_KEEP:],
            jnp.concatenate([state_ffn_conv[l], g3], axis=1)[:, -(FFN_CONV - 1):],
        ))

    def stk(outs, i):
        return jnp.stack([o[i] for o in outs])

    return (xp.reshape(bp, tp, D_MODEL), xs.reshape(bs, ts, D_MODEL), stk(outs_p, 0), stk(outs_s, 0),
            stk(outs_p, 1), stk(outs_s, 1), stk(outs_p, 2), stk(outs_s, 2), stk(outs_p, 3), stk(outs_s, 3),
            stk(outs_p, 4), stk(outs_s, 4), stk(outs_p, 5), stk(outs_s, 5))
```

```python
import functools
import math

import jax
import jax.numpy as jnp
from jax import lax
from jax.experimental import pallas as pl
from jax.experimental.pallas import tpu as pltpu

F32 = jnp.float32
BF16 = jnp.bfloat16
HIGHEST = lax.Precision.HIGHEST

D_MODEL = 1024
PAGE_SIZE = 128
SSD_W = 512
SSD_HD = 64
SSD_H = 8
SSD_N = 128
SSD_G = 2
SSD_CONV = 4
SSD_CONV_DIM = 1024
POOL_W = 256
POOL_WINDOWS = (2, 4, 8, 16)
POOL_GC = 64
POOL_KEEP = 15
NSA_W = 256
NSA_HD = 64
NSA_H = 4
CMP_STRIDE = 16
CMP_LEN = 32
CMP_HID = 128
SLC_BLOCK = 64
N_SELECT = 16
WINDOW = 512
ROPE_DIM = 16
ROPE_THETA = 500000.0
D_FF = 4096
FFN_CONV = 3
RMS_EPS = 1e-6
OFF_Z = 512
OFF_XBC = 1536
OFF_DT = 1544
OFF_POOL = 1800
OFF_Q = 2056
OFF_KV = 2440
IN_W = 2452
IN_W_PAD = 2560
CHUNK = 128
NEG = -1e30
VMEM_LIMIT = 56 * 1024 * 1024


def _cparams(sem):
    return pltpu.CompilerParams(dimension_semantics=sem, vmem_limit_bytes=VMEM_LIMIT)


def _rmsnorm(x, w):
    return x * lax.rsqrt(jnp.mean(x * x, axis=-1, keepdims=True) + RMS_EPS) * w


def _silu(x):
    return x / (1.0 + jnp.exp(-x))


def _gelu_tanh(x):
    k = 0.7978845608028654
    hx = 0.5 * x
    return hx + hx * jnp.tanh(x * (k + (k * 0.044715) * (x * x)))


def _dot(a, b):
    return jnp.dot(a, b, preferred_element_type=F32)


def _dot_nt(a, b):
    return lax.dot_general(a, b, (((1,), (1,)), ((), ())), preferred_element_type=F32)


def _dot_tn(a, b):
    return lax.dot_general(a, b, (((0,), (0,)), ((), ())), preferred_element_type=F32)


def _in_proj_body(x_ref, nw_ref, w_ref, tab_ref, z_ref, xbc_ref, pool_ref, q_ref, kvc_ref, kvw_ref, dtg_ref):
    tm = x_ref.shape[0]
    h = _rmsnorm(x_ref[...], nw_ref[...]).astype(BF16)

    def proj(a, b):
        return _dot(h, w_ref[0, :, a:b])

    z_ref[...] = proj(0, 512)
    xbc_ref[...] = proj(512, 1536)
    pool_ref[...] = proj(1536, 1792)
    dtg_ref[...] = proj(2432, 2560)
    lane = lax.broadcasted_iota(jnp.int32, (tm, 128), 1)
    first = (lane % NSA_HD) < (ROPE_DIM // 2)

    def rope(x, c, s):
        outs = []
        for j in range(x.shape[1] // 128):
            xj = x[:, j * 128:(j + 1) * 128]
            rot = jnp.where(first, pltpu.roll(xj, 128 - ROPE_DIM // 2, 1), pltpu.roll(xj, ROPE_DIM // 2, 1))
            outs.append(xj * c + rot * s)
        return outs

    q = rope(proj(1792, 2048), tab_ref[:, 0:128], tab_ref[:, 128:256])
    q_ref[:, 0:128] = q[0]
    q_ref[:, 128:256] = q[1]
    kv = rope(proj(2048, 2432), tab_ref[:, 256:384], tab_ref[:, 384:512])
    kvc_ref[:, 0:128] = kv[0]
    kvc_ref[:, 128:256] = kv[1]
    kvw_ref[...] = kv[2]


def _in_proj(x2d, nw, w_in_p, layer, tab, tm):
    m = x2d.shape[0]
    tab_blocks = tab.shape[0] // tm
    widths = (512, 1024, 256, 256, 256, 128, 128)
    return pl.pallas_call(
        _in_proj_body,
        grid=(m // tm,),
        in_specs=[
            pl.BlockSpec((tm, D_MODEL), lambda i: (i, 0)),
            pl.BlockSpec((1, D_MODEL), lambda i: (0, 0)),
            pl.BlockSpec((1, D_MODEL, IN_W_PAD), lambda i: (layer, 0, 0)),
            pl.BlockSpec((tm, 512), lambda i: (i % tab_blocks, 0)),
        ],
        out_specs=[pl.BlockSpec((tm, w), lambda i: (i, 0)) for w in widths],
        out_shape=[jax.ShapeDtypeStruct((m, w), F32) for w in widths],
        compiler_params=_cparams(("arbitrary",)),
        name="in_proj",
    )(x2d, nw, w_in_p, tab)


def _rope_table(pos):
    half = ROPE_DIM // 2
    inv = 1.0 / (ROPE_THETA ** (jnp.arange(half, dtype=F32) / half))
    ang = pos.astype(F32)[:, None] * inv
    cos, sin = jnp.cos(ang), jnp.sin(ang)
    t = pos.shape[0]
    c_rot = jnp.concatenate([cos, cos, jnp.ones((t, NSA_HD - ROPE_DIM), F32)], axis=1)
    s_rot = jnp.concatenate([-sin, sin, jnp.zeros((t, NSA_HD - ROPE_DIM), F32)], axis=1)
    one, zero = jnp.ones((t, NSA_HD), F32), jnp.zeros((t, NSA_HD), F32)
    return jnp.concatenate([c_rot, c_rot, s_rot, s_rot, c_rot, one, s_rot, zero], axis=1)


def _ssd_pool_body(lr, nc, pos0, xbc_ref, z_ref, dtg_ref, u_ref, cprev_ref, sprev_ref, pprev_ref,
                   cw_ref, cb_ref, dtb_ref, alog_ref, dsk_ref, nrm_ref, pw_ref, ps_ref,
                   yssd_ref, ypool_ref, snew_ref, ext_ref, state_ref, pext_ref):
    c = pl.program_id(1)
    L = CHUNK

    @pl.when(c == 0)
    def _():
        ext_ref[0:8, :] = cprev_ref[0]
        pext_ref[0:16, :] = pprev_ref[0]
        state_ref[...] = sprev_ref[0]

    if nc > 1:
        @pl.when(c > 0)
        def _():
            ext_ref[0:8, :] = ext_ref[L:L + 8, :]
            pext_ref[0:16, :] = pext_ref[L:L + 16, :]

    ext_ref[8:8 + lr, :] = xbc_ref[...]
    pext_ref[16:16 + lr, :] = u_ref[...]
    if lr < L:
        ext_ref[8 + lr:8 + L, :] = jnp.zeros((L - lr, SSD_CONV_DIM), F32)
        pext_ref[16 + lr:16 + L, :] = jnp.zeros((L - lr, POOL_W), F32)

    cw = cw_ref[...]
    conv = (ext_ref[8:8 + L, :] * cw[3:4, :] + ext_ref[7:7 + L, :] * cw[2:3, :]
            + ext_ref[6:6 + L, :] * cw[1:2, :] + ext_ref[5:5 + L, :] * cw[0:1, :] + cb_ref[...])
    xbc = _silu(conv)
    xs = xbc[:, 0:SSD_W]
    bmat = [xbc[:, SSD_W + g * SSD_N:SSD_W + (g + 1) * SSD_N].astype(BF16) for g in range(SSD_G)]
    cmat = [xbc[:, SSD_W + (SSD_G + g) * SSD_N:SSD_W + (SSD_G + g + 1) * SSD_N].astype(BF16)
            for g in range(SSD_G)]

    row = lax.broadcasted_iota(jnp.int32, (L, L), 0)
    col = lax.broadcasted_iota(jnp.int32, (L, L), 1)
    dtraw = jnp.zeros((L, 128), F32)
    if lr == L:
        dtraw = dtg_ref[...]
    else:
        dtraw = jnp.concatenate([dtg_ref[...], jnp.zeros((L - lr, 128), F32)], axis=0)
    xdt = dtraw + dtb_ref[...]
    dt = jnp.maximum(xdt, 0.0) + jnp.log1p(jnp.exp(-jnp.abs(xdt)))
    dt = jnp.where((col < SSD_H) & (row < lr), dt, 0.0)
    da = dt * (-jnp.exp(alog_ref[...]))
    tri = row >= col
    acum = jnp.dot(tri.astype(F32), da, precision=HIGHEST, preferred_element_type=F32)
    arow = acum.T
    cb_scores = [_dot_nt(cmat[g], bmat[g]) for g in range(SSD_G)]
    dsk = dsk_ref[...]
    ys = []
    for h in range(SSD_H):
        g = h // (SSD_H // SSD_G)
        ac = acum[:, h:h + 1]
        ar = arow[h:h + 1, :]
        decay = jnp.exp(jnp.where(tri, ac - ar, NEG))
        xh = xs[:, h * SSD_HD:(h + 1) * SSD_HD]
        xdt_h = xh * dt[:, h:h + 1]
        st = state_ref[h]
        y = _dot((cb_scores[g] * decay).astype(BF16), xdt_h.astype(BF16))
        y = y + _dot_nt(cmat[g], st.astype(BF16)) * jnp.exp(ac)
        alast = acum[L - 1:L, h:h + 1]
        xw = (xdt_h * jnp.exp(alast - ac)).astype(BF16)
        state_ref[h] = jnp.exp(alast) * st + _dot_tn(xw, bmat[g])
        ys.append(y + dsk[:, h * SSD_HD:(h + 1) * SSD_HD] * xh)
    y = jnp.concatenate(ys, axis=1)
    if lr < L:
        y = y[0:lr]
    y = y * _silu(z_ref[...])
    yssd_ref[...] = _rmsnorm(y, nrm_ref[...])

    @pl.when(c == nc - 1)
    def _():
        snew_ref[0] = state_ref[...]

    e1 = pext_ref[...]
    s2 = e1 + pltpu.roll(e1, 1, 0)
    s4 = s2 + pltpu.roll(s2, 2, 0)
    s8 = s4 + pltpu.roll(s4, 4, 0)
    s16 = s8 + pltpu.roll(s8, 8, 0)
    prow = lax.broadcasted_iota(jnp.int32, (L, POOL_W), 0)
    plane = lax.broadcasted_iota(jnp.int32, (L, POOL_W), 1)
    pos = pos0 + c * L + prow + 1
    pooled = jnp.zeros((L, POOL_W), F32)
    for gi, (w, s) in enumerate(zip(POOL_WINDOWS, (s2, s4, s8, s16))):
        cnt = jnp.minimum(pos, w).astype(F32)
        grp = (plane >= gi * POOL_GC) & (plane < (gi + 1) * POOL_GC)
        pooled = jnp.where(grp, s[16:16 + L] / cnt, pooled)
    pooled = pooled - e1[16:16 + L]
    yp = _dot(pooled.astype(BF16), pw_ref[...]) * ps_ref[...]
    ypool_ref[...] = yp[0:lr] if lr < L else yp


def _ssd_pool(xbc, z, dtg, u, cprev, sprev, pprev, wts, nb, nc, lr, pos0):
    m = xbc.shape[0]
    cw, cb, dtb, alog, dsk, nrm, pw, ps = wts

    def tile(w):
        return pl.BlockSpec((lr, w), lambda b, c: (b * nc + c, 0))

    def full2(a):
        return pl.BlockSpec(a.shape, lambda b, c: (0, 0))

    return pl.pallas_call(
        functools.partial(_ssd_pool_body, lr, nc, pos0),
        grid=(nb, nc),
        in_specs=[
            tile(SSD_CONV_DIM), tile(SSD_W), tile(128), tile(POOL_W),
            pl.BlockSpec((1, 8, SSD_CONV_DIM), lambda b, c: (b, 0, 0)),
            pl.BlockSpec((1, SSD_H, SSD_HD, SSD_N), lambda b, c: (b, 0, 0, 0)),
            pl.BlockSpec((1, 16, POOL_W), lambda b, c: (b, 0, 0)),
            full2(cw), full2(cb), full2(dtb), full2(alog), full2(dsk), full2(nrm), full2(pw), full2(ps),
        ],
        out_specs=[tile(SSD_W), tile(POOL_W),
                   pl.BlockSpec((1, SSD_H, SSD_HD, SSD_N), lambda b, c: (b, 0, 0, 0))],
        out_shape=[jax.ShapeDtypeStruct((m, SSD_W), F32), jax.ShapeDtypeStruct((m, POOL_W), F32),
                   jax.ShapeDtypeStruct((nb, SSD_H, SSD_HD, SSD_N), F32)],
        scratch_shapes=[pltpu.VMEM((CHUNK + 8, SSD_CONV_DIM), F32), pltpu.VMEM((SSD_H, SSD_HD, SSD_N), F32),
                        pltpu.VMEM((CHUNK + 16, POOL_W), F32)],
        compiler_params=_cparams(("arbitrary", "arbitrary")),
        name="ssd_pool",
    )(xbc, z, dtg, u, cprev, sprev, pprev, cw, cb, dtb, alog, dsk, nrm, pw, ps)


def _compress(load_rows, n_rows, w1_ref, pe_ref, w2_ref, pq_ref):
    acc = jnp.zeros((n_rows, 4 * CMP_HID), F32)
    cacc = jnp.zeros((8, 4 * CMP_HID), F32)
    for l in range(CMP_STRIDE):
        w = w1_ref[l]
        acc = acc + _dot(load_rows(l).astype(BF16), w)
        cacc = cacc + _dot(pe_ref[l].astype(BF16), w)
    pq_ref[0:n_rows, :] = acc
    pq_ref[n_rows:n_rows + 8, :] = jnp.zeros((8, 4 * CMP_HID), F32)
    const = cacc[0:1, 0:2 * CMP_HID] + cacc[1:2, 2 * CMP_HID:4 * CMP_HID]
    hid = pq_ref[0:n_rows, 0:2 * CMP_HID] + pq_ref[1:n_rows + 1, 2 * CMP_HID:4 * CMP_HID] + const
    return _dot(_gelu_tanh(hid).astype(BF16), w2_ref[...])


def _stack_heads(x):
    return jnp.concatenate([x[:, h * NSA_HD:(h + 1) * NSA_HD] for h in range(NSA_H)], axis=0)


def _gated_mix(dtg, o_cmp, o_slc, o_win, r):
    sig = 1.0 / (1.0 + jnp.exp(-dtg))
    outs = []
    for h in range(NSA_H):
        b = SSD_H + 3 * h
        sl = slice(h * r, (h + 1) * r)
        outs.append(sig[:, b:b + 1] * o_cmp[sl] + sig[:, b + 1:b + 2] * o_slc[sl] + sig[:, b + 2:b + 3] * o_win[sl])
    return jnp.concatenate(outs, axis=1)


def _nsa_cmp_body(kv_ref, w1_ref, pe_ref, w2_ref, ckv_ref, pq_ref):
    n_ch = kv_ref.shape[0] // CMP_STRIDE
    ckv_ref[...] = _compress(lambda l: kv_ref[pl.ds(l, n_ch, stride=CMP_STRIDE), :], n_ch, w1_ref, pe_ref, w2_ref,
                             pq_ref)


def _nsa_cmp(kvc, w1, pe, w2, nb, t):
    n_ch = t // CMP_STRIDE
    return pl.pallas_call(
        _nsa_cmp_body,
        grid=(nb,),
        in_specs=[
            pl.BlockSpec((t, 128), lambda b: (b, 0)),
            pl.BlockSpec(w1.shape, lambda b: (0, 0, 0)),
            pl.BlockSpec(pe.shape, lambda b: (0, 0, 0)),
            pl.BlockSpec(w2.shape, lambda b: (0, 0)),
        ],
        out_specs=pl.BlockSpec((n_ch, 128), lambda b: (b, 0)),
        out_shape=jax.ShapeDtypeStruct((nb * n_ch, 128), F32),
        scratch_shapes=[pltpu.VMEM((n_ch + 8, 4 * CMP_HID), F32)],
        compiler_params=_cparams(("arbitrary",)),
        name="nsa_cmp",
    )(kvc, w1, pe, w2)


def _flash_step_t(carry, st, mask, vt):
    m, l, acc = carry
    sm = jnp.where(mask, st, NEG)
    m_new = jnp.maximum(m, jnp.max(sm, axis=0, keepdims=True))
    p = jnp.exp(sm - m_new)
    alpha = jnp.exp(m - m_new)
    l = alpha * l + jnp.sum(p, axis=0, keepdims=True)
    acc = alpha * acc + _dot(vt, p.astype(BF16))
    return m_new, l, acc


def _flash_init_t(cols):
    return jnp.full((1, cols), NEG, F32), jnp.zeros((1, cols), F32), jnp.zeros((NSA_HD, cols), F32)


def _nsa_attn_body(t, q_ref, dtg_ref, ckv_ref, kvs_ref, kvw_ref, y_ref, ks_ref, kst_ref, kw_ref, kwt_ref, selt_ref):
    s = pl.program_id(1)
    qb = CHUNK
    cols = NSA_H * qb
    n_kc = t // qb
    n_ch = t // CMP_STRIDE
    n_cmp = n_ch - 1
    n_slc = t // SLC_BLOCK
    nb_pad = -(-n_slc // 8) * 8
    n_sel = min(N_SELECT, n_slc)

    kb = ks_ref.shape[1] // qb
    n_wc = WINDOW // qb + 1

    @pl.when(s == 0)
    def _():
        for c in range(n_kc):
            tile = kvs_ref[c * qb:(c + 1) * qb, :]
            ks_ref[c // kb, (c % kb) * qb:(c % kb + 1) * qb, :] = tile.astype(BF16)
            kst_ref[c // kb, :, (c % kb) * qb:(c % kb + 1) * qb] = tile.T.astype(BF16)
            tile = kvw_ref[c * qb:(c + 1) * qb, :]
            kw_ref[c] = tile.astype(BF16)
            kwt_ref[c] = tile.T.astype(BF16)

    q_t = (q_ref[...] * (NSA_HD ** -0.5)).T
    qs_t = jnp.concatenate([q_t[h * NSA_HD:(h + 1) * NSA_HD, :] for h in range(NSA_H)], axis=1).astype(BF16)
    lane = lax.broadcasted_iota(jnp.int32, (1, cols), 1)
    tpos = s * qb + lane % qb

    ckv = ckv_ref[...]
    ck = ckv[:, 0:NSA_HD].astype(BF16)
    cv_t = ckv.T[NSA_HD:2 * NSA_HD, :].astype(BF16)
    sc = _dot(ck, qs_t)
    nrow = lax.broadcasted_iota(jnp.int32, (n_ch, cols), 0)
    cmask = (nrow * CMP_STRIDE + (CMP_LEN - 1) <= tpos) & (nrow < n_cmp)
    mx = jnp.max(jnp.where(cmask, sc, NEG), axis=0, keepdims=True)
    e = jnp.where(cmask, jnp.exp(sc - mx), 0.0)
    pc = e / jnp.maximum(jnp.sum(e, axis=0, keepdims=True), 1e-30)
    o_cmp = _dot(cv_t, pc.astype(BF16))
    pcsum = pc[:, 0:qb] + pc[:, qb:2 * qb] + pc[:, 2 * qb:3 * qb] + pc[:, 3 * qb:4 * qb]

    jrow = lax.broadcasted_iota(jnp.int32, (nb_pad, n_ch), 0)
    ncol = lax.broadcasted_iota(jnp.int32, (nb_pad, n_ch), 1)
    ov_t = ((ncol * CMP_STRIDE < (jrow + 1) * SLC_BLOCK) & (ncol * CMP_STRIDE + CMP_LEN > jrow * SLC_BLOCK)
            & (ncol < n_cmp)).astype(F32)
    imp = jnp.dot(ov_t, pcsum, precision=HIGHEST, preferred_element_type=F32)
    brow = lax.broadcasted_iota(jnp.int32, (nb_pad, qb), 0)
    tq = s * qb + lax.broadcasted_iota(jnp.int32, (nb_pad, qb), 1)
    cur = tq // SLC_BLOCK
    forced = (brow == 0) | (brow == cur) | (brow == cur - 1)
    imp = jnp.where(forced, jnp.inf, imp)
    imp = jnp.where((brow * SLC_BLOCK <= tq) & (brow < n_slc), imp, -jnp.inf)
    rank = jnp.zeros((nb_pad, qb), F32)
    for i in range(n_slc):
        ri = imp[i:i + 1, :]
        before = (ri > imp) | ((ri == imp) & (brow > i))
        rank = rank + before.astype(F32)
    sel_t = ((rank < n_sel) & (imp > -jnp.inf)).astype(F32)
    selt_ref[...] = jnp.concatenate([sel_t] * NSA_H, axis=1)

    bps = kb * qb // SLC_BLOCK
    krow_s = lax.broadcasted_iota(jnp.int32, (kb * qb, cols), 0)

    def slc_step(c, carry):
        k = ks_ref[c, :, 0:NSA_HD]
        vt = kst_ref[c, NSA_HD:2 * NSA_HD, :]
        sel_rows = selt_ref[pl.ds(pl.multiple_of(c * bps, bps), bps), :]
        selm = jnp.concatenate([jnp.broadcast_to(sel_rows[j:j + 1, :], (SLC_BLOCK, cols)) for j in range(bps)],
                               axis=0)
        mask = (selm > 0.5) & (c * (kb * qb) + krow_s <= tpos)
        return _flash_step_t(carry, _dot(k, qs_t), mask, vt)

    _, l_s, acc_s = lax.fori_loop(0, (s + kb) // kb, slc_step, _flash_init_t(cols))
    o_slc = acc_s / l_s

    k0 = jnp.maximum(s - (n_wc - 1), 0)
    kw = jnp.concatenate([kw_ref[k0 + i, :, 0:NSA_HD] for i in range(n_wc)], axis=0)
    vwt = jnp.concatenate([kwt_ref[k0 + i, NSA_HD:2 * NSA_HD, :] for i in range(n_wc)], axis=1)
    kpos = k0 * qb + lax.broadcasted_iota(jnp.int32, (n_wc * qb, cols), 0)
    sw = jnp.where((kpos <= tpos) & (kpos > tpos - WINDOW), _dot(kw, qs_t), NEG)
    pw = jnp.exp(sw - jnp.max(sw, axis=0, keepdims=True))
    o_win = _dot(vwt, pw.astype(BF16)) / jnp.sum(pw, axis=0, keepdims=True)

    sig = 1.0 / (1.0 + jnp.exp(-dtg_ref[...].T[0:32, :]))
    outs = []
    for h in range(NSA_H):
        b = SSD_H + 3 * h
        sl = slice(h * qb, (h + 1) * qb)
        outs.append(sig[b:b + 1, :] * o_cmp[:, sl] + sig[b + 1:b + 2, :] * o_slc[:, sl]
                    + sig[b + 2:b + 3, :] * o_win[:, sl])
    y_ref[...] = jnp.concatenate(outs, axis=0).T


def _nsa_attn(q, dtg, ckv, kvc, kvw, nb, t):
    nq = t // CHUNK
    n_ch = t // CMP_STRIDE
    kb = 4
    assert n_ch % 128 == 0 and nq % kb == 0 and nq > WINDOW // CHUNK
    nb_pad = -(-(t // SLC_BLOCK) // 8) * 8
    kv_scratch = pltpu.VMEM((nq, CHUNK, 2 * NSA_HD), BF16)
    return pl.pallas_call(
        functools.partial(_nsa_attn_body, t),
        grid=(nb, nq),
        in_specs=[
            pl.BlockSpec((CHUNK, NSA_W), lambda b, s: (b * nq + s, 0)),
            pl.BlockSpec((CHUNK, 128), lambda b, s: (b * nq + s, 0)),
            pl.BlockSpec((n_ch, 128), lambda b, s: (b, 0)),
            pl.BlockSpec((t, 128), lambda b, s: (b, 1)),
            pl.BlockSpec((t, 128), lambda b, s: (b, 0)),
        ],
        out_specs=pl.BlockSpec((CHUNK, NSA_W), lambda b, s: (b * nq + s, 0)),
        out_shape=jax.ShapeDtypeStruct((nb * t, NSA_W), F32),
        scratch_shapes=[pltpu.VMEM((nq // kb, kb * CHUNK, 2 * NSA_HD), BF16),
                        pltpu.VMEM((nq // kb, 2 * NSA_HD, kb * CHUNK), BF16),
                        kv_scratch, kv_scratch, pltpu.VMEM((nb_pad, NSA_H * CHUNK), F32)],
        compiler_params=_cparams(("arbitrary", "arbitrary")),
        name="nsa_attn",
    )(q, dtg, ckv, kvc, kvw)


def _softmax_pv_nt(sc, mask, vt):
    mx = jnp.max(jnp.where(mask, sc, NEG), axis=1, keepdims=True)
    e = jnp.where(mask, jnp.exp(sc - mx), 0.0)
    l = jnp.sum(e, axis=1, keepdims=True)
    return _dot_nt(e.astype(BF16), vt) / jnp.maximum(l, 1e-30)


def _rows_to_cols(x):
    r, w = x.shape
    return jnp.concatenate([x, jnp.zeros((128 - r, w), F32)], axis=0).T


def _nsa_dec_body(layer, n_phys, n_pages, nb, t, pt_ref, cache_ref, q_ref, dtg_ref, kvn_ref, kwn_ref, win_ref,
                  w1_ref, pe_ref, w2_ref, exp_ref, y_ref, xt_ref, xc_ref, pq_ref, wk_ref, sem):
    b = pl.program_id(0)
    past = n_pages * PAGE_SIZE
    lk = past + t
    n_ch = -(-lk // CMP_STRIDE)
    n_cmp = n_ch - 1
    xrows = xt_ref.shape[2]
    nch_pad = xrows // CMP_STRIDE
    n_slc = -(-lk // SLC_BLOCK)
    nblk = exp_ref.shape[0]
    n_sel = min(N_SELECT, n_slc)
    slot = b % 2

    def page_copy(bb, sl, p):
        phys = pt_ref[bb, p] + layer * n_phys
        dst = pl.ds(pl.multiple_of(p * PAGE_SIZE, PAGE_SIZE), PAGE_SIZE)
        return pltpu.make_async_copy(cache_ref.at[phys], xt_ref.at[sl, :, dst], sem.at[sl])

    def start_all(bb, sl):
        def body(p, carry):
            page_copy(bb, sl, p).start()
            return carry
        lax.fori_loop(0, n_pages, body, 0)

    @pl.when(b == 0)
    def _():
        start_all(0, 0)

    @pl.when(b + 1 < nb)
    def _():
        start_all(b + 1, 1 - slot)

    def wait_body(p, carry):
        page_copy(b, slot, p).wait()
        return carry
    lax.fori_loop(0, n_pages, wait_body, 0)

    xt_ref[slot, :, past:xrows] = _rows_to_cols(kvn_ref[...])

    eye = (lax.broadcasted_iota(jnp.int32, (128, 128), 0)
           == lax.broadcasted_iota(jnp.int32, (128, 128), 1)).astype(BF16)
    for p in range(xrows // 128):
        tile = xt_ref[slot, 0:128, p * 128:(p + 1) * 128].astype(BF16)
        xc_ref[p * 128:(p + 1) * 128, :] = _dot_nt(eye, tile)

    rows = NSA_H * t
    qs = _stack_heads(q_ref[...] * (NSA_HD ** -0.5)).astype(BF16)
    ti = lax.broadcasted_iota(jnp.int32, (rows, 1), 0) % t
    tpos = past + ti

    ckv = _compress(lambda l: xc_ref[pl.ds(l, nch_pad, stride=CMP_STRIDE), :], nch_pad, w1_ref, pe_ref,
                    w2_ref, pq_ref)
    nc_use = (n_cmp + 127) // 128 * 128
    ck = ckv[0:nc_use, 0:NSA_HD].astype(BF16)
    cv = ckv[0:nc_use, NSA_HD:2 * NSA_HD].astype(BF16)
    sc = _dot_nt(qs, ck)
    ncol = lax.broadcasted_iota(jnp.int32, (rows, nc_use), 1)
    cmask = (ncol * CMP_STRIDE + (CMP_LEN - 1) <= tpos) & (ncol < n_cmp)
    mx = jnp.max(jnp.where(cmask, sc, NEG), axis=1, keepdims=True)
    e = jnp.where(cmask, jnp.exp(sc - mx), 0.0)
    pc = e / jnp.maximum(jnp.sum(e, axis=1, keepdims=True), 1e-30)
    o_cmp = _dot(pc.astype(BF16), cv)
    pcsum = pc[0:t] + pc[t:2 * t] + pc[2 * t:3 * t] + pc[3 * t:4 * t]

    nrow = lax.broadcasted_iota(jnp.int32, (nc_use, nblk), 0)
    jcol = lax.broadcasted_iota(jnp.int32, (nc_use, nblk), 1)
    ov = ((nrow * CMP_STRIDE < (jcol + 1) * SLC_BLOCK) & (nrow * CMP_STRIDE + CMP_LEN > jcol * SLC_BLOCK)
          & (nrow < n_cmp)).astype(F32)
    imp = jnp.dot(pcsum, ov, precision=HIGHEST, preferred_element_type=F32)
    jl = lax.broadcasted_iota(jnp.int32, (t, nblk), 1)
    tq = past + lax.broadcasted_iota(jnp.int32, (t, nblk), 0)
    cur = tq // SLC_BLOCK
    forced = (jl == 0) | (jl == cur) | (jl == cur - 1)
    imp = jnp.where(forced, jnp.inf, imp)
    imp = jnp.where((jl * SLC_BLOCK <= tq) & (jl < n_slc), imp, -jnp.inf)
    imp_t = jnp.concatenate([imp, jnp.full((128 - t, nblk), -jnp.inf, F32)], axis=0).T
    nb_rows = -(-n_slc // 8) * 8
    irow = lax.broadcasted_iota(jnp.int32, (nb_rows, nblk), 0)
    jlane = lax.broadcasted_iota(jnp.int32, (nb_rows, nblk), 1)
    ranks = []
    for qi in range(t):
        colv = imp_t[0:nb_rows, qi:qi + 1]
        rowv = imp[qi:qi + 1, :]
        before = (colv > rowv) | ((colv == rowv) & (irow < jlane))
        ranks.append(jnp.sum(before.astype(F32), axis=0, keepdims=True))
    rank = jnp.concatenate(ranks, axis=0)
    sel = ((rank < n_sel) & (imp > -jnp.inf)).astype(BF16)
    selexp = _dot(sel, exp_ref[...])
    sel4 = jnp.concatenate([selexp] * NSA_H, axis=0)

    kt = xt_ref[slot, 2 * NSA_HD:3 * NSA_HD, :].astype(BF16)
    vt = xt_ref[slot, 3 * NSA_HD:4 * NSA_HD, :].astype(BF16)
    kpos = lax.broadcasted_iota(jnp.int32, (rows, xrows), 1)
    o_slc = _softmax_pv_nt(_dot(qs, kt), (sel4 > 0.5) & (kpos <= tpos), vt)

    wrows = wk_ref.shape[1]
    wk_ref[:, 0:WINDOW] = win_ref[0]
    wk_ref[:, WINDOW:wrows] = _rows_to_cols(kwn_ref[...])
    kwt = wk_ref[0:NSA_HD, :].astype(BF16)
    vwt = wk_ref[NSA_HD:2 * NSA_HD, :].astype(BF16)
    wpos = past - WINDOW + lax.broadcasted_iota(jnp.int32, (rows, wrows), 1)
    wmask = (wpos >= 0) & (wpos <= tpos) & (wpos > tpos - WINDOW)
    o_win = _softmax_pv_nt(_dot(qs, kwt), wmask, vwt)

    y_ref[...] = _gated_mix(dtg_ref[...], o_cmp, o_slc, o_win, t)


def _nsa_dec(page_table, cache_t, q, dtg, kvc, kvw, win_t, w1, pe, w2, expand, layer, n_phys, nb, t):
    n_pages = page_table.shape[1]
    xrows = expand.shape[1]
    assert xrows == n_pages * PAGE_SIZE + 128 and t <= 128
    grid_spec = pltpu.PrefetchScalarGridSpec(
        num_scalar_prefetch=1,
        grid=(nb,),
        in_specs=[
            pl.BlockSpec(memory_space=pl.ANY),
            pl.BlockSpec((t, NSA_W), lambda b, pt: (b, 0)),
            pl.BlockSpec((t, 128), lambda b, pt: (b, 0)),
            pl.BlockSpec((t, 4 * NSA_HD), lambda b, pt: (b, 0)),
            pl.BlockSpec((t, 2 * NSA_HD), lambda b, pt: (b, 0)),
            pl.BlockSpec((1, 2 * NSA_HD, WINDOW), lambda b, pt: (layer * nb + b, 0, 0)),
            pl.BlockSpec(w1.shape, lambda b, pt: (0, 0, 0)),
            pl.BlockSpec(pe.shape, lambda b, pt: (0, 0, 0)),
            pl.BlockSpec(w2.shape, lambda b, pt: (0, 0)),
            pl.BlockSpec(expand.shape, lambda b, pt: (0, 0)),
        ],
        out_specs=pl.BlockSpec((t, NSA_W), lambda b, pt: (b, 0)),
        scratch_shapes=[
            pltpu.VMEM((2, 4 * NSA_HD, xrows), F32),
            pltpu.VMEM((xrows, 128), F32),
            pltpu.VMEM((xrows // CMP_STRIDE + 8, 4 * CMP_HID), F32),
            pltpu.VMEM((2 * NSA_HD, WINDOW + 128), F32),
            pltpu.SemaphoreType.DMA((2,)),
        ],
    )
    return pl.pallas_call(
        functools.partial(_nsa_dec_body, layer, n_phys, n_pages, nb, t),
        grid_spec=grid_spec,
        out_shape=jax.ShapeDtypeStruct((nb * t, NSA_W), F32),
        compiler_params=_cparams(("arbitrary",)),
        name="nsa_dec",
    )(page_table, cache_t, q, dtg, kvc, kvw, win_t, w1, pe, w2, expand)


def _out_proj_body(ys_ref, yp_ref, yn_ref, x_ref, w_ref, npost_ref, nffn_ref, x1_ref, h2_ref):
    mix = (_dot(ys_ref[...].astype(BF16), w_ref[0, 0:SSD_W, :])
           + _dot(yp_ref[...].astype(BF16), w_ref[0, SSD_W:SSD_W + POOL_W, :])
           + _dot(yn_ref[...].astype(BF16), w_ref[0, SSD_W + POOL_W:D_MODEL, :]))
    x1 = x_ref[...] + _rmsnorm(mix, npost_ref[...])
    x1_ref[...] = x1
    h2_ref[...] = _rmsnorm(x1, nffn_ref[...]).astype(BF16)


def _out_proj(ys, yp, yn, x2d, w_out, layer, npost, nffn, tm):
    m = x2d.shape[0]

    def tile(w):
        return pl.BlockSpec((tm, w), lambda i: (i, 0))

    vec = pl.BlockSpec((1, D_MODEL), lambda i: (0, 0))
    return pl.pallas_call(
        _out_proj_body,
        grid=(m // tm,),
        in_specs=[tile(SSD_W), tile(POOL_W), tile(NSA_W), tile(D_MODEL),
                  pl.BlockSpec((1, D_MODEL, D_MODEL), lambda i: (layer, 0, 0)), vec, vec],
        out_specs=[tile(D_MODEL), tile(D_MODEL)],
        out_shape=[jax.ShapeDtypeStruct((m, D_MODEL), F32), jax.ShapeDtypeStruct((m, D_MODEL), BF16)],
        compiler_params=_cparams(("arbitrary",)),
        name="out_proj",
    )(ys, yp, yn, x2d, w_out, npost, nffn)


def _ffn_body(carry_mode, tiles_per_seq, t, tf, *refs):
    if carry_mode:
        (h2_ref, x1_ref, wg_ref, wv_ref, wd_ref, cw_ref, cb_ref, npost_ref,
         x2_ref, gsave_ref, act_ref, gext_ref, carry_ref) = refs
    else:
        (h2_ref, x1_ref, wg_ref, wv_ref, wd_ref, cw_ref, cb_ref, npost_ref, prev_ref,
         x2_ref, gsave_ref, act_ref) = refs
    tm = h2_ref.shape[0]
    h2 = h2_ref[...]
    if carry_mode:
        @pl.when(pl.program_id(0) % tiles_per_seq == 0)
        def _():
            carry_ref[...] = jnp.zeros(carry_ref.shape, F32)
    else:
        r = lax.broadcasted_iota(jnp.int32, (tm, tf), 0) % t
    for f in range(D_FF // tf):
        fs = slice(f * tf, (f + 1) * tf)
        g = _dot(h2, wg_ref[0, :, fs])
        v = _dot(h2, wv_ref[0, :, fs])
        if carry_mode:
            buf = f % 2
            gext_ref[buf, 0:8, :] = carry_ref[:, fs]
            gext_ref[buf, 8:8 + tm, :] = g
            carry_ref[:, fs] = g[tm - 8:tm]
            gsave_ref[0, :, fs] = g[tm - 8:tm]
            g1 = gext_ref[buf, 7:7 + tm, :]
            g2 = gext_ref[buf, 6:6 + tm, :]
        else:
            prev = prev_ref[:, fs]
            g1 = jnp.where(r >= 1, pltpu.roll(g, 1, 0), pltpu.roll(prev, tm - 1, 0))
            g2 = jnp.where(r >= 2, pltpu.roll(g, 2, 0), prev)
            gsave_ref[:, fs] = g
        cw = cw_ref[:, fs]
        gc = g * cw[2:3, :] + g1 * cw[1:2, :] + g2 * cw[0:1, :] + cb_ref[:, fs]
        act_ref[:, fs] = (_gelu_tanh(gc) * v).astype(BF16)
    out = _dot(act_ref[...], wd_ref[0])
    x2_ref[...] = x1_ref[...] + _rmsnorm(out, npost_ref[...])


def _ffn(h2, x1, wg, wv, wd, layer, cw, cb, npost, tm, tf, carry_mode, tiles_per_seq, t, prev_rows=None):
    m = h2.shape[0]
    nm = m // tm
    resident = pl.Buffered(1)
    in_specs = [
        pl.BlockSpec((tm, D_MODEL), lambda i: (i, 0)),
        pl.BlockSpec((tm, D_MODEL), lambda i: (i, 0)),
        pl.BlockSpec((1, D_MODEL, D_FF), lambda i: (layer, 0, 0), pipeline_mode=resident),
        pl.BlockSpec((1, D_MODEL, D_FF), lambda i: (layer, 0, 0), pipeline_mode=resident),
        pl.BlockSpec((1, D_FF, D_MODEL), lambda i: (layer, 0, 0), pipeline_mode=resident),
        pl.BlockSpec((FFN_CONV, D_FF), lambda i: (0, 0)),
        pl.BlockSpec((1, D_FF), lambda i: (0, 0)),
        pl.BlockSpec((1, D_MODEL), lambda i: (0, 0)),
    ]
    args = [h2, x1, wg, wv, wd, cw, cb, npost]
    scratch = [pltpu.VMEM((tm, D_FF), BF16)]
    if carry_mode:
        gsave_shape = jax.ShapeDtypeStruct((nm, 8, D_FF), F32)
        gsave_spec = pl.BlockSpec((1, 8, D_FF), lambda i: (i, 0, 0))
        scratch += [pltpu.VMEM((2, tm + 8, tf), F32), pltpu.VMEM((8, D_FF), F32)]
    else:
        gsave_shape = jax.ShapeDtypeStruct((m, D_FF), F32)
        gsave_spec = pl.BlockSpec((tm, D_FF), lambda i: (i, 0))
        in_specs.append(pl.BlockSpec((tm, D_FF), lambda i: (i, 0)))
        args.append(prev_rows)
    return pl.pallas_call(
        functools.partial(_ffn_body, carry_mode, tiles_per_seq, t, tf),
        grid=(nm,),
        in_specs=in_specs,
        out_specs=[pl.BlockSpec((tm, D_MODEL), lambda i: (i, 0)), gsave_spec],
        out_shape=[jax.ShapeDtypeStruct((m, D_MODEL), F32), gsave_shape],
        scratch_shapes=scratch,
        compiler_params=_cparams(("arbitrary",)),
        name="ffn",
    )(*args)


def _prep_params(w_in, ssd_conv_w, ssd_conv_b, ssd_dt_bias, ssd_a_log, ssd_d, ssd_norm, pool_w, pool_scale,
                 nsa_pe_k, nsa_pe_v, nsa_w1_k, nsa_w1_v, nsa_w2_k, nsa_w2_v, w_out, ffn_w_gate, ffn_w_val,
                 ffn_w_down):
    depth = w_in.shape[0]
    w_in_p = jnp.concatenate(
        [w_in[..., 0:OFF_XBC], w_in[..., OFF_DT:OFF_KV], w_in[..., OFF_XBC:OFF_DT], w_in[..., OFF_KV:IN_W],
         jnp.zeros((depth, D_MODEL, IN_W_PAD - IN_W), F32)], axis=-1).astype(BF16)
    pad_h = ((0, 0), (0, 128 - SSD_H))
    dtb = jnp.pad(ssd_dt_bias, pad_h)[:, None, :]
    alog = jnp.pad(ssd_a_log, pad_h)[:, None, :]
    dsk = jnp.repeat(ssd_d, SSD_HD, axis=1)[:, None, :]
    n_g = len(POOL_WINDOWS)
    eye = jnp.eye(n_g, dtype=F32)
    pw_bd = (pool_w[:, :, :, None, :] * eye[None, :, None, :, None]).reshape(depth, POOL_W, POOL_W).astype(BF16)
    zk = jnp.zeros((depth, CMP_STRIDE, NSA_HD, CMP_HID), F32)
    top = jnp.concatenate([nsa_w1_k[:, :CMP_STRIDE], zk, nsa_w1_k[:, CMP_STRIDE:], zk], axis=-1)
    bot = jnp.concatenate([zk, nsa_w1_v[:, :CMP_STRIDE], zk, nsa_w1_v[:, CMP_STRIDE:]], axis=-1)
    w1 = jnp.concatenate([top, bot], axis=2).astype(BF16)
    pe_a = jnp.concatenate([nsa_pe_k[:, :CMP_STRIDE], nsa_pe_v[:, :CMP_STRIDE]], axis=-1)
    pe_b = jnp.concatenate([nsa_pe_k[:, CMP_STRIDE:], nsa_pe_v[:, CMP_STRIDE:]], axis=-1)
    pe = jnp.concatenate([pe_a[:, :, None], pe_b[:, :, None], jnp.zeros((depth, CMP_STRIDE, 6, 128), F32)], axis=2)
    zw = jnp.zeros((depth, CMP_HID, NSA_HD), F32)
    w2 = jnp.concatenate([jnp.concatenate([nsa_w2_k, zw], axis=-1), jnp.concatenate([zw, nsa_w2_v], axis=-1)],
                         axis=1).astype(BF16)
    return dict(w_in=w_in_p, cw=ssd_conv_w, cb=ssd_conv_b[:, None, :], dtb=dtb, alog=alog, dsk=dsk,
                nrm=ssd_norm[:, None, :], pw=pw_bd, ps=pool_scale[:, None, :], w1=w1, pe=pe, w2=w2,
                w_out=w_out.astype(BF16), wg=ffn_w_gate.astype(BF16), wv=ffn_w_val.astype(BF16),
                wd=ffn_w_down.astype(BF16))


def _mixer_weights(p, l):
    return (p['cw'][l], p['cb'][l], p['dtb'][l], p['alog'][l], p['dsk'][l], p['nrm'][l], p['pw'][l], p['ps'][l])


def kernel(x_prompt, x_sample, cache_nsa_kv, page_table, state_nsa_win, state_ssd_conv, state_ssm, state_pool, state_ffn_conv, norm_mix_pre, w_in, ssd_conv_w, ssd_conv_b, ssd_dt_bias, ssd_a_log, ssd_d, ssd_norm, pool_w, pool_scale, nsa_pe_k, nsa_pe_v, nsa_w1_k, nsa_w1_v, nsa_w2_k, nsa_w2_v, w_out, norm_mix_post, norm_ffn_pre, ffn_w_gate, ffn_w_val, ffn_conv_w, ffn_conv_b, ffn_w_down, norm_ffn_post):
    depth = w_in.shape[0]
    bp, tp, _ = x_prompt.shape
    bs, ts, _ = x_sample.shape
    n_phys = cache_nsa_kv.shape[1]
    n_pages = page_table.shape[1]
    past = n_pages * PAGE_SIZE
    assert tp % 512 == 0 and tp >= WINDOW and ts == 8 and state_nsa_win.shape[2] == WINDOW
    p = _prep_params(w_in, ssd_conv_w, ssd_conv_b, ssd_dt_bias, ssd_a_log, ssd_d, ssd_norm, pool_w, pool_scale,
                     nsa_pe_k, nsa_pe_v, nsa_w1_k, nsa_w1_v, nsa_w2_k, nsa_w2_v, w_out, ffn_w_gate, ffn_w_val,
                     ffn_w_down)
    ms = bs * ts
    tab_p = _rope_table(jnp.arange(tp, dtype=jnp.int32))
    tab_s = jnp.tile(_rope_table(past + jnp.arange(ts, dtype=jnp.int32)), (bs, 1))
    cache = jnp.transpose(cache_nsa_kv, (0, 1, 3, 4, 2)).reshape(depth * n_phys, 4 * NSA_HD, PAGE_SIZE)
    win_state = jnp.transpose(state_nsa_win, (0, 1, 3, 4, 2)).reshape(depth * bs, 2 * NSA_HD, WINDOW)
    lk = past + ts
    xrows = -(-(-(-lk // CMP_STRIDE)) // 8) * 8 * CMP_STRIDE
    xrows = -(-xrows // 128) * 128
    nblk = -(-(xrows // SLC_BLOCK) // 128) * 128
    expand = (jnp.arange(nblk, dtype=jnp.int32)[:, None]
              == jnp.arange(xrows, dtype=jnp.int32)[None, :] // SLC_BLOCK).astype(BF16)

    tm_p = 512
    tm_f = 512
    tf = 512
    nc_p = tp // CHUNK
    zeros_c = jnp.zeros((bp, 8, SSD_CONV_DIM), F32)
    zeros_s = jnp.zeros((bp, SSD_H, SSD_HD, SSD_N), F32)
    zeros_p = jnp.zeros((bp, 16, POOL_W), F32)

    xp = x_prompt.reshape(bp * tp, D_MODEL)
    xs = x_sample.reshape(ms, D_MODEL)
    outs_p, outs_s = [], []
    for l in range(depth):
        nw = norm_mix_pre[l][None]
        npost = norm_mix_post[l][None]
        nffn = norm_ffn_pre[l][None]
        nfpost = norm_ffn_post[l][None]
        mw = _mixer_weights(p, l)
        fcw, fcb = ffn_conv_w[l], ffn_conv_b[l][None]

        z, xbc, u, q, kvc, kvw, dtg = _in_proj(xp, nw, p['w_in'], l, tab_p, tm_p)
        y_ssd, y_pool, ssm_new = _ssd_pool(xbc, z, dtg, u, zeros_c, zeros_s, zeros_p, mw, bp, nc_p, CHUNK, 0)
        ckv = _nsa_cmp(kvc, p['w1'][l], p['pe'][l], p['w2'][l], bp, tp)
        y_nsa = _nsa_attn(q, dtg, ckv, kvc, kvw, bp, tp)
        x1, h2 = _out_proj(y_ssd, y_pool, y_nsa, xp, p['w_out'], l, npost, nffn, tm_p)
        xp, gsave = _ffn(h2, x1, p['wg'], p['wv'], p['wd'], l, fcw, fcb, nfpost, tm_f, tf, True, tp // tm_f, tp)
        outs_p.append((
            kvc.reshape(bp, tp, 4, NSA_HD),
            kvw.reshape(bp, tp, 2, NSA_HD)[:, tp - WINDOW:],
            xbc.reshape(bp, tp, SSD_CONV_DIM)[:, tp - (SSD_CONV - 1):],
            ssm_new,
            u.reshape(bp, tp, POOL_W)[:, tp - POOL_KEEP:],
            gsave.reshape(bp, tp // tm_f, 8, D_FF)[:, -1, 8 - (FFN_CONV - 1):],
        ))

        z, xbc, u, q, kvc, kvw, dtg = _in_proj(xs, nw, p['w_in'], l, tab_s, ms)
        cprev = jnp.pad(state_ssd_conv[l], ((0, 0), (8 - (SSD_CONV - 1), 0), (0, 0)))
        pprev = jnp.pad(state_pool[l], ((0, 0), (16 - POOL_KEEP, 0), (0, 0)))
        y_ssd, y_pool, ssm_new = _ssd_pool(xbc, z, dtg, u, cprev, state_ssm[l], pprev, mw, bs, 1, ts, past)
        y_nsa = _nsa_dec(page_table, cache, q, dtg, kvc, kvw, win_state, p['w1'][l], p['pe'][l], p['w2'][l],
                         expand, l, n_phys, bs, ts)
        x1, h2 = _out_proj(y_ssd, y_pool, y_nsa, xs, p['w_out'], l, npost, nffn, ms)
        fprev = state_ffn_conv[l]
        fprev_rows = jnp.pad(fprev, ((0, 0), (0, ts - (FFN_CONV - 1)), (0, 0))).reshape(ms, D_FF)
        xs, gsave = _ffn(h2, x1, p['wg'], p['wv'], p['wd'], l, fcw, fcb, nfpost, ms, tf, False, 1, ts, fprev_rows)
        kvw3 = kvw.reshape(bs, ts, 2, NSA_HD)
        xbc3 = xbc.reshape(bs, ts, SSD_CONV_DIM)
        u3 = u.reshape(bs, ts, POOL_W)
        g3 = gsave.reshape(bs, ts, D_FF)
        outs_s.append((
            kvc.reshape(bs, ts, 4, NSA_HD),
            jnp.concatenate([state_nsa_win[l], kvw3], axis=1)[:, -WINDOW:],
            jnp.concatenate([state_ssd_conv[l], xbc3], axis=1)[:, -(SSD_CONV - 1):],
            ssm_new,
            jnp.concatenate([state_pool[l], u3], axis=1)[:, -POOL_KEEP:],
            jnp.concatenate([state_ffn_conv[l], g3], axis=1)[:, -(FFN_CONV - 1):],
        ))

    def stk(outs, i):
        return jnp.stack([o[i] for o in outs])

    return (xp.reshape(bp, tp, D_MODEL), xs.reshape(bs, ts, D_MODEL), stk(outs_p, 0), stk(outs_s, 0),
            stk(outs_p, 1), stk(outs_s, 1), stk(outs_p, 2), stk(outs_s, 2), stk(outs_p, 3), stk(outs_s, 3),
            stk(outs_p, 4), stk(outs_s, 4), stk(outs_p, 5), stk(outs_s, 5))
```

```python
import functools
import math

import jax
import jax.numpy as jnp
from jax import lax
from jax.experimental import pallas as pl
from jax.experimental.pallas import tpu as pltpu

F32 = jnp.float32
BF16 = jnp.bfloat16
HIGHEST = lax.Precision.HIGHEST

D_MODEL = 1024
PAGE_SIZE = 128
SSD_W = 512
SSD_HD = 64
SSD_H = 8
SSD_N = 128
SSD_G = 2
SSD_CONV = 4
SSD_CONV_DIM = 1024
POOL_W = 256
POOL_WINDOWS = (2, 4, 8, 16)
POOL_GC = 64
POOL_KEEP = 15
NSA_W = 256
NSA_HD = 64
NSA_H = 4
CMP_STRIDE = 16
CMP_LEN = 32
CMP_HID = 128
SLC_BLOCK = 64
N_SELECT = 16
WINDOW = 512
ROPE_DIM = 16
ROPE_THETA = 500000.0
D_FF = 4096
FFN_CONV = 3
RMS_EPS = 1e-6
OFF_Z = 512
OFF_XBC = 1536
OFF_DT = 1544
OFF_POOL = 1800
OFF_Q = 2056
OFF_KV = 2440
IN_W = 2452
IN_W_PAD = 2560
CHUNK = 128
NEG = -1e30
VMEM_LIMIT = 56 * 1024 * 1024


def _cparams(sem):
    return pltpu.CompilerParams(dimension_semantics=sem, vmem_limit_bytes=VMEM_LIMIT)


def _rmsnorm(x, w):
    return x * lax.rsqrt(jnp.mean(x * x, axis=-1, keepdims=True) + RMS_EPS) * w


def _silu(x):
    return x / (1.0 + jnp.exp(-x))


def _gelu_tanh(x):
    k = 0.7978845608028654
    hx = 0.5 * x
    return hx + hx * jnp.tanh(x * (k + (k * 0.044715) * (x * x)))


def _dot(a, b):
    return jnp.dot(a, b, preferred_element_type=F32)


def _dot_nt(a, b):
    return lax.dot_general(a, b, (((1,), (1,)), ((), ())), preferred_element_type=F32)


def _dot_tn(a, b):
    return lax.dot_general(a, b, (((0,), (0,)), ((), ())), preferred_element_type=F32)


def _in_proj_body(x_ref, nw_ref, w_ref, tab_ref, z_ref, xbc_ref, pool_ref, q_ref, kvc_ref, kvw_ref, dtg_ref):
    tm = x_ref.shape[0]
    h = _rmsnorm(x_ref[...], nw_ref[...]).astype(BF16)

    def proj(a, b):
        return _dot(h, w_ref[0, :, a:b])

    z_ref[...] = proj(0, 512)
    xbc_ref[...] = proj(512, 1536)
    pool_ref[...] = proj(1536, 1792)
    dtg_ref[...] = proj(2432, 2560)
    lane = lax.broadcasted_iota(jnp.int32, (tm, 128), 1)
    first = (lane % NSA_HD) < (ROPE_DIM // 2)

    def rope(x, c, s):
        outs = []
        for j in range(x.shape[1] // 128):
            xj = x[:, j * 128:(j + 1) * 128]
            rot = jnp.where(first, pltpu.roll(xj, 128 - ROPE_DIM // 2, 1), pltpu.roll(xj, ROPE_DIM // 2, 1))
            outs.append(xj * c + rot * s)
        return outs

    q = rope(proj(1792, 2048), tab_ref[:, 0:128], tab_ref[:, 128:256])
    q_ref[:, 0:128] = q[0]
    q_ref[:, 128:256] = q[1]
    kv = rope(proj(2048, 2432), tab_ref[:, 256:384], tab_ref[:, 384:512])
    kvc_ref[:, 0:128] = kv[0]
    kvc_ref[:, 128:256] = kv[1]
    kvw_ref[...] = kv[2]


def _in_proj(x2d, nw, w_in_p, layer, tab, tm):
    m = x2d.shape[0]
    tab_blocks = tab.shape[0] // tm
    widths = (512, 1024, 256, 256, 256, 128, 128)
    return pl.pallas_call(
        _in_proj_body,
        grid=(m // tm,),
        in_specs=[
            pl.BlockSpec((tm, D_MODEL), lambda i: (i, 0)),
            pl.BlockSpec((1, D_MODEL), lambda i: (0, 0)),
            pl.BlockSpec((1, D_MODEL, IN_W_PAD), lambda i: (layer, 0, 0)),
            pl.BlockSpec((tm, 512), lambda i: (i % tab_blocks, 0)),
        ],
        out_specs=[pl.BlockSpec((tm, w), lambda i: (i, 0)) for w in widths],
        out_shape=[jax.ShapeDtypeStruct((m, w), F32) for w in widths],
        compiler_params=_cparams(("arbitrary",)),
        name="in_proj",
    )(x2d, nw, w_in_p, tab)


def _rope_table(pos):
    half = ROPE_DIM // 2
    inv = 1.0 / (ROPE_THETA ** (jnp.arange(half, dtype=F32) / half))
    ang = pos.astype(F32)[:, None] * inv
    cos, sin = jnp.cos(ang), jnp.sin(ang)
    t = pos.shape[0]
    c_rot = jnp.concatenate([cos, cos, jnp.ones((t, NSA_HD - ROPE_DIM), F32)], axis=1)
    s_rot = jnp.concatenate([-sin, sin, jnp.zeros((t, NSA_HD - ROPE_DIM), F32)], axis=1)
    one, zero = jnp.ones((t, NSA_HD), F32), jnp.zeros((t, NSA_HD), F32)
    return jnp.concatenate([c_rot, c_rot, s_rot, s_rot, c_rot, one, s_rot, zero], axis=1)


def _ssd_pool_body(lr, nc, pos0, nsub, xbc_ref, z_ref, dtg_ref, u_ref, cprev_ref, sprev_ref, pprev_ref,
                   cw_ref, cb_ref, dtb_ref, alog_ref, dsk_ref, nrm_ref, pw_ref, ps_ref,
                   yssd_ref, ypool_ref, snew_ref, ext_ref, state_ref, pext_ref):
    c = pl.program_id(1)
    L = CHUNK

    @pl.when(c == 0)
    def _():
        ext_ref[:, 0:8, :] = cprev_ref[...]
        pext_ref[:, 0:16, :] = pprev_ref[...]
        state_ref[...] = sprev_ref[...]

    if nc > 1:
        @pl.when(c > 0)
        def _():
            ext_ref[:, 0:8, :] = ext_ref[:, L:L + 8, :]
            pext_ref[:, 0:16, :] = pext_ref[:, L:L + 16, :]

    ext_ref[:, 8:8 + lr, :] = xbc_ref[...]
    pext_ref[:, 16:16 + lr, :] = u_ref[...]
    if lr < L:
        ext_ref[:, 8 + lr:8 + L, :] = jnp.zeros((nsub, L - lr, SSD_CONV_DIM), F32)
        pext_ref[:, 16 + lr:16 + L, :] = jnp.zeros((nsub, L - lr, POOL_W), F32)

    cw = cw_ref[...]
    row = lax.broadcasted_iota(jnp.int32, (L, L), 0)
    col = lax.broadcasted_iota(jnp.int32, (L, L), 1)
    tri = row >= col
    tri_f = tri.astype(F32)
    eye_f = (row == col).astype(F32)
    eye_b = eye_f.astype(BF16)
    gw = (SSD_H // SSD_G) * SSD_HD
    expand = (lax.broadcasted_iota(jnp.int32, (128, SSD_W), 0)
              == lax.broadcasted_iota(jnp.int32, (128, SSD_W), 1) // SSD_HD).astype(F32)
    neg_a = -jnp.exp(alog_ref[...])
    dsk = dsk_ref[...]
    prow = lax.broadcasted_iota(jnp.int32, (L, POOL_W), 0)
    plane = lax.broadcasted_iota(jnp.int32, (L, POOL_W), 1)
    pos = pos0 + c * L + prow + 1

    for sub in range(nsub):
        conv = (ext_ref[sub, 8:8 + L, :] * cw[3:4, :] + ext_ref[sub, 7:7 + L, :] * cw[2:3, :]
                + ext_ref[sub, 6:6 + L, :] * cw[1:2, :] + ext_ref[sub, 5:5 + L, :] * cw[0:1, :] + cb_ref[...])
        xbc = _silu(conv)
        xs = xbc[:, 0:SSD_W]
        bmat = [xbc[:, SSD_W + g * SSD_N:SSD_W + (g + 1) * SSD_N].astype(BF16) for g in range(SSD_G)]
        cmat = [xbc[:, SSD_W + (SSD_G + g) * SSD_N:SSD_W + (SSD_G + g + 1) * SSD_N].astype(BF16)
                for g in range(SSD_G)]

        if lr == L:
            dtraw = dtg_ref[sub]
        else:
            dtraw = jnp.concatenate([dtg_ref[sub], jnp.zeros((L - lr, 128), F32)], axis=0)
        xdt = dtraw + dtb_ref[...]
        dt = jnp.maximum(xdt, 0.0) + jnp.log1p(jnp.exp(-jnp.abs(xdt)))
        dt = jnp.where((col < SSD_H) & (row < lr), dt, 0.0)
        acum = jnp.dot(tri_f, dt * neg_a, precision=HIGHEST, preferred_element_type=F32)
        arow = lax.dot_general(eye_f, acum, (((1,), (1,)), ((), ())), precision=HIGHEST,
                               preferred_element_type=F32)
        dt_x = jnp.dot(dt, expand, precision=HIGHEST, preferred_element_type=F32)
        ac_x = jnp.dot(acum, expand, precision=HIGHEST, preferred_element_type=F32)
        alast_x = ac_x[L - 1:L, :]
        xdt = xs * dt_x
        xdt_b = xdt.astype(BF16)
        xw = (xdt * jnp.exp(alast_x - ac_x)).astype(BF16)
        cb_scores = [_dot_nt(cmat[g], bmat[g]) for g in range(SSD_G)]
        y_off = []
        for g in range(SSD_G):
            gs = slice(g * gw, (g + 1) * gw)
            st = state_ref[sub, g]
            y_off.append(_dot(cmat[g], st.astype(BF16)))
            b_t = _dot_nt(eye_b, bmat[g]).astype(BF16)
            state_ref[sub, g] = jnp.exp(alast_x[:, gs]) * st + _dot(b_t, xw[:, gs])
        y_diag = []
        for hp in range(SSD_H // 2):
            tile_b = xdt_b[:, hp * 128:(hp + 1) * 128]
            acc = None
            for h in (2 * hp, 2 * hp + 1):
                g = h // (SSD_H // SSD_G)
                decay = jnp.exp(jnp.where(tri, acum[:, h:h + 1] - arow[h:h + 1, :], NEG))
                own = (col < SSD_HD) if h % 2 == 0 else (col >= SSD_HD)
                part = _dot((cb_scores[g] * decay).astype(BF16), jnp.where(own, tile_b, jnp.zeros_like(tile_b)))
                acc = part if acc is None else acc + part
            y_diag.append(acc)
        y = jnp.concatenate(y_diag, axis=1) + jnp.concatenate(y_off, axis=1) * jnp.exp(ac_x) + dsk * xs
        if lr < L:
            y = y[0:lr]
        y = y * _silu(z_ref[sub])
        yssd_ref[sub] = _rmsnorm(y, nrm_ref[...])

        e1 = pext_ref[sub]
        s2 = e1 + pltpu.roll(e1, 1, 0)
        s4 = s2 + pltpu.roll(s2, 2, 0)
        s8 = s4 + pltpu.roll(s4, 4, 0)
        s16 = s8 + pltpu.roll(s8, 8, 0)
        pooled = jnp.zeros((L, POOL_W), F32)
        for gi, (w, sw) in enumerate(zip(POOL_WINDOWS, (s2, s4, s8, s16))):
            cnt = jnp.minimum(pos, w).astype(F32)
            grp = (plane >= gi * POOL_GC) & (plane < (gi + 1) * POOL_GC)
            pooled = jnp.where(grp, sw[16:16 + L] / cnt, pooled)
        pooled = pooled - e1[16:16 + L]
        yp = _dot(pooled.astype(BF16), pw_ref[...]) * ps_ref[...]
        ypool_ref[sub] = yp[0:lr] if lr < L else yp

    @pl.when(c == nc - 1)
    def _():
        snew_ref[...] = state_ref[...]


def _ssd_pool(xbc, z, dtg, u, cprev, sprev, pprev, wts, nb, nc, lr, pos0, nsub):
    cw, cb, dtb, alog, dsk, nrm, pw, ps = wts
    t = nc * lr
    assert nb % nsub == 0

    def tile(w):
        return pl.BlockSpec((nsub, lr, w), lambda b, c: (b, c, 0))

    def full2(a):
        return pl.BlockSpec(a.shape, lambda b, c: (0, 0))

    hg = SSD_H // SSD_G
    st_shape = (SSD_G, SSD_N, hg * SSD_HD)
    sprev_t = sprev.reshape(nb, SSD_G, hg, SSD_HD, SSD_N).transpose(0, 1, 4, 2, 3).reshape((nb,) + st_shape)
    state_spec = pl.BlockSpec((nsub,) + st_shape, lambda b, c: (b, 0, 0, 0))
    yssd, ypool, snew_t = pl.pallas_call(
        functools.partial(_ssd_pool_body, lr, nc, pos0, nsub),
        grid=(nb // nsub, nc),
        in_specs=[
            tile(SSD_CONV_DIM), tile(SSD_W), tile(128), tile(POOL_W),
            pl.BlockSpec((nsub, 8, SSD_CONV_DIM), lambda b, c: (b, 0, 0)),
            state_spec,
            pl.BlockSpec((nsub, 16, POOL_W), lambda b, c: (b, 0, 0)),
            full2(cw), full2(cb), full2(dtb), full2(alog), full2(dsk), full2(nrm), full2(pw), full2(ps),
        ],
        out_specs=[tile(SSD_W), tile(POOL_W), state_spec],
        out_shape=[jax.ShapeDtypeStruct((nb, t, SSD_W), F32), jax.ShapeDtypeStruct((nb, t, POOL_W), F32),
                   jax.ShapeDtypeStruct((nb,) + st_shape, F32)],
        scratch_shapes=[pltpu.VMEM((nsub, CHUNK + 8, SSD_CONV_DIM), F32),
                        pltpu.VMEM((nsub,) + st_shape, F32),
                        pltpu.VMEM((nsub, CHUNK + 16, POOL_W), F32)],
        compiler_params=_cparams(("arbitrary", "arbitrary")),
        name="ssd_pool",
    )(xbc.reshape(nb, t, SSD_CONV_DIM), z.reshape(nb, t, SSD_W), dtg.reshape(nb, t, 128),
      u.reshape(nb, t, POOL_W), cprev, sprev_t, pprev, cw, cb, dtb, alog, dsk, nrm, pw, ps)
    snew = snew_t.reshape(nb, SSD_G, SSD_N, hg, SSD_HD).transpose(0, 1, 3, 4, 2).reshape(nb, SSD_H, SSD_HD, SSD_N)
    return yssd.reshape(nb * t, SSD_W), ypool.reshape(nb * t, POOL_W), snew


def _compress(load_rows, n_rows, w1_ref, pe_ref, w2_ref, pq_ref):
    acc = jnp.zeros((n_rows, 4 * CMP_HID), F32)
    cacc = jnp.zeros((8, 4 * CMP_HID), F32)
    for l2 in range(CMP_STRIDE // 2):
        w = w1_ref[l2]
        rows = jnp.concatenate([load_rows(2 * l2), load_rows(2 * l2 + 1)], axis=1)
        acc = acc + _dot(rows.astype(BF16), w)
        cacc = cacc + _dot(pe_ref[l2].astype(BF16), w)
    pq_ref[0:n_rows, :] = acc
    pq_ref[n_rows:n_rows + 8, :] = jnp.zeros((8, 4 * CMP_HID), F32)
    const = cacc[0:1, 0:2 * CMP_HID] + cacc[1:2, 2 * CMP_HID:4 * CMP_HID]
    hid = pq_ref[0:n_rows, 0:2 * CMP_HID] + pq_ref[1:n_rows + 1, 2 * CMP_HID:4 * CMP_HID] + const
    return _dot(_gelu_tanh(hid).astype(BF16), w2_ref[...])


def _stack_heads(x):
    return jnp.concatenate([x[:, h * NSA_HD:(h + 1) * NSA_HD] for h in range(NSA_H)], axis=0)


def _gated_mix(dtg, o_cmp, o_slc, o_win, r):
    sig = 1.0 / (1.0 + jnp.exp(-dtg))
    outs = []
    for h in range(NSA_H):
        b = SSD_H + 3 * h
        sl = slice(h * r, (h + 1) * r)
        outs.append(sig[:, b:b + 1] * o_cmp[sl] + sig[:, b + 1:b + 2] * o_slc[sl] + sig[:, b + 2:b + 3] * o_win[sl])
    return jnp.concatenate(outs, axis=1)


def _nsa_cmp_body(kv_ref, w1_ref, pe_ref, w2_ref, ckv_ref, pq_ref):
    n_ch = kv_ref.shape[0] // CMP_STRIDE
    ckv_ref[...] = _compress(lambda l: kv_ref[pl.ds(l, n_ch, stride=CMP_STRIDE), :], n_ch, w1_ref, pe_ref, w2_ref,
                             pq_ref)


def _nsa_cmp(kvc, w1, pe, w2, nb, t):
    n_ch = t // CMP_STRIDE
    return pl.pallas_call(
        _nsa_cmp_body,
        grid=(nb,),
        in_specs=[
            pl.BlockSpec((t, 128), lambda b: (b, 0)),
            pl.BlockSpec(w1.shape, lambda b: (0, 0, 0)),
            pl.BlockSpec(pe.shape, lambda b: (0, 0, 0)),
            pl.BlockSpec(w2.shape, lambda b: (0, 0)),
        ],
        out_specs=pl.BlockSpec((n_ch, 128), lambda b: (b, 0)),
        out_shape=jax.ShapeDtypeStruct((nb * n_ch, 128), F32),
        scratch_shapes=[pltpu.VMEM((n_ch + 8, 4 * CMP_HID), F32)],
        compiler_params=_cparams(("arbitrary",)),
        name="nsa_cmp",
    )(kvc, w1, pe, w2)


def _flash_step_t(carry, st, mask, vt):
    m, l, acc = carry
    sm = jnp.where(mask, st, NEG)
    m_new = jnp.maximum(m, jnp.max(sm, axis=0, keepdims=True))
    p = jnp.exp(sm - m_new)
    alpha = jnp.exp(m - m_new)
    l = alpha * l + jnp.sum(p, axis=0, keepdims=True)
    acc = alpha * acc + _dot(vt, p.astype(BF16))
    return m_new, l, acc


def _flash_init_t(cols):
    return jnp.full((1, cols), NEG, F32), jnp.zeros((1, cols), F32), jnp.zeros((NSA_HD, cols), F32)


def _nsa_attn_body(t, q_ref, dtg_ref, ckv_ref, kvs_ref, kvw_ref, y_ref, ks_ref, kst_ref, kw_ref, kwt_ref, selt_ref):
    s = pl.program_id(1)
    qb = CHUNK
    cols = NSA_H * qb
    n_kc = t // qb
    n_ch = t // CMP_STRIDE
    n_cmp = n_ch - 1
    n_slc = t // SLC_BLOCK
    nb_pad = -(-n_slc // 8) * 8
    n_sel = min(N_SELECT, n_slc)

    kb = ks_ref.shape[1] // qb
    n_wc = WINDOW // qb + 1

    @pl.when(s == 0)
    def _():
        for c in range(n_kc):
            tile = kvs_ref[c * qb:(c + 1) * qb, :]
            ks_ref[c // kb, (c % kb) * qb:(c % kb + 1) * qb, :] = tile.astype(BF16)
            kst_ref[c // kb, :, (c % kb) * qb:(c % kb + 1) * qb] = tile.T.astype(BF16)
            tile = kvw_ref[c * qb:(c + 1) * qb, :]
            kw_ref[c] = tile.astype(BF16)
            kwt_ref[c] = tile.T.astype(BF16)

    q_t = (q_ref[...] * (NSA_HD ** -0.5)).T
    qs_t = jnp.concatenate([q_t[h * NSA_HD:(h + 1) * NSA_HD, :] for h in range(NSA_H)], axis=1).astype(BF16)
    lane = lax.broadcasted_iota(jnp.int32, (1, cols), 1)
    tpos = s * qb + lane % qb

    ckv = ckv_ref[...]
    ck = ckv[:, 0:NSA_HD].astype(BF16)
    cv_t = ckv.T[NSA_HD:2 * NSA_HD, :].astype(BF16)
    sc = _dot(ck, qs_t)
    nrow = lax.broadcasted_iota(jnp.int32, (n_ch, cols), 0)
    cmask = (nrow * CMP_STRIDE + (CMP_LEN - 1) <= tpos) & (nrow < n_cmp)
    mx = jnp.max(jnp.where(cmask, sc, NEG), axis=0, keepdims=True)
    e = jnp.where(cmask, jnp.exp(sc - mx), 0.0)
    pc = e / jnp.maximum(jnp.sum(e, axis=0, keepdims=True), 1e-30)
    o_cmp = _dot(cv_t, pc.astype(BF16))
    pcsum = pc[:, 0:qb] + pc[:, qb:2 * qb] + pc[:, 2 * qb:3 * qb] + pc[:, 3 * qb:4 * qb]

    jrow = lax.broadcasted_iota(jnp.int32, (nb_pad, n_ch), 0)
    ncol = lax.broadcasted_iota(jnp.int32, (nb_pad, n_ch), 1)
    ov_t = ((ncol * CMP_STRIDE < (jrow + 1) * SLC_BLOCK) & (ncol * CMP_STRIDE + CMP_LEN > jrow * SLC_BLOCK)
            & (ncol < n_cmp)).astype(F32)
    imp = jnp.dot(ov_t, pcsum, precision=HIGHEST, preferred_element_type=F32)
    brow = lax.broadcasted_iota(jnp.int32, (nb_pad, qb), 0)
    tq = s * qb + lax.broadcasted_iota(jnp.int32, (nb_pad, qb), 1)
    cur = tq // SLC_BLOCK
    forced = (brow == 0) | (brow == cur) | (brow == cur - 1)
    imp = jnp.where(forced, jnp.inf, imp)
    imp = jnp.where((brow * SLC_BLOCK <= tq) & (brow < n_slc), imp, -jnp.inf)
    rank = jnp.zeros((nb_pad, qb), F32)
    for i in range(n_slc):
        ri = imp[i:i + 1, :]
        before = (ri > imp) | ((ri == imp) & (brow > i))
        rank = rank + before.astype(F32)
    sel_t = ((rank < n_sel) & (imp > -jnp.inf)).astype(F32)
    selt_ref[...] = jnp.concatenate([sel_t] * NSA_H, axis=1)

    bps = kb * qb // SLC_BLOCK
    krow_s = lax.broadcasted_iota(jnp.int32, (kb * qb, cols), 0)

    def slc_step(c, carry):
        k = ks_ref[c, :, 0:NSA_HD]
        vt = kst_ref[c, NSA_HD:2 * NSA_HD, :]
        sel_rows = selt_ref[pl.ds(pl.multiple_of(c * bps, bps), bps), :]
        selm = jnp.concatenate([jnp.broadcast_to(sel_rows[j:j + 1, :], (SLC_BLOCK, cols)) for j in range(bps)],
                               axis=0)
        mask = (selm > 0.5) & (c * (kb * qb) + krow_s <= tpos)
        return _flash_step_t(carry, _dot(k, qs_t), mask, vt)

    _, l_s, acc_s = lax.fori_loop(0, (s + kb) // kb, slc_step, _flash_init_t(cols))
    o_slc = acc_s / l_s

    k0 = jnp.maximum(s - (n_wc - 1), 0)
    kw = jnp.concatenate([kw_ref[k0 + i, :, 0:NSA_HD] for i in range(n_wc)], axis=0)
    vwt = jnp.concatenate([kwt_ref[k0 + i, NSA_HD:2 * NSA_HD, :] for i in range(n_wc)], axis=1)
    kpos = k0 * qb + lax.broadcasted_iota(jnp.int32, (n_wc * qb, cols), 0)
    sw = jnp.where((kpos <= tpos) & (kpos > tpos - WINDOW), _dot(kw, qs_t), NEG)
    pw = jnp.exp(sw - jnp.max(sw, axis=0, keepdims=True))
    o_win = _dot(vwt, pw.astype(BF16)) / jnp.sum(pw, axis=0, keepdims=True)

    sig = 1.0 / (1.0 + jnp.exp(-dtg_ref[...].T[0:32, :]))
    outs = []
    for h in range(NSA_H):
        b = SSD_H + 3 * h
        sl = slice(h * qb, (h + 1) * qb)
        outs.append(sig[b:b + 1, :] * o_cmp[:, sl] + sig[b + 1:b + 2, :] * o_slc[:, sl]
                    + sig[b + 2:b + 3, :] * o_win[:, sl])
    y_ref[...] = jnp.concatenate(outs, axis=0).T


def _nsa_attn(q, dtg, ckv, kvc, kvw, nb, t):
    nq = t // CHUNK
    n_ch = t // CMP_STRIDE
    kb = 4
    assert n_ch % 128 == 0 and nq % kb == 0 and nq > WINDOW // CHUNK
    nb_pad = -(-(t // SLC_BLOCK) // 8) * 8
    kv_scratch = pltpu.VMEM((nq, CHUNK, 2 * NSA_HD), BF16)
    return pl.pallas_call(
        functools.partial(_nsa_attn_body, t),
        grid=(nb, nq),
        in_specs=[
            pl.BlockSpec((CHUNK, NSA_W), lambda b, s: (b * nq + s, 0)),
            pl.BlockSpec((CHUNK, 128), lambda b, s: (b * nq + s, 0)),
            pl.BlockSpec((n_ch, 128), lambda b, s: (b, 0)),
            pl.BlockSpec((t, 128), lambda b, s: (b, 1)),
            pl.BlockSpec((t, 128), lambda b, s: (b, 0)),
        ],
        out_specs=pl.BlockSpec((CHUNK, NSA_W), lambda b, s: (b * nq + s, 0)),
        out_shape=jax.ShapeDtypeStruct((nb * t, NSA_W), F32),
        scratch_shapes=[pltpu.VMEM((nq // kb, kb * CHUNK, 2 * NSA_HD), BF16),
                        pltpu.VMEM((nq // kb, 2 * NSA_HD, kb * CHUNK), BF16),
                        kv_scratch, kv_scratch, pltpu.VMEM((nb_pad, NSA_H * CHUNK), F32)],
        compiler_params=_cparams(("arbitrary", "arbitrary")),
        name="nsa_attn",
    )(q, dtg, ckv, kvc, kvw)


def _softmax_pv_nt(sc, mask, vt):
    mx = jnp.max(jnp.where(mask, sc, NEG), axis=1, keepdims=True)
    e = jnp.where(mask, jnp.exp(sc - mx), 0.0)
    l = jnp.sum(e, axis=1, keepdims=True)
    return _dot_nt(e.astype(BF16), vt) / jnp.maximum(l, 1e-30)


def _rows_to_cols(x):
    r, w = x.shape
    return jnp.concatenate([x, jnp.zeros((128 - r, w), F32)], axis=0).T


def _nsa_dec_body(layer, n_phys, n_pages, nb, t, pt_ref, cache_ref, q_ref, dtg_ref, kvn_ref, kwn_ref, win_ref,
                  w1_ref, pe_ref, w2_ref, exp_ref, y_ref, xt_ref, xc_ref, pq_ref, wk_ref, sem):
    b = pl.program_id(0)
    past = n_pages * PAGE_SIZE
    lk = past + t
    n_ch = -(-lk // CMP_STRIDE)
    n_cmp = n_ch - 1
    xrows = xt_ref.shape[1] * PAGE_SIZE
    nch_pad = xrows // CMP_STRIDE
    n_slc = -(-lk // SLC_BLOCK)
    nblk = exp_ref.shape[0]
    n_sel = min(N_SELECT, n_slc)
    slot = b % 2

    def page_copy(bb, sl, p):
        phys = pt_ref[bb, p] + layer * n_phys
        return pltpu.make_async_copy(cache_ref.at[phys], xt_ref.at[sl, p], sem.at[sl])

    def start_all(bb, sl):
        def body(p, carry):
            page_copy(bb, sl, p).start()
            return carry
        lax.fori_loop(0, n_pages, body, 0)

    @pl.when(b == 0)
    def _():
        start_all(0, 0)

    @pl.when(b + 1 < nb)
    def _():
        start_all(b + 1, 1 - slot)

    def wait_body(p, carry):
        page_copy(b, slot, p).wait()
        return carry
    lax.fori_loop(0, n_pages, wait_body, 0)

    xt_ref[slot, n_pages] = _rows_to_cols(kvn_ref[...])

    eye = (lax.broadcasted_iota(jnp.int32, (128, 128), 0)
           == lax.broadcasted_iota(jnp.int32, (128, 128), 1)).astype(BF16)
    for p in range(n_pages + 1):
        tile = xt_ref[slot, p, 0:128, :].astype(BF16)
        xc_ref[p * PAGE_SIZE:(p + 1) * PAGE_SIZE, :] = _dot_nt(eye, tile)

    rows = NSA_H * t
    qs = _stack_heads(q_ref[...] * (NSA_HD ** -0.5)).astype(BF16)
    ti = lax.broadcasted_iota(jnp.int32, (rows, 1), 0) % t
    tpos = past + ti

    ckv = _compress(lambda l: xc_ref[pl.ds(l, nch_pad, stride=CMP_STRIDE), :], nch_pad, w1_ref, pe_ref,
                    w2_ref, pq_ref)
    nc_use = (n_cmp + 127) // 128 * 128
    ck = ckv[0:nc_use, 0:NSA_HD].astype(BF16)
    cv = ckv[0:nc_use, NSA_HD:2 * NSA_HD].astype(BF16)
    sc = _dot_nt(qs, ck)
    ncol = lax.broadcasted_iota(jnp.int32, (rows, nc_use), 1)
    cmask = (ncol * CMP_STRIDE + (CMP_LEN - 1) <= tpos) & (ncol < n_cmp)
    mx = jnp.max(jnp.where(cmask, sc, NEG), axis=1, keepdims=True)
    e = jnp.where(cmask, jnp.exp(sc - mx), 0.0)
    pc = e / jnp.maximum(jnp.sum(e, axis=1, keepdims=True), 1e-30)
    o_cmp = _dot(pc.astype(BF16), cv)
    pcsum = pc[0:t] + pc[t:2 * t] + pc[2 * t:3 * t] + pc[3 * t:4 * t]

    nrow = lax.broadcasted_iota(jnp.int32, (nc_use, nblk), 0)
    jcol = lax.broadcasted_iota(jnp.int32, (nc_use, nblk), 1)
    ov = ((nrow * CMP_STRIDE < (jcol + 1) * SLC_BLOCK) & (nrow * CMP_STRIDE + CMP_LEN > jcol * SLC_BLOCK)
          & (nrow < n_cmp)).astype(F32)
    imp = jnp.dot(pcsum, ov, precision=HIGHEST, preferred_element_type=F32)
    jl = lax.broadcasted_iota(jnp.int32, (t, nblk), 1)
    tq = past + lax.broadcasted_iota(jnp.int32, (t, nblk), 0)
    cur = tq // SLC_BLOCK
    forced = (jl == 0) | (jl == cur) | (jl == cur - 1)
    imp = jnp.where(forced, jnp.inf, imp)
    imp = jnp.where((jl * SLC_BLOCK <= tq) & (jl < n_slc), imp, -jnp.inf)
    imp_t = jnp.concatenate([imp, jnp.full((128 - t, nblk), -jnp.inf, F32)], axis=0).T
    nb_rows = -(-n_slc // 8) * 8
    irow = lax.broadcasted_iota(jnp.int32, (nb_rows, nblk), 0)
    jlane = lax.broadcasted_iota(jnp.int32, (nb_rows, nblk), 1)
    ranks = []
    for qi in range(t):
        colv = imp_t[0:nb_rows, qi:qi + 1]
        rowv = imp[qi:qi + 1, :]
        before = (colv > rowv) | ((colv == rowv) & (irow < jlane))
        ranks.append(jnp.sum(before.astype(F32), axis=0, keepdims=True))
    rank = jnp.concatenate(ranks, axis=0)
    sel = ((rank < n_sel) & (imp > -jnp.inf)).astype(BF16)
    selexp = _dot(sel, exp_ref[...])
    sel4 = jnp.concatenate([selexp] * NSA_H, axis=0)

    sc = jnp.concatenate([_dot(qs, xt_ref[slot, p, 2 * NSA_HD:3 * NSA_HD, :].astype(BF16))
                          for p in range(n_pages + 1)], axis=1)
    kpos = lax.broadcasted_iota(jnp.int32, (rows, xrows), 1)
    smask = (sel4 > 0.5) & (kpos <= tpos)
    mx = jnp.max(jnp.where(smask, sc, NEG), axis=1, keepdims=True)
    e = jnp.where(smask, jnp.exp(sc - mx), 0.0)
    acc = jnp.zeros((rows, NSA_HD), F32)
    for p in range(n_pages + 1):
        acc = acc + _dot_nt(e[:, p * PAGE_SIZE:(p + 1) * PAGE_SIZE].astype(BF16),
                            xt_ref[slot, p, 3 * NSA_HD:4 * NSA_HD, :].astype(BF16))
    o_slc = acc / jnp.maximum(jnp.sum(e, axis=1, keepdims=True), 1e-30)

    wrows = wk_ref.shape[1]
    wk_ref[:, 0:WINDOW] = win_ref[0]
    wk_ref[:, WINDOW:wrows] = _rows_to_cols(kwn_ref[...])
    kwt = wk_ref[0:NSA_HD, :].astype(BF16)
    vwt = wk_ref[NSA_HD:2 * NSA_HD, :].astype(BF16)
    wpos = past - WINDOW + lax.broadcasted_iota(jnp.int32, (rows, wrows), 1)
    wmask = (wpos >= 0) & (wpos <= tpos) & (wpos > tpos - WINDOW)
    o_win = _softmax_pv_nt(_dot(qs, kwt), wmask, vwt)

    y_ref[...] = _gated_mix(dtg_ref[...], o_cmp, o_slc, o_win, t)


def _nsa_dec(page_table, cache_t, q, dtg, kvc, kvw, win_t, w1, pe, w2, expand, layer, n_phys, nb, t):
    n_pages = page_table.shape[1]
    xrows = expand.shape[1]
    assert xrows == n_pages * PAGE_SIZE + 128 and t <= 128
    grid_spec = pltpu.PrefetchScalarGridSpec(
        num_scalar_prefetch=1,
        grid=(nb,),
        in_specs=[
            pl.BlockSpec(memory_space=pl.ANY),
            pl.BlockSpec((t, NSA_W), lambda b, pt: (b, 0)),
            pl.BlockSpec((t, 128), lambda b, pt: (b, 0)),
            pl.BlockSpec((t, 4 * NSA_HD), lambda b, pt: (b, 0)),
            pl.BlockSpec((t, 2 * NSA_HD), lambda b, pt: (b, 0)),
            pl.BlockSpec((1, 2 * NSA_HD, WINDOW), lambda b, pt: (layer * nb + b, 0, 0)),
            pl.BlockSpec(w1.shape, lambda b, pt: (0, 0, 0)),
            pl.BlockSpec(pe.shape, lambda b, pt: (0, 0, 0)),
            pl.BlockSpec(w2.shape, lambda b, pt: (0, 0)),
            pl.BlockSpec(expand.shape, lambda b, pt: (0, 0)),
        ],
        out_specs=pl.BlockSpec((t, NSA_W), lambda b, pt: (b, 0)),
        scratch_shapes=[
            pltpu.VMEM((2, n_pages + 1, 4 * NSA_HD, PAGE_SIZE), F32),
            pltpu.VMEM((xrows, 128), F32),
            pltpu.VMEM((xrows // CMP_STRIDE + 8, 4 * CMP_HID), F32),
            pltpu.VMEM((2 * NSA_HD, WINDOW + 128), F32),
            pltpu.SemaphoreType.DMA((2,)),
        ],
    )
    return pl.pallas_call(
        functools.partial(_nsa_dec_body, layer, n_phys, n_pages, nb, t),
        grid_spec=grid_spec,
        out_shape=jax.ShapeDtypeStruct((nb * t, NSA_W), F32),
        compiler_params=_cparams(("arbitrary",)),
        name="nsa_dec",
    )(page_table, cache_t, q, dtg, kvc, kvw, win_t, w1, pe, w2, expand)


def _out_proj_body(ys_ref, yp_ref, yn_ref, x_ref, w_ref, npost_ref, nffn_ref, x1_ref, h2_ref):
    mix = (_dot(ys_ref[...].astype(BF16), w_ref[0, 0:SSD_W, :])
           + _dot(yp_ref[...].astype(BF16), w_ref[0, SSD_W:SSD_W + POOL_W, :])
           + _dot(yn_ref[...].astype(BF16), w_ref[0, SSD_W + POOL_W:D_MODEL, :]))
    x1 = x_ref[...] + _rmsnorm(mix, npost_ref[...])
    x1_ref[...] = x1
    h2_ref[...] = _rmsnorm(x1, nffn_ref[...]).astype(BF16)


def _out_proj(ys, yp, yn, x2d, w_out, layer, npost, nffn, tm):
    m = x2d.shape[0]

    def tile(w):
        return pl.BlockSpec((tm, w), lambda i: (i, 0))

    vec = pl.BlockSpec((1, D_MODEL), lambda i: (0, 0))
    return pl.pallas_call(
        _out_proj_body,
        grid=(m // tm,),
        in_specs=[tile(SSD_W), tile(POOL_W), tile(NSA_W), tile(D_MODEL),
                  pl.BlockSpec((1, D_MODEL, D_MODEL), lambda i: (layer, 0, 0)), vec, vec],
        out_specs=[tile(D_MODEL), tile(D_MODEL)],
        out_shape=[jax.ShapeDtypeStruct((m, D_MODEL), F32), jax.ShapeDtypeStruct((m, D_MODEL), BF16)],
        compiler_params=_cparams(("arbitrary",)),
        name="out_proj",
    )(ys, yp, yn, x2d, w_out, npost, nffn)


def _ffn_body(carry_mode, tiles_per_seq, t, tf, *refs):
    if carry_mode:
        (h2_ref, x1_ref, wg_ref, wv_ref, wd_ref, cw_ref, cb_ref, npost_ref,
         x2_ref, gsave_ref, act_ref, gext_ref, carry_ref) = refs
    else:
        (h2_ref, x1_ref, wg_ref, wv_ref, wd_ref, cw_ref, cb_ref, npost_ref, prev_ref,
         x2_ref, gsave_ref, act_ref) = refs
    tm = h2_ref.shape[0]
    h2 = h2_ref[...]
    if carry_mode:
        @pl.when(pl.program_id(0) % tiles_per_seq == 0)
        def _():
            carry_ref[...] = jnp.zeros(carry_ref.shape, F32)
    else:
        r = lax.broadcasted_iota(jnp.int32, (tm, tf), 0) % t
    for f in range(D_FF // tf):
        fs = slice(f * tf, (f + 1) * tf)
        g = _dot(h2, wg_ref[0, :, fs])
        v = _dot(h2, wv_ref[0, :, fs])
        if carry_mode:
            buf = f % 2
            gext_ref[buf, 0:8, :] = carry_ref[:, fs]
            gext_ref[buf, 8:8 + tm, :] = g
            carry_ref[:, fs] = g[tm - 8:tm]
            gsave_ref[0, :, fs] = g[tm - 8:tm]
            g1 = gext_ref[buf, 7:7 + tm, :]
            g2 = gext_ref[buf, 6:6 + tm, :]
        else:
            prev = prev_ref[:, fs]
            g1 = jnp.where(r >= 1, pltpu.roll(g, 1, 0), pltpu.roll(prev, tm - 1, 0))
            g2 = jnp.where(r >= 2, pltpu.roll(g, 2, 0), prev)
            gsave_ref[:, fs] = g
        cw = cw_ref[:, fs]
        gc = g * cw[2:3, :] + g1 * cw[1:2, :] + g2 * cw[0:1, :] + cb_ref[:, fs]
        act_ref[:, fs] = (_gelu_tanh(gc) * v).astype(BF16)
    out = _dot(act_ref[...], wd_ref[0])
    x2_ref[...] = x1_ref[...] + _rmsnorm(out, npost_ref[...])


def _ffn(h2, x1, wg, wv, wd, layer, cw, cb, npost, tm, tf, carry_mode, tiles_per_seq, t, prev_rows=None):
    m = h2.shape[0]
    nm = m // tm
    resident = pl.Buffered(1)
    in_specs = [
        pl.BlockSpec((tm, D_MODEL), lambda i: (i, 0)),
        pl.BlockSpec((tm, D_MODEL), lambda i: (i, 0)),
        pl.BlockSpec((1, D_MODEL, D_FF), lambda i: (layer, 0, 0), pipeline_mode=resident),
        pl.BlockSpec((1, D_MODEL, D_FF), lambda i: (layer, 0, 0), pipeline_mode=resident),
        pl.BlockSpec((1, D_FF, D_MODEL), lambda i: (layer, 0, 0), pipeline_mode=resident),
        pl.BlockSpec((FFN_CONV, D_FF), lambda i: (0, 0)),
        pl.BlockSpec((1, D_FF), lambda i: (0, 0)),
        pl.BlockSpec((1, D_MODEL), lambda i: (0, 0)),
    ]
    args = [h2, x1, wg, wv, wd, cw, cb, npost]
    scratch = [pltpu.VMEM((tm, D_FF), BF16)]
    if carry_mode:
        gsave_shape = jax.ShapeDtypeStruct((nm, 8, D_FF), F32)
        gsave_spec = pl.BlockSpec((1, 8, D_FF), lambda i: (i, 0, 0))
        scratch += [pltpu.VMEM((2, tm + 8, tf), F32), pltpu.VMEM((8, D_FF), F32)]
    else:
        gsave_shape = jax.ShapeDtypeStruct((m, D_FF), F32)
        gsave_spec = pl.BlockSpec((tm, D_FF), lambda i: (i, 0))
        in_specs.append(pl.BlockSpec((tm, D_FF), lambda i: (i, 0)))
        args.append(prev_rows)
    return pl.pallas_call(
        functools.partial(_ffn_body, carry_mode, tiles_per_seq, t, tf),
        grid=(nm,),
        in_specs=in_specs,
        out_specs=[pl.BlockSpec((tm, D_MODEL), lambda i: (i, 0)), gsave_spec],
        out_shape=[jax.ShapeDtypeStruct((m, D_MODEL), F32), gsave_shape],
        scratch_shapes=scratch,
        compiler_params=_cparams(("arbitrary",)),
        name="ffn",
    )(*args)


def _prep_params(w_in, ssd_conv_w, ssd_conv_b, ssd_dt_bias, ssd_a_log, ssd_d, ssd_norm, pool_w, pool_scale,
                 nsa_pe_k, nsa_pe_v, nsa_w1_k, nsa_w1_v, nsa_w2_k, nsa_w2_v, w_out, ffn_w_gate, ffn_w_val,
                 ffn_w_down):
    depth = w_in.shape[0]
    w_in_p = jnp.concatenate(
        [w_in[..., 0:OFF_XBC], w_in[..., OFF_DT:OFF_KV], w_in[..., OFF_XBC:OFF_DT], w_in[..., OFF_KV:IN_W],
         jnp.zeros((depth, D_MODEL, IN_W_PAD - IN_W), F32)], axis=-1).astype(BF16)
    pad_h = ((0, 0), (0, 128 - SSD_H))
    dtb = jnp.pad(ssd_dt_bias, pad_h)[:, None, :]
    alog = jnp.pad(ssd_a_log, pad_h)[:, None, :]
    dsk = jnp.repeat(ssd_d, SSD_HD, axis=1)[:, None, :]
    n_g = len(POOL_WINDOWS)
    eye = jnp.eye(n_g, dtype=F32)
    pw_bd = (pool_w[:, :, :, None, :] * eye[None, :, None, :, None]).reshape(depth, POOL_W, POOL_W).astype(BF16)
    zk = jnp.zeros((depth, CMP_STRIDE, NSA_HD, CMP_HID), F32)
    top = jnp.concatenate([nsa_w1_k[:, :CMP_STRIDE], zk, nsa_w1_k[:, CMP_STRIDE:], zk], axis=-1)
    bot = jnp.concatenate([zk, nsa_w1_v[:, :CMP_STRIDE], zk, nsa_w1_v[:, CMP_STRIDE:]], axis=-1)
    w1 = jnp.concatenate([top, bot], axis=2).astype(BF16).reshape(depth, CMP_STRIDE // 2, 256, 4 * CMP_HID)
    pe_a = jnp.concatenate([nsa_pe_k[:, :CMP_STRIDE], nsa_pe_v[:, :CMP_STRIDE]], axis=-1)
    pe_b = jnp.concatenate([nsa_pe_k[:, CMP_STRIDE:], nsa_pe_v[:, CMP_STRIDE:]], axis=-1)
    pe = jnp.concatenate([pe_a[:, :, None], pe_b[:, :, None], jnp.zeros((depth, CMP_STRIDE, 6, 128), F32)], axis=2)
    pe = pe.reshape(depth, CMP_STRIDE // 2, 2, 8, 128).transpose(0, 1, 3, 2, 4).reshape(depth, CMP_STRIDE // 2, 8, 256)
    zw = jnp.zeros((depth, CMP_HID, NSA_HD), F32)
    w2 = jnp.concatenate([jnp.concatenate([nsa_w2_k, zw], axis=-1), jnp.concatenate([zw, nsa_w2_v], axis=-1)],
                         axis=1).astype(BF16)
    return dict(w_in=w_in_p, cw=ssd_conv_w, cb=ssd_conv_b[:, None, :], dtb=dtb, alog=alog, dsk=dsk,
                nrm=ssd_norm[:, None, :], pw=pw_bd, ps=pool_scale[:, None, :], w1=w1, pe=pe, w2=w2,
                w_out=w_out.astype(BF16), wg=ffn_w_gate.astype(BF16), wv=ffn_w_val.astype(BF16),
                wd=ffn_w_down.astype(BF16))


def _mixer_weights(p, l):
    return (p['cw'][l], p['cb'][l], p['dtb'][l], p['alog'][l], p['dsk'][l], p['nrm'][l], p['pw'][l], p['ps'][l])


def kernel(x_prompt, x_sample, cache_nsa_kv, page_table, state_nsa_win, state_ssd_conv, state_ssm, state_pool, state_ffn_conv, norm_mix_pre, w_in, ssd_conv_w, ssd_conv_b, ssd_dt_bias, ssd_a_log, ssd_d, ssd_norm, pool_w, pool_scale, nsa_pe_k, nsa_pe_v, nsa_w1_k, nsa_w1_v, nsa_w2_k, nsa_w2_v, w_out, norm_mix_post, norm_ffn_pre, ffn_w_gate, ffn_w_val, ffn_conv_w, ffn_conv_b, ffn_w_down, norm_ffn_post):
    depth = w_in.shape[0]
    bp, tp, _ = x_prompt.shape
    bs, ts, _ = x_sample.shape
    n_phys = cache_nsa_kv.shape[1]
    n_pages = page_table.shape[1]
    past = n_pages * PAGE_SIZE
    assert tp % 512 == 0 and tp >= WINDOW and ts == 8 and state_nsa_win.shape[2] == WINDOW
    p = _prep_params(w_in, ssd_conv_w, ssd_conv_b, ssd_dt_bias, ssd_a_log, ssd_d, ssd_norm, pool_w, pool_scale,
                     nsa_pe_k, nsa_pe_v, nsa_w1_k, nsa_w1_v, nsa_w2_k, nsa_w2_v, w_out, ffn_w_gate, ffn_w_val,
                     ffn_w_down)
    ms = bs * ts
    tab_p = _rope_table(jnp.arange(tp, dtype=jnp.int32))
    tab_s = jnp.tile(_rope_table(past + jnp.arange(ts, dtype=jnp.int32)), (bs, 1))
    cache = jnp.transpose(cache_nsa_kv, (0, 1, 3, 4, 2)).reshape(depth * n_phys, 4 * NSA_HD, PAGE_SIZE)
    win_state = jnp.transpose(state_nsa_win, (0, 1, 3, 4, 2)).reshape(depth * bs, 2 * NSA_HD, WINDOW)
    lk = past + ts
    xrows = -(-(-(-lk // CMP_STRIDE)) // 8) * 8 * CMP_STRIDE
    xrows = -(-xrows // 128) * 128
    nblk = -(-(xrows // SLC_BLOCK) // 128) * 128
    expand = (jnp.arange(nblk, dtype=jnp.int32)[:, None]
              == jnp.arange(xrows, dtype=jnp.int32)[None, :] // SLC_BLOCK).astype(BF16)

    tm_p = 512
    tm_f = 512
    tf = 512
    nc_p = tp // CHUNK
    zeros_c = jnp.zeros((bp, 8, SSD_CONV_DIM), F32)
    zeros_s = jnp.zeros((bp, SSD_H, SSD_HD, SSD_N), F32)
    zeros_p = jnp.zeros((bp, 16, POOL_W), F32)

    xp = x_prompt.reshape(bp * tp, D_MODEL)
    xs = x_sample.reshape(ms, D_MODEL)
    outs_p, outs_s = [], []
    for l in range(depth):
        nw = norm_mix_pre[l][None]
        npost = norm_mix_post[l][None]
        nffn = norm_ffn_pre[l][None]
        nfpost = norm_ffn_post[l][None]
        mw = _mixer_weights(p, l)
        fcw, fcb = ffn_conv_w[l], ffn_conv_b[l][None]

        z, xbc, u, q, kvc, kvw, dtg = _in_proj(xp, nw, p['w_in'], l, tab_p, tm_p)
        y_ssd, y_pool, ssm_new = _ssd_pool(xbc, z, dtg, u, zeros_c, zeros_s, zeros_p, mw, bp, nc_p, CHUNK, 0, 2)
        ckv = _nsa_cmp(kvc, p['w1'][l], p['pe'][l], p['w2'][l], bp, tp)
        y_nsa = _nsa_attn(q, dtg, ckv, kvc, kvw, bp, tp)
        x1, h2 = _out_proj(y_ssd, y_pool, y_nsa, xp, p['w_out'], l, npost, nffn, tm_p)
        xp, gsave = _ffn(h2, x1, p['wg'], p['wv'], p['wd'], l, fcw, fcb, nfpost, tm_f, tf, True, tp // tm_f, tp)
        outs_p.append((
            kvc.reshape(bp, tp, 4, NSA_HD),
            kvw.reshape(bp, tp, 2, NSA_HD)[:, tp - WINDOW:],
            xbc.reshape(bp, tp, SSD_CONV_DIM)[:, tp - (SSD_CONV - 1):],
            ssm_new,
            u.reshape(bp, tp, POOL_W)[:, tp - POOL_KEEP:],
            gsave.reshape(bp, tp // tm_f, 8, D_FF)[:, -1, 8 - (FFN_CONV - 1):],
        ))

        z, xbc, u, q, kvc, kvw, dtg = _in_proj(xs, nw, p['w_in'], l, tab_s, ms)
        cprev = jnp.pad(state_ssd_conv[l], ((0, 0), (8 - (SSD_CONV - 1), 0), (0, 0)))
        pprev = jnp.pad(state_pool[l], ((0, 0), (16 - POOL_KEEP, 0), (0, 0)))
        y_ssd, y_pool, ssm_new = _ssd_pool(xbc, z, dtg, u, cprev, state_ssm[l], pprev, mw, bs, 1, ts, past, 4)
        y_nsa = _nsa_dec(page_table, cache, q, dtg, kvc, kvw, win_state, p['w1'][l], p['pe'][l], p['w2'][l],
                         expand, l, n_phys, bs, ts)
        x1, h2 = _out_proj(y_ssd, y_pool, y_nsa, xs, p['w_out'], l, npost, nffn, ms)
        fprev = state_ffn_conv[l]
        fprev_rows = jnp.pad(fprev, ((0, 0), (0, ts - (FFN_CONV - 1)), (0, 0))).reshape(ms, D_FF)
        xs, gsave = _ffn(h2, x1, p['wg'], p['wv'], p['wd'], l, fcw, fcb, nfpost, ms, tf, False, 1, ts, fprev_rows)
        kvw3 = kvw.reshape(bs, ts, 2, NSA_HD)
        xbc3 = xbc.reshape(bs, ts, SSD_CONV_DIM)
        u3 = u.reshape(bs, ts, POOL_W)
        g3 = gsave.reshape(bs, ts, D_FF)
        outs_s.append((
            kvc.reshape(bs, ts, 4, NSA_HD),
            jnp.concatenate([state_nsa_win[l], kvw3], axis=1)[:, -WINDOW:],
            jnp.concatenate([state_ssd_conv[l], xbc3], axis=1)[:, -(SSD_CONV - 1):],
            ssm_new,
            jnp.concatenate([state_pool[l], u3], axis=1)[:, -POOL_KEEP:],
            jnp.concatenate([state_ffn_conv[l], g3], axis=1)[:, -(FFN_CONV - 1):],
        ))

    def stk(outs, i):
        return jnp.stack([o[i] for o in outs])

    return (xp.reshape(bp, tp, D_MODEL), xs.reshape(bs, ts, D_MODEL), stk(outs_p, 0), stk(outs_s, 0),
            stk(outs_p, 1), stk(outs_s, 1), stk(outs_p, 2), stk(outs_s, 2), stk(outs_p, 3), stk(outs_s, 3),
            stk(outs_p, 4), stk(outs_s, 4), stk(outs_p, 5), stk(outs_s, 5))
```

```python
import functools
import math

import jax
import jax.numpy as jnp
from jax import lax
from jax.experimental import pallas as pl
from jax.experimental.pallas import tpu as pltpu

F32 = jnp.float32
BF16 = jnp.bfloat16
HIGHEST = lax.Precision.HIGHEST

D_MODEL = 1024
PAGE_SIZE = 128
SSD_W = 512
SSD_HD = 64
SSD_H = 8
SSD_N = 128
SSD_G = 2
SSD_CONV = 4
SSD_CONV_DIM = 1024
POOL_W = 256
POOL_WINDOWS = (2, 4, 8, 16)
POOL_GC = 64
POOL_KEEP = 15
NSA_W = 256
NSA_HD = 64
NSA_H = 4
CMP_STRIDE = 16
CMP_LEN = 32
CMP_HID = 128
SLC_BLOCK = 64
N_SELECT = 16
WINDOW = 512
ROPE_DIM = 16
ROPE_THETA = 500000.0
D_FF = 4096
FFN_CONV = 3
RMS_EPS = 1e-6
OFF_Z = 512
OFF_XBC = 1536
OFF_DT = 1544
OFF_POOL = 1800
OFF_Q = 2056
OFF_KV = 2440
IN_W = 2452
IN_W_PAD = 2560
CHUNK = 128
NEG = -1e30
LOG2E = 1.4426950408889634
VMEM_LIMIT = 56 * 1024 * 1024


def _cparams(sem):
    return pltpu.CompilerParams(dimension_semantics=sem, vmem_limit_bytes=VMEM_LIMIT)


def _rmsnorm(x, w):
    return x * lax.rsqrt(jnp.mean(x * x, axis=-1, keepdims=True) + RMS_EPS) * w


def _silu(x):
    return x / (1.0 + jnp.exp(-x))


def _gelu_tanh(x):
    k = 0.7978845608028654
    hx = 0.5 * x
    return hx + hx * jnp.tanh(x * (k + (k * 0.044715) * (x * x)))


def _dot(a, b):
    return jnp.dot(a, b, preferred_element_type=F32)


def _dot_nt(a, b):
    return lax.dot_general(a, b, (((1,), (1,)), ((), ())), preferred_element_type=F32)


def _dot_tn(a, b):
    return lax.dot_general(a, b, (((0,), (0,)), ((), ())), preferred_element_type=F32)


def _in_proj_body(x_ref, nw_ref, w_ref, tab_ref, z_ref, xbc_ref, pool_ref, q_ref, kvc_ref, kvw_ref, dtg_ref):
    tm = x_ref.shape[0]
    h = _rmsnorm(x_ref[...], nw_ref[...]).astype(BF16)

    def proj(a, b):
        return _dot(h, w_ref[0, :, a:b])

    z_ref[...] = proj(0, 512)
    xbc_ref[...] = proj(512, 1536)
    pool_ref[...] = proj(1536, 1792)
    dtg_ref[...] = proj(2432, 2560)
    lane = lax.broadcasted_iota(jnp.int32, (tm, 128), 1)
    first = (lane % NSA_HD) < (ROPE_DIM // 2)

    def rope(x, c, s):
        outs = []
        for j in range(x.shape[1] // 128):
            xj = x[:, j * 128:(j + 1) * 128]
            rot = jnp.where(first, pltpu.roll(xj, 128 - ROPE_DIM // 2, 1), pltpu.roll(xj, ROPE_DIM // 2, 1))
            outs.append(xj * c + rot * s)
        return outs

    q = rope(proj(1792, 2048), tab_ref[:, 0:128], tab_ref[:, 128:256])
    q_ref[:, 0:128] = q[0]
    q_ref[:, 128:256] = q[1]
    kv = rope(proj(2048, 2432), tab_ref[:, 256:384], tab_ref[:, 384:512])
    kvc_ref[:, 0:128] = kv[0]
    kvc_ref[:, 128:256] = kv[1]
    kvw_ref[...] = kv[2]


def _in_proj(x2d, nw, w_in_p, layer, tab, tm):
    m = x2d.shape[0]
    tab_blocks = tab.shape[0] // tm
    widths = (512, 1024, 256, 256, 256, 128, 128)
    return pl.pallas_call(
        _in_proj_body,
        grid=(m // tm,),
        in_specs=[
            pl.BlockSpec((tm, D_MODEL), lambda i: (i, 0)),
            pl.BlockSpec((1, D_MODEL), lambda i: (0, 0)),
            pl.BlockSpec((1, D_MODEL, IN_W_PAD), lambda i: (layer, 0, 0)),
            pl.BlockSpec((tm, 512), lambda i: (i % tab_blocks, 0)),
        ],
        out_specs=[pl.BlockSpec((tm, w), lambda i: (i, 0)) for w in widths],
        out_shape=[jax.ShapeDtypeStruct((m, w), F32) for w in widths],
        compiler_params=_cparams(("arbitrary",)),
        name="in_proj",
    )(x2d, nw, w_in_p, tab)


def _rope_table(pos):
    half = ROPE_DIM // 2
    inv = 1.0 / (ROPE_THETA ** (jnp.arange(half, dtype=F32) / half))
    ang = pos.astype(F32)[:, None] * inv
    cos, sin = jnp.cos(ang), jnp.sin(ang)
    t = pos.shape[0]
    c_rot = jnp.concatenate([cos, cos, jnp.ones((t, NSA_HD - ROPE_DIM), F32)], axis=1)
    s_rot = jnp.concatenate([-sin, sin, jnp.zeros((t, NSA_HD - ROPE_DIM), F32)], axis=1)
    one, zero = jnp.ones((t, NSA_HD), F32), jnp.zeros((t, NSA_HD), F32)
    return jnp.concatenate([c_rot, c_rot, s_rot, s_rot, c_rot, one, s_rot, zero], axis=1)


def _ssd_pool_body(lr, nc, pos0, nsub, xbc_ref, z_ref, dtg_ref, u_ref, cprev_ref, sprev_ref, pprev_ref,
                   cw_ref, cb_ref, dtb_ref, alog_ref, dsk_ref, nrm_ref, pw_ref, ps_ref,
                   yssd_ref, ypool_ref, snew_ref, ext_ref, state_ref, pext_ref):
    c = pl.program_id(1)
    L = CHUNK

    @pl.when(c == 0)
    def _():
        ext_ref[:, 0:8, :] = cprev_ref[...]
        pext_ref[:, 0:16, :] = pprev_ref[...]
        state_ref[...] = sprev_ref[...]

    if nc > 1:
        @pl.when(c > 0)
        def _():
            ext_ref[:, 0:8, :] = ext_ref[:, L:L + 8, :]
            pext_ref[:, 0:16, :] = pext_ref[:, L:L + 16, :]

    ext_ref[:, 8:8 + lr, :] = xbc_ref[...]
    pext_ref[:, 16:16 + lr, :] = u_ref[...]
    if lr < L:
        ext_ref[:, 8 + lr:8 + L, :] = jnp.zeros((nsub, L - lr, SSD_CONV_DIM), F32)
        pext_ref[:, 16 + lr:16 + L, :] = jnp.zeros((nsub, L - lr, POOL_W), F32)

    cw = cw_ref[...]
    row = lax.broadcasted_iota(jnp.int32, (L, L), 0)
    col = lax.broadcasted_iota(jnp.int32, (L, L), 1)
    tri = row >= col
    tri_f = tri.astype(F32)
    eye_f = (row == col).astype(F32)
    eye_b = eye_f.astype(BF16)
    gw = (SSD_H // SSD_G) * SSD_HD
    expand = (lax.broadcasted_iota(jnp.int32, (128, SSD_W), 0)
              == lax.broadcasted_iota(jnp.int32, (128, SSD_W), 1) // SSD_HD).astype(F32)
    neg_a = -jnp.exp(alog_ref[...])
    dsk = dsk_ref[...]
    prow = lax.broadcasted_iota(jnp.int32, (L, POOL_W), 0)
    plane = lax.broadcasted_iota(jnp.int32, (L, POOL_W), 1)
    pos = pos0 + c * L + prow + 1

    for sub in range(nsub):
        conv = (ext_ref[sub, 8:8 + L, :] * cw[3:4, :] + ext_ref[sub, 7:7 + L, :] * cw[2:3, :]
                + ext_ref[sub, 6:6 + L, :] * cw[1:2, :] + ext_ref[sub, 5:5 + L, :] * cw[0:1, :] + cb_ref[...])
        xbc = _silu(conv)
        xs = xbc[:, 0:SSD_W]
        bmat = [xbc[:, SSD_W + g * SSD_N:SSD_W + (g + 1) * SSD_N].astype(BF16) for g in range(SSD_G)]
        cmat = [xbc[:, SSD_W + (SSD_G + g) * SSD_N:SSD_W + (SSD_G + g + 1) * SSD_N].astype(BF16)
                for g in range(SSD_G)]

        if lr == L:
            dtraw = dtg_ref[sub]
        else:
            dtraw = jnp.concatenate([dtg_ref[sub], jnp.zeros((L - lr, 128), F32)], axis=0)
        xdt = dtraw + dtb_ref[...]
        dt = jnp.maximum(xdt, 0.0) + jnp.log1p(jnp.exp(-jnp.abs(xdt)))
        dt = jnp.where((col < SSD_H) & (row < lr), dt, 0.0)
        acum = jnp.dot(tri_f, dt * neg_a, precision=HIGHEST, preferred_element_type=F32)
        arow = lax.dot_general(eye_f, acum, (((1,), (1,)), ((), ())), precision=HIGHEST,
                               preferred_element_type=F32)
        dt_x = jnp.dot(dt, expand, precision=HIGHEST, preferred_element_type=F32)
        ac_x = jnp.dot(acum, expand, precision=HIGHEST, preferred_element_type=F32)
        alast_x = ac_x[L - 1:L, :]
        xdt = xs * dt_x
        xdt_b = xdt.astype(BF16)
        xw = (xdt * jnp.exp(alast_x - ac_x)).astype(BF16)
        cb_scores = [_dot_nt(cmat[g], bmat[g]) for g in range(SSD_G)]
        y_off = []
        for g in range(SSD_G):
            gs = slice(g * gw, (g + 1) * gw)
            st = state_ref[sub, g]
            y_off.append(_dot(cmat[g], st.astype(BF16)))
            b_t = _dot_nt(eye_b, bmat[g]).astype(BF16)
            state_ref[sub, g] = jnp.exp(alast_x[:, gs]) * st + _dot(b_t, xw[:, gs])
        y_diag = []
        for hp in range(SSD_H // 2):
            tile_b = xdt_b[:, hp * 128:(hp + 1) * 128]
            acc = None
            for h in (2 * hp, 2 * hp + 1):
                g = h // (SSD_H // SSD_G)
                decay = jnp.exp(jnp.where(tri, acum[:, h:h + 1] - arow[h:h + 1, :], NEG))
                own = (col < SSD_HD) if h % 2 == 0 else (col >= SSD_HD)
                part = _dot((cb_scores[g] * decay).astype(BF16), jnp.where(own, tile_b, jnp.zeros_like(tile_b)))
                acc = part if acc is None else acc + part
            y_diag.append(acc)
        y = jnp.concatenate(y_diag, axis=1) + jnp.concatenate(y_off, axis=1) * jnp.exp(ac_x) + dsk * xs
        if lr < L:
            y = y[0:lr]
        y = y * _silu(z_ref[sub])
        yssd_ref[sub] = _rmsnorm(y, nrm_ref[...])

        e1 = pext_ref[sub]
        s2 = e1 + pltpu.roll(e1, 1, 0)
        s4 = s2 + pltpu.roll(s2, 2, 0)
        s8 = s4 + pltpu.roll(s4, 4, 0)
        s16 = s8 + pltpu.roll(s8, 8, 0)
        pooled = jnp.zeros((L, POOL_W), F32)
        for gi, (w, sw) in enumerate(zip(POOL_WINDOWS, (s2, s4, s8, s16))):
            cnt = jnp.minimum(pos, w).astype(F32)
            grp = (plane >= gi * POOL_GC) & (plane < (gi + 1) * POOL_GC)
            pooled = jnp.where(grp, sw[16:16 + L] / cnt, pooled)
        pooled = pooled - e1[16:16 + L]
        yp = _dot(pooled.astype(BF16), pw_ref[...]) * ps_ref[...]
        ypool_ref[sub] = yp[0:lr] if lr < L else yp

    @pl.when(c == nc - 1)
    def _():
        snew_ref[...] = state_ref[...]


def _ssd_pool(xbc, z, dtg, u, cprev, sprev, pprev, wts, nb, nc, lr, pos0, nsub):
    cw, cb, dtb, alog, dsk, nrm, pw, ps = wts
    t = nc * lr
    assert nb % nsub == 0

    def tile(w):
        return pl.BlockSpec((nsub, lr, w), lambda b, c: (b, c, 0))

    def full2(a):
        return pl.BlockSpec(a.shape, lambda b, c: (0, 0))

    hg = SSD_H // SSD_G
    st_shape = (SSD_G, SSD_N, hg * SSD_HD)
    sprev_t = sprev.reshape(nb, SSD_G, hg, SSD_HD, SSD_N).transpose(0, 1, 4, 2, 3).reshape((nb,) + st_shape)
    state_spec = pl.BlockSpec((nsub,) + st_shape, lambda b, c: (b, 0, 0, 0))
    yssd, ypool, snew_t = pl.pallas_call(
        functools.partial(_ssd_pool_body, lr, nc, pos0, nsub),
        grid=(nb // nsub, nc),
        in_specs=[
            tile(SSD_CONV_DIM), tile(SSD_W), tile(128), tile(POOL_W),
            pl.BlockSpec((nsub, 8, SSD_CONV_DIM), lambda b, c: (b, 0, 0)),
            state_spec,
            pl.BlockSpec((nsub, 16, POOL_W), lambda b, c: (b, 0, 0)),
            full2(cw), full2(cb), full2(dtb), full2(alog), full2(dsk), full2(nrm), full2(pw), full2(ps),
        ],
        out_specs=[tile(SSD_W), tile(POOL_W), state_spec],
        out_shape=[jax.ShapeDtypeStruct((nb, t, SSD_W), F32), jax.ShapeDtypeStruct((nb, t, POOL_W), F32),
                   jax.ShapeDtypeStruct((nb,) + st_shape, F32)],
        scratch_shapes=[pltpu.VMEM((nsub, CHUNK + 8, SSD_CONV_DIM), F32),
                        pltpu.VMEM((nsub,) + st_shape, F32),
                        pltpu.VMEM((nsub, CHUNK + 16, POOL_W), F32)],
        compiler_params=_cparams(("arbitrary", "arbitrary")),
        name="ssd_pool",
    )(xbc.reshape(nb, t, SSD_CONV_DIM), z.reshape(nb, t, SSD_W), dtg.reshape(nb, t, 128),
      u.reshape(nb, t, POOL_W), cprev, sprev_t, pprev, cw, cb, dtb, alog, dsk, nrm, pw, ps)
    snew = snew_t.reshape(nb, SSD_G, SSD_N, hg, SSD_HD).transpose(0, 1, 3, 4, 2).reshape(nb, SSD_H, SSD_HD, SSD_N)
    return yssd.reshape(nb * t, SSD_W), ypool.reshape(nb * t, POOL_W), snew


def _compress(load_rows, n_rows, w1_ref, pe_ref, w2_ref, pq_ref):
    acc = jnp.zeros((n_rows, 4 * CMP_HID), F32)
    cacc = jnp.zeros((8, 4 * CMP_HID), F32)
    for l2 in range(CMP_STRIDE // 2):
        w = w1_ref[l2]
        rows = jnp.concatenate([load_rows(2 * l2), load_rows(2 * l2 + 1)], axis=1)
        acc = acc + _dot(rows.astype(BF16), w)
        cacc = cacc + _dot(pe_ref[l2].astype(BF16), w)
    pq_ref[0:n_rows, :] = acc
    pq_ref[n_rows:n_rows + 8, :] = jnp.zeros((8, 4 * CMP_HID), F32)
    const = cacc[0:1, 0:2 * CMP_HID] + cacc[1:2, 2 * CMP_HID:4 * CMP_HID]
    hid = pq_ref[0:n_rows, 0:2 * CMP_HID] + pq_ref[1:n_rows + 1, 2 * CMP_HID:4 * CMP_HID] + const
    return _dot(_gelu_tanh(hid).astype(BF16), w2_ref[...])


def _stack_heads(x):
    return jnp.concatenate([x[:, h * NSA_HD:(h + 1) * NSA_HD] for h in range(NSA_H)], axis=0)


def _gated_mix(dtg, o_cmp, o_slc, o_win, r):
    sig = 1.0 / (1.0 + jnp.exp(-dtg))
    outs = []
    for h in range(NSA_H):
        b = SSD_H + 3 * h
        sl = slice(h * r, (h + 1) * r)
        outs.append(sig[:, b:b + 1] * o_cmp[sl] + sig[:, b + 1:b + 2] * o_slc[sl] + sig[:, b + 2:b + 3] * o_win[sl])
    return jnp.concatenate(outs, axis=1)


def _nsa_cmp_body(kv_ref, w1_ref, pe_ref, w2_ref, ckv_ref, pq_ref):
    n_ch = kv_ref.shape[0] // CMP_STRIDE
    ckv_ref[...] = _compress(lambda l: kv_ref[pl.ds(l, n_ch, stride=CMP_STRIDE), :], n_ch, w1_ref, pe_ref, w2_ref,
                             pq_ref)


def _nsa_cmp(kvc, w1, pe, w2, nb, t):
    n_ch = t // CMP_STRIDE
    return pl.pallas_call(
        _nsa_cmp_body,
        grid=(nb,),
        in_specs=[
            pl.BlockSpec((t, 128), lambda b: (b, 0)),
            pl.BlockSpec(w1.shape, lambda b: (0, 0, 0)),
            pl.BlockSpec(pe.shape, lambda b: (0, 0, 0)),
            pl.BlockSpec(w2.shape, lambda b: (0, 0)),
        ],
        out_specs=pl.BlockSpec((n_ch, 128), lambda b: (b, 0)),
        out_shape=jax.ShapeDtypeStruct((nb * n_ch, 128), F32),
        scratch_shapes=[pltpu.VMEM((n_ch + 8, 4 * CMP_HID), F32)],
        compiler_params=_cparams(("arbitrary",)),
        name="nsa_cmp",
    )(kvc, w1, pe, w2)


def _flash_step_t(carry, st, bias, vt):
    m, l, acc = carry
    sm = st + bias
    m_new = jnp.maximum(m, jnp.max(sm, axis=0, keepdims=True))
    p = jnp.exp2(sm - m_new)
    alpha = jnp.exp2(m - m_new)
    l = alpha * l + jnp.sum(p, axis=0, keepdims=True)
    acc = alpha * acc + _dot(vt, p.astype(BF16))
    return m_new, l, acc


def _flash_init_t(cols):
    return jnp.full((1, cols), NEG, F32), jnp.zeros((1, cols), F32), jnp.zeros((NSA_HD, cols), F32)


def _nsa_attn_body(t, q_ref, dtg_ref, ckv_ref, kvs_ref, kvw_ref, y_ref, ks_ref, kst_ref, kw_ref, kwt_ref, selt_ref):
    s = pl.program_id(1)
    qb = CHUNK
    cols = NSA_H * qb
    n_kc = t // qb
    n_ch = t // CMP_STRIDE
    n_cmp = n_ch - 1
    n_slc = t // SLC_BLOCK
    nb_pad = -(-n_slc // 8) * 8
    n_sel = min(N_SELECT, n_slc)

    kb = ks_ref.shape[1] // qb
    n_wc = WINDOW // qb + 1

    @pl.when(s == 0)
    def _():
        for c in range(n_kc):
            tile = kvs_ref[c * qb:(c + 1) * qb, :]
            ks_ref[c // kb, (c % kb) * qb:(c % kb + 1) * qb, :] = tile.astype(BF16)
            kst_ref[c // kb, :, (c % kb) * qb:(c % kb + 1) * qb] = tile.T.astype(BF16)
            tile = kvw_ref[c * qb:(c + 1) * qb, :]
            kw_ref[c] = tile.astype(BF16)
            kwt_ref[c] = tile.T.astype(BF16)

    q_t = (q_ref[...] * (NSA_HD ** -0.5 * LOG2E)).T
    qs_t = jnp.concatenate([q_t[h * NSA_HD:(h + 1) * NSA_HD, :] for h in range(NSA_H)], axis=1).astype(BF16)
    lane = lax.broadcasted_iota(jnp.int32, (1, cols), 1)
    tpos = s * qb + lane % qb
    tpos_h = tpos[:, 0:qb]

    ckv = ckv_ref[...]
    ck = ckv[:, 0:NSA_HD].astype(BF16)
    cv_t = ckv.T[NSA_HD:2 * NSA_HD, :].astype(BF16)
    sc = _dot(ck, qs_t)
    nrow = lax.broadcasted_iota(jnp.int32, (n_ch, cols), 0)
    cmask = (nrow * CMP_STRIDE + (CMP_LEN - 1) <= tpos) & (nrow < n_cmp)
    mx = jnp.max(jnp.where(cmask, sc, NEG), axis=0, keepdims=True)
    e = jnp.where(cmask, jnp.exp2(sc - mx), 0.0)
    pc = e / jnp.maximum(jnp.sum(e, axis=0, keepdims=True), 1e-30)
    o_cmp = _dot(cv_t, pc.astype(BF16))
    pcsum = pc[:, 0:qb] + pc[:, qb:2 * qb] + pc[:, 2 * qb:3 * qb] + pc[:, 3 * qb:4 * qb]

    jrow = lax.broadcasted_iota(jnp.int32, (nb_pad, n_ch), 0)
    ncol = lax.broadcasted_iota(jnp.int32, (nb_pad, n_ch), 1)
    ov_t = ((ncol * CMP_STRIDE < (jrow + 1) * SLC_BLOCK) & (ncol * CMP_STRIDE + CMP_LEN > jrow * SLC_BLOCK)
            & (ncol < n_cmp)).astype(F32)
    imp = jnp.dot(ov_t, pcsum, precision=HIGHEST, preferred_element_type=F32)
    brow = lax.broadcasted_iota(jnp.int32, (nb_pad, qb), 0)
    tq = s * qb + lax.broadcasted_iota(jnp.int32, (nb_pad, qb), 1)
    cur = tq // SLC_BLOCK
    forced = (brow == 0) | (brow == cur) | (brow == cur - 1)
    imp = jnp.where(forced, jnp.inf, imp)
    imp = jnp.where((brow * SLC_BLOCK <= tq) & (brow < n_slc), imp, -jnp.inf)
    rank = jnp.zeros((nb_pad, qb), F32)
    for i in range(n_slc):
        ri = imp[i:i + 1, :]
        before = (ri > imp) | ((ri == imp) & (brow > i))
        rank = rank + before.astype(F32)
    selt_ref[...] = jnp.where((rank < n_sel) & (imp > -jnp.inf), 0.0, NEG)

    bps = kb * qb // SLC_BLOCK
    krow_s = lax.broadcasted_iota(jnp.int32, (kb * qb, qb), 0)

    def slc_step(c, carry):
        k = ks_ref[c, :, 0:NSA_HD]
        vt = kst_ref[c, NSA_HD:2 * NSA_HD, :]
        sel_rows = selt_ref[pl.ds(pl.multiple_of(c * bps, bps), bps), :]
        bias = jnp.concatenate([jnp.broadcast_to(sel_rows[j:j + 1, :], (SLC_BLOCK, qb)) for j in range(bps)],
                               axis=0)
        bias = jnp.where(c * (kb * qb) + krow_s <= tpos_h, bias, NEG)
        return _flash_step_t(carry, _dot(k, qs_t), jnp.concatenate([bias] * NSA_H, axis=1), vt)

    _, l_s, acc_s = lax.fori_loop(0, (s + kb) // kb, slc_step, _flash_init_t(cols))
    o_slc = acc_s / l_s

    k0 = jnp.maximum(s - (n_wc - 1), 0)
    kw = jnp.concatenate([kw_ref[k0 + i, :, 0:NSA_HD] for i in range(n_wc)], axis=0)
    vwt = jnp.concatenate([kwt_ref[k0 + i, NSA_HD:2 * NSA_HD, :] for i in range(n_wc)], axis=1)
    kpos = k0 * qb + lax.broadcasted_iota(jnp.int32, (n_wc * qb, qb), 0)
    wbias = jnp.where((kpos <= tpos_h) & (kpos > tpos_h - WINDOW), 0.0, NEG)
    sw = _dot(kw, qs_t) + jnp.concatenate([wbias] * NSA_H, axis=1)
    pw = jnp.exp2(sw - jnp.max(sw, axis=0, keepdims=True))
    o_win = _dot(vwt, pw.astype(BF16)) / jnp.sum(pw, axis=0, keepdims=True)

    sig = 1.0 / (1.0 + jnp.exp(-dtg_ref[...].T[0:32, :]))
    outs = []
    for h in range(NSA_H):
        b = SSD_H + 3 * h
        sl = slice(h * qb, (h + 1) * qb)
        outs.append(sig[b:b + 1, :] * o_cmp[:, sl] + sig[b + 1:b + 2, :] * o_slc[:, sl]
                    + sig[b + 2:b + 3, :] * o_win[:, sl])
    y_ref[...] = jnp.concatenate(outs, axis=0).T


def _nsa_attn(q, dtg, ckv, kvc, kvw, nb, t):
    nq = t // CHUNK
    n_ch = t // CMP_STRIDE
    kb = 4
    assert n_ch % 128 == 0 and nq % kb == 0 and nq > WINDOW // CHUNK
    nb_pad = -(-(t // SLC_BLOCK) // 8) * 8
    kv_scratch = pltpu.VMEM((nq, CHUNK, 2 * NSA_HD), BF16)
    return pl.pallas_call(
        functools.partial(_nsa_attn_body, t),
        grid=(nb, nq),
        in_specs=[
            pl.BlockSpec((CHUNK, NSA_W), lambda b, s: (b * nq + s, 0)),
            pl.BlockSpec((CHUNK, 128), lambda b, s: (b * nq + s, 0)),
            pl.BlockSpec((n_ch, 128), lambda b, s: (b, 0)),
            pl.BlockSpec((t, 128), lambda b, s: (b, 1)),
            pl.BlockSpec((t, 128), lambda b, s: (b, 0)),
        ],
        out_specs=pl.BlockSpec((CHUNK, NSA_W), lambda b, s: (b * nq + s, 0)),
        out_shape=jax.ShapeDtypeStruct((nb * t, NSA_W), F32),
        scratch_shapes=[pltpu.VMEM((nq // kb, kb * CHUNK, 2 * NSA_HD), BF16),
                        pltpu.VMEM((nq // kb, 2 * NSA_HD, kb * CHUNK), BF16),
                        kv_scratch, kv_scratch, pltpu.VMEM((nb_pad, CHUNK), F32)],
        compiler_params=_cparams(("arbitrary", "arbitrary")),
        name="nsa_attn",
    )(q, dtg, ckv, kvc, kvw)


def _softmax_pv_nt(sc, mask, vt):
    mx = jnp.max(jnp.where(mask, sc, NEG), axis=1, keepdims=True)
    e = jnp.where(mask, jnp.exp(sc - mx), 0.0)
    l = jnp.sum(e, axis=1, keepdims=True)
    return _dot_nt(e.astype(BF16), vt) / jnp.maximum(l, 1e-30)


def _rows_to_cols(x):
    r, w = x.shape
    return jnp.concatenate([x, jnp.zeros((128 - r, w), F32)], axis=0).T


def _nsa_dec_body(layer, n_phys, n_pages, nb, t, pt_ref, cache_ref, q_ref, dtg_ref, kvn_ref, kwn_ref, win_ref,
                  w1_ref, pe_ref, w2_ref, y_ref, xt_ref, xc_ref, pq_ref, wk_ref, sem):
    b = pl.program_id(0)
    past = n_pages * PAGE_SIZE
    lk = past + t
    n_ch = -(-lk // CMP_STRIDE)
    n_cmp = n_ch - 1
    xrows = xt_ref.shape[1] * PAGE_SIZE
    nch_pad = xrows // CMP_STRIDE
    n_slc = -(-lk // SLC_BLOCK)
    nblk = -(-(xrows // SLC_BLOCK) // 128) * 128
    n_sel = min(N_SELECT, n_slc)
    slot = b % 2

    def page_copy(bb, sl, p):
        phys = pt_ref[bb, p] + layer * n_phys
        return pltpu.make_async_copy(cache_ref.at[phys], xt_ref.at[sl, p], sem.at[sl])

    def start_all(bb, sl):
        def body(p, carry):
            page_copy(bb, sl, p).start()
            return carry
        lax.fori_loop(0, n_pages, body, 0, unroll=8)

    @pl.when(b == 0)
    def _():
        start_all(0, 0)

    @pl.when(b + 1 < nb)
    def _():
        start_all(b + 1, 1 - slot)

    def wait_body(p, carry):
        page_copy(b, slot, p).wait()
        return carry
    lax.fori_loop(0, n_pages, wait_body, 0, unroll=8)

    xt_ref[slot, n_pages] = _rows_to_cols(kvn_ref[...])

    eye = (lax.broadcasted_iota(jnp.int32, (128, 128), 0)
           == lax.broadcasted_iota(jnp.int32, (128, 128), 1)).astype(BF16)
    for p in range(n_pages + 1):
        tile = xt_ref[slot, p, 0:128, :].astype(BF16)
        xc_ref[p * PAGE_SIZE:(p + 1) * PAGE_SIZE, :] = _dot_nt(eye, tile)

    rows = NSA_H * t
    qs = _stack_heads(q_ref[...] * (NSA_HD ** -0.5)).astype(BF16)
    ti = lax.broadcasted_iota(jnp.int32, (rows, 1), 0) % t
    tpos = past + ti

    ckv = _compress(lambda l: xc_ref[pl.ds(l, nch_pad, stride=CMP_STRIDE), :], nch_pad, w1_ref, pe_ref,
                    w2_ref, pq_ref)
    nc_use = (n_cmp + 127) // 128 * 128
    ck = ckv[0:nc_use, 0:NSA_HD].astype(BF16)
    cv = ckv[0:nc_use, NSA_HD:2 * NSA_HD].astype(BF16)
    sc = _dot_nt(qs, ck)
    ncol = lax.broadcasted_iota(jnp.int32, (rows, nc_use), 1)
    cmask = (ncol * CMP_STRIDE + (CMP_LEN - 1) <= tpos) & (ncol < n_cmp)
    mx = jnp.max(jnp.where(cmask, sc, NEG), axis=1, keepdims=True)
    e = jnp.where(cmask, jnp.exp(sc - mx), 0.0)
    pc = e / jnp.maximum(jnp.sum(e, axis=1, keepdims=True), 1e-30)
    o_cmp = _dot(pc.astype(BF16), cv)
    pcsum = pc[0:t] + pc[t:2 * t] + pc[2 * t:3 * t] + pc[3 * t:4 * t]

    nrow = lax.broadcasted_iota(jnp.int32, (nc_use, nblk), 0)
    jcol = lax.broadcasted_iota(jnp.int32, (nc_use, nblk), 1)
    ov = ((nrow * CMP_STRIDE < (jcol + 1) * SLC_BLOCK) & (nrow * CMP_STRIDE + CMP_LEN > jcol * SLC_BLOCK)
          & (nrow < n_cmp)).astype(F32)
    imp = jnp.dot(pcsum, ov, precision=HIGHEST, preferred_element_type=F32)
    jl = lax.broadcasted_iota(jnp.int32, (t, nblk), 1)
    tq = past + lax.broadcasted_iota(jnp.int32, (t, nblk), 0)
    cur = tq // SLC_BLOCK
    forced = (jl == 0) | (jl == cur) | (jl == cur - 1)
    imp = jnp.where(forced, jnp.inf, imp)
    imp = jnp.where((jl * SLC_BLOCK <= tq) & (jl < n_slc), imp, -jnp.inf)
    imp_t = jnp.concatenate([imp, jnp.full((128 - t, nblk), -jnp.inf, F32)], axis=0).T
    nb_rows = -(-n_slc // 8) * 8
    irow = lax.broadcasted_iota(jnp.int32, (nb_rows, nblk), 0)
    jlane = lax.broadcasted_iota(jnp.int32, (nb_rows, nblk), 1)
    ranks = []
    for qi in range(t):
        colv = imp_t[0:nb_rows, qi:qi + 1]
        rowv = imp[qi:qi + 1, :]
        before = (colv > rowv) | ((colv == rowv) & (irow < jlane))
        ranks.append(jnp.sum(before.astype(F32), axis=0, keepdims=True))
    rank = jnp.concatenate(ranks, axis=0)
    sel = ((rank < n_sel) & (imp > -jnp.inf)).astype(F32)
    sel4 = jnp.concatenate([sel] * NSA_H, axis=0)

    bpp = PAGE_SIZE // SLC_BLOCK
    plane = lax.broadcasted_iota(jnp.int32, (rows, PAGE_SIZE), 1)
    parts = []
    for p in range(n_pages + 1):
        scp = _dot(qs, xt_ref[slot, p, 2 * NSA_HD:3 * NSA_HD, :].astype(BF16))
        mp = jnp.where(plane < SLC_BLOCK, sel4[:, bpp * p:bpp * p + 1], sel4[:, bpp * p + 1:bpp * p + 2]) > 0.5
        if p == n_pages:
            mp = mp & (past + plane <= tpos)
        parts.append(jnp.where(mp, scp, NEG))
    sc = jnp.concatenate(parts, axis=1)
    e = jnp.exp(sc - jnp.max(sc, axis=1, keepdims=True))
    acc = jnp.zeros((rows, NSA_HD), F32)
    for p in range(n_pages + 1):
        acc = acc + _dot_nt(e[:, p * PAGE_SIZE:(p + 1) * PAGE_SIZE].astype(BF16),
                            xt_ref[slot, p, 3 * NSA_HD:4 * NSA_HD, :].astype(BF16))
    o_slc = acc / jnp.sum(e, axis=1, keepdims=True)

    wrows = wk_ref.shape[1]
    wk_ref[:, 0:WINDOW] = win_ref[0]
    wk_ref[:, WINDOW:wrows] = _rows_to_cols(kwn_ref[...])
    kwt = wk_ref[0:NSA_HD, :].astype(BF16)
    vwt = wk_ref[NSA_HD:2 * NSA_HD, :].astype(BF16)
    wpos = past - WINDOW + lax.broadcasted_iota(jnp.int32, (rows, wrows), 1)
    wmask = (wpos >= 0) & (wpos <= tpos) & (wpos > tpos - WINDOW)
    o_win = _softmax_pv_nt(_dot(qs, kwt), wmask, vwt)

    y_ref[...] = _gated_mix(dtg_ref[...], o_cmp, o_slc, o_win, t)


def _nsa_dec(page_table, cache_t, q, dtg, kvc, kvw, win_t, w1, pe, w2, layer, n_phys, nb, t):
    n_pages = page_table.shape[1]
    xrows = (n_pages + 1) * PAGE_SIZE
    assert t <= PAGE_SIZE and n_pages % 8 == 0
    grid_spec = pltpu.PrefetchScalarGridSpec(
        num_scalar_prefetch=1,
        grid=(nb,),
        in_specs=[
            pl.BlockSpec(memory_space=pl.ANY),
            pl.BlockSpec((t, NSA_W), lambda b, pt: (b, 0)),
            pl.BlockSpec((t, 128), lambda b, pt: (b, 0)),
            pl.BlockSpec((t, 4 * NSA_HD), lambda b, pt: (b, 0)),
            pl.BlockSpec((t, 2 * NSA_HD), lambda b, pt: (b, 0)),
            pl.BlockSpec((1, 2 * NSA_HD, WINDOW), lambda b, pt: (layer * nb + b, 0, 0)),
            pl.BlockSpec(w1.shape, lambda b, pt: (0, 0, 0)),
            pl.BlockSpec(pe.shape, lambda b, pt: (0, 0, 0)),
            pl.BlockSpec(w2.shape, lambda b, pt: (0, 0)),
        ],
        out_specs=pl.BlockSpec((t, NSA_W), lambda b, pt: (b, 0)),
        scratch_shapes=[
            pltpu.VMEM((2, n_pages + 1, 4 * NSA_HD, PAGE_SIZE), F32),
            pltpu.VMEM((xrows, 128), F32),
            pltpu.VMEM((xrows // CMP_STRIDE + 8, 4 * CMP_HID), F32),
            pltpu.VMEM((2 * NSA_HD, WINDOW + 128), F32),
            pltpu.SemaphoreType.DMA((2,)),
        ],
    )
    return pl.pallas_call(
        functools.partial(_nsa_dec_body, layer, n_phys, n_pages, nb, t),
        grid_spec=grid_spec,
        out_shape=jax.ShapeDtypeStruct((nb * t, NSA_W), F32),
        compiler_params=_cparams(("arbitrary",)),
        name="nsa_dec",
    )(page_table, cache_t, q, dtg, kvc, kvw, win_t, w1, pe, w2)


def _out_proj_body(ys_ref, yp_ref, yn_ref, x_ref, w_ref, npost_ref, nffn_ref, x1_ref, h2_ref):
    tm = x_ref.shape[0]
    n_split = 2 if tm % 32 == 0 else 1
    for i in range(n_split):
        r = slice(i * tm // n_split, (i + 1) * tm // n_split)
        mix = (_dot(ys_ref[r, :].astype(BF16), w_ref[0, 0:SSD_W, :])
               + _dot(yp_ref[r, :].astype(BF16), w_ref[0, SSD_W:SSD_W + POOL_W, :])
               + _dot(yn_ref[r, :].astype(BF16), w_ref[0, SSD_W + POOL_W:D_MODEL, :]))
        x1 = x_ref[r, :] + _rmsnorm(mix, npost_ref[...])
        x1_ref[r, :] = x1
        h2_ref[r, :] = _rmsnorm(x1, nffn_ref[...]).astype(BF16)


def _out_proj(ys, yp, yn, x2d, w_out, layer, npost, nffn, tm):
    m = x2d.shape[0]

    def tile(w):
        return pl.BlockSpec((tm, w), lambda i: (i, 0))

    vec = pl.BlockSpec((1, D_MODEL), lambda i: (0, 0))
    return pl.pallas_call(
        _out_proj_body,
        grid=(m // tm,),
        in_specs=[tile(SSD_W), tile(POOL_W), tile(NSA_W), tile(D_MODEL),
                  pl.BlockSpec((1, D_MODEL, D_MODEL), lambda i: (layer, 0, 0)), vec, vec],
        out_specs=[tile(D_MODEL), tile(D_MODEL)],
        out_shape=[jax.ShapeDtypeStruct((m, D_MODEL), F32), jax.ShapeDtypeStruct((m, D_MODEL), BF16)],
        compiler_params=_cparams(("arbitrary",)),
        name="out_proj",
    )(ys, yp, yn, x2d, w_out, npost, nffn)


def _ffn_body(carry_mode, tiles_per_seq, t, tf, *refs):
    if carry_mode:
        (h2_ref, x1_ref, wg_ref, wv_ref, wd_ref, cw_ref, cb_ref, npost_ref,
         x2_ref, gsave_ref, act_ref, gext_ref, carry_ref) = refs
    else:
        (h2_ref, x1_ref, wg_ref, wv_ref, wd_ref, cw_ref, cb_ref, npost_ref, prev_ref,
         x2_ref, gsave_ref, act_ref) = refs
    tm = h2_ref.shape[0]
    h2 = h2_ref[...]
    if carry_mode:
        @pl.when(pl.program_id(0) % tiles_per_seq == 0)
        def _():
            carry_ref[...] = jnp.zeros(carry_ref.shape, F32)
    else:
        r = lax.broadcasted_iota(jnp.int32, (tm, tf), 0) % t
    for f in range(D_FF // tf):
        fs = slice(f * tf, (f + 1) * tf)
        g = _dot(h2, wg_ref[0, :, fs])
        v = _dot(h2, wv_ref[0, :, fs])
        if carry_mode:
            buf = f % 2
            gext_ref[buf, 0:8, :] = carry_ref[:, fs]
            gext_ref[buf, 8:8 + tm, :] = g
            carry_ref[:, fs] = g[tm - 8:tm]
            gsave_ref[0, :, fs] = g[tm - 8:tm]
            g1 = gext_ref[buf, 7:7 + tm, :]
            g2 = gext_ref[buf, 6:6 + tm, :]
        else:
            prev = prev_ref[:, fs]
            g1 = jnp.where(r >= 1, pltpu.roll(g, 1, 0), pltpu.roll(prev, tm - 1, 0))
            g2 = jnp.where(r >= 2, pltpu.roll(g, 2, 0), prev)
            gsave_ref[:, fs] = g
        cw = cw_ref[:, fs]
        gc = g * cw[2:3, :] + g1 * cw[1:2, :] + g2 * cw[0:1, :] + cb_ref[:, fs]
        act_ref[:, fs] = (_gelu_tanh(gc) * v).astype(BF16)
    out = _dot(act_ref[...], wd_ref[0])
    x2_ref[...] = x1_ref[...] + _rmsnorm(out, npost_ref[...])


def _ffn(h2, x1, wg, wv, wd, layer, cw, cb, npost, tm, tf, carry_mode, tiles_per_seq, t, prev_rows=None):
    m = h2.shape[0]
    nm = m // tm
    resident = pl.Buffered(1)
    in_specs = [
        pl.BlockSpec((tm, D_MODEL), lambda i: (i, 0)),
        pl.BlockSpec((tm, D_MODEL), lambda i: (i, 0)),
        pl.BlockSpec((1, D_MODEL, D_FF), lambda i: (layer, 0, 0), pipeline_mode=resident),
        pl.BlockSpec((1, D_MODEL, D_FF), lambda i: (layer, 0, 0), pipeline_mode=resident),
        pl.BlockSpec((1, D_FF, D_MODEL), lambda i: (layer, 0, 0), pipeline_mode=resident),
        pl.BlockSpec((FFN_CONV, D_FF), lambda i: (0, 0)),
        pl.BlockSpec((1, D_FF), lambda i: (0, 0)),
        pl.BlockSpec((1, D_MODEL), lambda i: (0, 0)),
    ]
    args = [h2, x1, wg, wv, wd, cw, cb, npost]
    scratch = [pltpu.VMEM((tm, D_FF), BF16)]
    if carry_mode:
        gsave_shape = jax.ShapeDtypeStruct((nm, 8, D_FF), F32)
        gsave_spec = pl.BlockSpec((1, 8, D_FF), lambda i: (i, 0, 0))
        scratch += [pltpu.VMEM((2, tm + 8, tf), F32), pltpu.VMEM((8, D_FF), F32)]
    else:
        gsave_shape = jax.ShapeDtypeStruct((m, D_FF), F32)
        gsave_spec = pl.BlockSpec((tm, D_FF), lambda i: (i, 0))
        in_specs.append(pl.BlockSpec((tm, D_FF), lambda i: (i, 0)))
        args.append(prev_rows)
    return pl.pallas_call(
        functools.partial(_ffn_body, carry_mode, tiles_per_seq, t, tf),
        grid=(nm,),
        in_specs=in_specs,
        out_specs=[pl.BlockSpec((tm, D_MODEL), lambda i: (i, 0)), gsave_spec],
        out_shape=[jax.ShapeDtypeStruct((m, D_MODEL), F32), gsave_shape],
        scratch_shapes=scratch,
        compiler_params=_cparams(("arbitrary",)),
        name="ffn",
    )(*args)


def _prep_params(w_in, ssd_conv_w, ssd_conv_b, ssd_dt_bias, ssd_a_log, ssd_d, ssd_norm, pool_w, pool_scale,
                 nsa_pe_k, nsa_pe_v, nsa_w1_k, nsa_w1_v, nsa_w2_k, nsa_w2_v, w_out, ffn_w_gate, ffn_w_val,
                 ffn_w_down):
    depth = w_in.shape[0]
    w_in_p = jnp.concatenate(
        [w_in[..., 0:OFF_XBC], w_in[..., OFF_DT:OFF_KV], w_in[..., OFF_XBC:OFF_DT], w_in[..., OFF_KV:IN_W],
         jnp.zeros((depth, D_MODEL, IN_W_PAD - IN_W), F32)], axis=-1).astype(BF16)
    pad_h = ((0, 0), (0, 128 - SSD_H))
    dtb = jnp.pad(ssd_dt_bias, pad_h)[:, None, :]
    alog = jnp.pad(ssd_a_log, pad_h)[:, None, :]
    dsk = jnp.repeat(ssd_d, SSD_HD, axis=1)[:, None, :]
    n_g = len(POOL_WINDOWS)
    eye = jnp.eye(n_g, dtype=F32)
    pw_bd = (pool_w[:, :, :, None, :] * eye[None, :, None, :, None]).reshape(depth, POOL_W, POOL_W).astype(BF16)
    zk = jnp.zeros((depth, CMP_STRIDE, NSA_HD, CMP_HID), F32)
    top = jnp.concatenate([nsa_w1_k[:, :CMP_STRIDE], zk, nsa_w1_k[:, CMP_STRIDE:], zk], axis=-1)
    bot = jnp.concatenate([zk, nsa_w1_v[:, :CMP_STRIDE], zk, nsa_w1_v[:, CMP_STRIDE:]], axis=-1)
    w1 = jnp.concatenate([top, bot], axis=2).astype(BF16).reshape(depth, CMP_STRIDE // 2, 256, 4 * CMP_HID)
    pe_a = jnp.concatenate([nsa_pe_k[:, :CMP_STRIDE], nsa_pe_v[:, :CMP_STRIDE]], axis=-1)
    pe_b = jnp.concatenate([nsa_pe_k[:, CMP_STRIDE:], nsa_pe_v[:, CMP_STRIDE:]], axis=-1)
    pe = jnp.concatenate([pe_a[:, :, None], pe_b[:, :, None], jnp.zeros((depth, CMP_STRIDE, 6, 128), F32)], axis=2)
    pe = pe.reshape(depth, CMP_STRIDE // 2, 2, 8, 128).transpose(0, 1, 3, 2, 4).reshape(depth, CMP_STRIDE // 2, 8, 256)
    zw = jnp.zeros((depth, CMP_HID, NSA_HD), F32)
    w2 = jnp.concatenate([jnp.concatenate([nsa_w2_k, zw], axis=-1), jnp.concatenate([zw, nsa_w2_v], axis=-1)],
                         axis=1).astype(BF16)
    return dict(w_in=w_in_p, cw=ssd_conv_w, cb=ssd_conv_b[:, None, :], dtb=dtb, alog=alog, dsk=dsk,
                nrm=ssd_norm[:, None, :], pw=pw_bd, ps=pool_scale[:, None, :], w1=w1, pe=pe, w2=w2,
                w_out=w_out.astype(BF16), wg=ffn_w_gate.astype(BF16), wv=ffn_w_val.astype(BF16),
                wd=ffn_w_down.astype(BF16))


def _mixer_weights(p, l):
    return (p['cw'][l], p['cb'][l], p['dtb'][l], p['alog'][l], p['dsk'][l], p['nrm'][l], p['pw'][l], p['ps'][l])


def kernel(x_prompt, x_sample, cache_nsa_kv, page_table, state_nsa_win, state_ssd_conv, state_ssm, state_pool, state_ffn_conv, norm_mix_pre, w_in, ssd_conv_w, ssd_conv_b, ssd_dt_bias, ssd_a_log, ssd_d, ssd_norm, pool_w, pool_scale, nsa_pe_k, nsa_pe_v, nsa_w1_k, nsa_w1_v, nsa_w2_k, nsa_w2_v, w_out, norm_mix_post, norm_ffn_pre, ffn_w_gate, ffn_w_val, ffn_conv_w, ffn_conv_b, ffn_w_down, norm_ffn_post):
    depth = w_in.shape[0]
    bp, tp, _ = x_prompt.shape
    bs, ts, _ = x_sample.shape
    n_phys = cache_nsa_kv.shape[1]
    n_pages = page_table.shape[1]
    past = n_pages * PAGE_SIZE
    assert tp % 512 == 0 and tp >= WINDOW and ts == 8 and state_nsa_win.shape[2] == WINDOW
    p = _prep_params(w_in, ssd_conv_w, ssd_conv_b, ssd_dt_bias, ssd_a_log, ssd_d, ssd_norm, pool_w, pool_scale,
                     nsa_pe_k, nsa_pe_v, nsa_w1_k, nsa_w1_v, nsa_w2_k, nsa_w2_v, w_out, ffn_w_gate, ffn_w_val,
                     ffn_w_down)
    ms = bs * ts
    tab_p = _rope_table(jnp.arange(tp, dtype=jnp.int32))
    tab_s = jnp.tile(_rope_table(past + jnp.arange(ts, dtype=jnp.int32)), (bs, 1))
    cache = jnp.transpose(cache_nsa_kv, (0, 1, 3, 4, 2)).reshape(depth * n_phys, 4 * NSA_HD, PAGE_SIZE)
    win_state = jnp.transpose(state_nsa_win, (0, 1, 3, 4, 2)).reshape(depth * bs, 2 * NSA_HD, WINDOW)

    tm_p = 512
    tm_f = 512
    tf = 512
    nc_p = tp // CHUNK
    zeros_c = jnp.zeros((bp, 8, SSD_CONV_DIM), F32)
    zeros_s = jnp.zeros((bp, SSD_H, SSD_HD, SSD_N), F32)
    zeros_p = jnp.zeros((bp, 16, POOL_W), F32)

    xp = x_prompt.reshape(bp * tp, D_MODEL)
    xs = x_sample.reshape(ms, D_MODEL)
    outs_p, outs_s = [], []
    for l in range(depth):
        nw = norm_mix_pre[l][None]
        npost = norm_mix_post[l][None]
        nffn = norm_ffn_pre[l][None]
        nfpost = norm_ffn_post[l][None]
        mw = _mixer_weights(p, l)
        fcw, fcb = ffn_conv_w[l], ffn_conv_b[l][None]

        z, xbc, u, q, kvc, kvw, dtg = _in_proj(xp, nw, p['w_in'], l, tab_p, tm_p)
        y_ssd, y_pool, ssm_new = _ssd_pool(xbc, z, dtg, u, zeros_c, zeros_s, zeros_p, mw, bp, nc_p, CHUNK, 0, 2)
        ckv = _nsa_cmp(kvc, p['w1'][l], p['pe'][l], p['w2'][l], bp, tp)
        y_nsa = _nsa_attn(q, dtg, ckv, kvc, kvw, bp, tp)
        x1, h2 = _out_proj(y_ssd, y_pool, y_nsa, xp, p['w_out'], l, npost, nffn, tm_p)
        xp, gsave = _ffn(h2, x1, p['wg'], p['wv'], p['wd'], l, fcw, fcb, nfpost, tm_f, tf, True, tp // tm_f, tp)
        outs_p.append((
            kvc.reshape(bp, tp, 4, NSA_HD),
            kvw.reshape(bp, tp, 2, NSA_HD)[:, tp - WINDOW:],
            xbc.reshape(bp, tp, SSD_CONV_DIM)[:, tp - (SSD_CONV - 1):],
            ssm_new,
            u.reshape(bp, tp, POOL_W)[:, tp - POOL_KEEP:],
            gsave.reshape(bp, tp // tm_f, 8, D_FF)[:, -1, 8 - (FFN_CONV - 1):],
        ))

        z, xbc, u, q, kvc, kvw, dtg = _in_proj(xs, nw, p['w_in'], l, tab_s, ms)
        cprev = jnp.pad(state_ssd_conv[l], ((0, 0), (8 - (SSD_CONV - 1), 0), (0, 0)))
        pprev = jnp.pad(state_pool[l], ((0, 0), (16 - POOL_KEEP, 0), (0, 0)))
        y_ssd, y_pool, ssm_new = _ssd_pool(xbc, z, dtg, u, cprev, state_ssm[l], pprev, mw, bs, 1, ts, past, 4)
        y_nsa = _nsa_dec(page_table, cache, q, dtg, kvc, kvw, win_state, p['w1'][l], p['pe'][l], p['w2'][l],
                         l, n_phys, bs, ts)
        x1, h2 = _out_proj(y_ssd, y_pool, y_nsa, xs, p['w_out'], l, npost, nffn, ms)
        fprev = state_ffn_conv[l]
        fprev_rows = jnp.pad(fprev, ((0, 0), (0, ts - (FFN_CONV - 1)), (0, 0))).reshape(ms, D_FF)
        xs, gsave = _ffn(h2, x1, p['wg'], p['wv'], p['wd'], l, fcw, fcb, nfpost, ms, tf, False, 1, ts, fprev_rows)
        kvw3 = kvw.reshape(bs, ts, 2, NSA_HD)
        xbc3 = xbc.reshape(bs, ts, SSD_CONV_DIM)
        u3 = u.reshape(bs, ts, POOL_W)
        g3 = gsave.reshape(bs, ts, D_FF)
        outs_s.append((
            kvc.reshape(bs, ts, 4, NSA_HD),
            jnp.concatenate([state_nsa_win[l], kvw3], axis=1)[:, -WINDOW:],
            jnp.concatenate([state_ssd_conv[l], xbc3], axis=1)[:, -(SSD_CONV - 1):],
            ssm_new,
            jnp.concatenate([state_pool[l], u3], axis=1)[:, -POOL_KEEP:],
            jnp.concatenate([state_ffn_conv[l], g3], axis=1)[:, -(FFN_CONV - 1):],
        ))

    def stk(outs, i):
        return jnp.stack([o[i] for o in outs])

    return (xp.reshape(bp, tp, D_MODEL), xs.reshape(bs, ts, D_MODEL), stk(outs_p, 0), stk(outs_s, 0),
            stk(outs_p, 1), stk(outs_s, 1), stk(outs_p, 2), stk(outs_s, 2), stk(outs_p, 3), stk(outs_s, 3),
            stk(outs_p, 4), stk(outs_s, 4), stk(outs_p, 5), stk(outs_s, 5))
```

```python
import functools
import math

import jax
import jax.numpy as jnp
from jax import lax
from jax.experimental import pallas as pl
from jax.experimental.pallas import tpu as pltpu

F32 = jnp.float32
BF16 = jnp.bfloat16
HIGHEST = lax.Precision.HIGHEST

D_MODEL = 1024
PAGE_SIZE = 128
SSD_W = 512
SSD_HD = 64
SSD_H = 8
SSD_N = 128
SSD_G = 2
SSD_CONV = 4
SSD_CONV_DIM = 1024
POOL_W = 256
POOL_WINDOWS = (2, 4, 8, 16)
POOL_GC = 64
POOL_KEEP = 15
NSA_W = 256
NSA_HD = 64
NSA_H = 4
CMP_STRIDE = 16
CMP_LEN = 32
CMP_HID = 128
SLC_BLOCK = 64
N_SELECT = 16
WINDOW = 512
ROPE_DIM = 16
ROPE_THETA = 500000.0
D_FF = 4096
FFN_CONV = 3
RMS_EPS = 1e-6
OFF_Z = 512
OFF_XBC = 1536
OFF_DT = 1544
OFF_POOL = 1800
OFF_Q = 2056
OFF_KV = 2440
IN_W = 2452
IN_W_PAD = 2560
CHUNK = 128
NEG = -1e30
LOG2E = 1.4426950408889634
VMEM_LIMIT = 56 * 1024 * 1024


def _cparams(sem):
    return pltpu.CompilerParams(dimension_semantics=sem, vmem_limit_bytes=VMEM_LIMIT)


def _rmsnorm(x, w):
    return x * lax.rsqrt(jnp.mean(x * x, axis=-1, keepdims=True) + RMS_EPS) * w


def _silu(x):
    return x / (1.0 + jnp.exp(-x))


def _gelu_tanh(x):
    k = 0.7978845608028654
    hx = 0.5 * x
    return hx + hx * jnp.tanh(x * (k + (k * 0.044715) * (x * x)))


def _dot(a, b):
    return jnp.dot(a, b, preferred_element_type=F32)


def _dot_nt(a, b):
    return lax.dot_general(a, b, (((1,), (1,)), ((), ())), preferred_element_type=F32)


def _dot_tn(a, b):
    return lax.dot_general(a, b, (((0,), (0,)), ((), ())), preferred_element_type=F32)


def _in_proj_rows(x, nw, w_ref, tab_ref):
    r = x.shape[0]
    h = _rmsnorm(x, nw).astype(BF16)

    def proj(a, b):
        return _dot(h, w_ref[0, :, a:b])

    lane = lax.broadcasted_iota(jnp.int32, (r, 128), 1)
    first = (lane % NSA_HD) < (ROPE_DIM // 2)

    def rope(v, c, s):
        outs = []
        for j in range(v.shape[1] // 128):
            vj = v[:, j * 128:(j + 1) * 128]
            rot = jnp.where(first, pltpu.roll(vj, 128 - ROPE_DIM // 2, 1), pltpu.roll(vj, ROPE_DIM // 2, 1))
            outs.append(vj * c + rot * s)
        return outs

    q = rope(proj(1792, 2048), tab_ref[:, 0:128], tab_ref[:, 128:256])
    kv = rope(proj(2048, 2432), tab_ref[:, 256:384], tab_ref[:, 384:512])
    return (proj(0, 512), proj(512, 1536), proj(1536, 1792), jnp.concatenate(q, axis=1),
            jnp.concatenate(kv[0:2], axis=1), kv[2], proj(2432, 2560))


def _in_proj_body(x_ref, nw_ref, w_ref, tab_ref, z_ref, xbc_ref, pool_ref, q_ref, kvc_ref, kvw_ref, dtg_ref):
    outs = _in_proj_rows(x_ref[...], nw_ref[...], w_ref, tab_ref)
    for o_ref, o in zip((z_ref, xbc_ref, pool_ref, q_ref, kvc_ref, kvw_ref, dtg_ref), outs):
        o_ref[...] = o


def _in_proj(x2d, nw, w_in_p, layer, tab, tm):
    m = x2d.shape[0]
    tab_blocks = tab.shape[0] // tm
    widths = (512, 1024, 256, 256, 256, 128, 128)
    return pl.pallas_call(
        _in_proj_body,
        grid=(m // tm,),
        in_specs=[
            pl.BlockSpec((tm, D_MODEL), lambda i: (i, 0)),
            pl.BlockSpec((1, D_MODEL), lambda i: (0, 0)),
            pl.BlockSpec((1, D_MODEL, IN_W_PAD), lambda i: (layer, 0, 0)),
            pl.BlockSpec((tm, 512), lambda i: (i % tab_blocks, 0)),
        ],
        out_specs=[pl.BlockSpec((tm, w), lambda i: (i, 0)) for w in widths],
        out_shape=[jax.ShapeDtypeStruct((m, w), F32) for w in widths],
        compiler_params=_cparams(("arbitrary",)),
        name="in_proj",
    )(x2d, nw, w_in_p, tab)


def _rope_table(pos):
    half = ROPE_DIM // 2
    inv = 1.0 / (ROPE_THETA ** (jnp.arange(half, dtype=F32) / half))
    ang = pos.astype(F32)[:, None] * inv
    cos, sin = jnp.cos(ang), jnp.sin(ang)
    t = pos.shape[0]
    c_rot = jnp.concatenate([cos, cos, jnp.ones((t, NSA_HD - ROPE_DIM), F32)], axis=1)
    s_rot = jnp.concatenate([-sin, sin, jnp.zeros((t, NSA_HD - ROPE_DIM), F32)], axis=1)
    one, zero = jnp.ones((t, NSA_HD), F32), jnp.zeros((t, NSA_HD), F32)
    return jnp.concatenate([c_rot, c_rot, s_rot, s_rot, c_rot, one, s_rot, zero], axis=1)


def _ssd_pool_body(lr, nc, pos0, nsub, fused, *refs):
    if fused:
        (x_ref, nw_ref, win_ref, tab_ref, cprev_ref, sprev_ref, pprev_ref,
         cw_ref, cb_ref, dtb_ref, alog_ref, dsk_ref, nrm_ref, pw_ref, ps_ref,
         yssd_ref, ypool_ref, snew_ref, q_ref, kvc_ref, kvw_ref, dtg_ref, ctail_ref, ptail_ref,
         ext_ref, state_ref, pext_ref) = refs
    else:
        (xbc_ref, z_ref, dtg_ref, u_ref, cprev_ref, sprev_ref, pprev_ref,
         cw_ref, cb_ref, dtb_ref, alog_ref, dsk_ref, nrm_ref, pw_ref, ps_ref,
         yssd_ref, ypool_ref, snew_ref, ext_ref, state_ref, pext_ref) = refs
    c = pl.program_id(1)
    L = CHUNK

    @pl.when(c == 0)
    def _():
        ext_ref[:, 0:8, :] = cprev_ref[...]
        pext_ref[:, 0:16, :] = pprev_ref[...]
        state_ref[...] = sprev_ref[...]

    if nc > 1:
        @pl.when(c > 0)
        def _():
            ext_ref[:, 0:8, :] = ext_ref[:, L:L + 8, :]
            pext_ref[:, 0:16, :] = pext_ref[:, L:L + 16, :]

    z_rows, dt_rows = [], []
    for sub in range(nsub):
        if fused:
            z, xbc, u, q, kvc, kvw, dtg = _in_proj_rows(x_ref[sub], nw_ref[...], win_ref, tab_ref)
            ext_ref[sub, 8:8 + lr, :] = xbc
            pext_ref[sub, 16:16 + lr, :] = u
            q_ref[sub] = q
            kvc_ref[sub] = kvc
            kvw_ref[sub] = kvw
            dtg_ref[sub] = dtg
        else:
            z, dtg = z_ref[sub], dtg_ref[sub]
            ext_ref[sub, 8:8 + lr, :] = xbc_ref[sub]
            pext_ref[sub, 16:16 + lr, :] = u_ref[sub]
        z_rows.append(z)
        dt_rows.append(dtg)
    if lr < L:
        ext_ref[:, 8 + lr:8 + L, :] = jnp.zeros((nsub, L - lr, SSD_CONV_DIM), F32)
        pext_ref[:, 16 + lr:16 + L, :] = jnp.zeros((nsub, L - lr, POOL_W), F32)

    cw = cw_ref[...]
    row = lax.broadcasted_iota(jnp.int32, (L, L), 0)
    col = lax.broadcasted_iota(jnp.int32, (L, L), 1)
    tri = row >= col
    tri_f = tri.astype(F32)
    eye_f = (row == col).astype(F32)
    eye_b = eye_f.astype(BF16)
    gw = (SSD_H // SSD_G) * SSD_HD
    expand = (lax.broadcasted_iota(jnp.int32, (128, SSD_W), 0)
              == lax.broadcasted_iota(jnp.int32, (128, SSD_W), 1) // SSD_HD).astype(F32)
    neg_a = -jnp.exp(alog_ref[...])
    dsk = dsk_ref[...]
    prow = lax.broadcasted_iota(jnp.int32, (L, POOL_W), 0)
    plane = lax.broadcasted_iota(jnp.int32, (L, POOL_W), 1)
    pos = pos0 + c * L + prow + 1

    for sub in range(nsub):
        conv = (ext_ref[sub, 8:8 + L, :] * cw[3:4, :] + ext_ref[sub, 7:7 + L, :] * cw[2:3, :]
                + ext_ref[sub, 6:6 + L, :] * cw[1:2, :] + ext_ref[sub, 5:5 + L, :] * cw[0:1, :] + cb_ref[...])
        xbc = _silu(conv)
        xs = xbc[:, 0:SSD_W]
        bmat = [xbc[:, SSD_W + g * SSD_N:SSD_W + (g + 1) * SSD_N].astype(BF16) for g in range(SSD_G)]
        cmat = [xbc[:, SSD_W + (SSD_G + g) * SSD_N:SSD_W + (SSD_G + g + 1) * SSD_N].astype(BF16)
                for g in range(SSD_G)]

        if lr == L:
            dtraw = dt_rows[sub]
        else:
            dtraw = jnp.concatenate([dt_rows[sub], jnp.zeros((L - lr, 128), F32)], axis=0)
        xdt = dtraw + dtb_ref[...]
        dt = jnp.maximum(xdt, 0.0) + jnp.log1p(jnp.exp(-jnp.abs(xdt)))
        dt = jnp.where((col < SSD_H) & (row < lr), dt, 0.0)
        acum = jnp.dot(tri_f, dt * neg_a, precision=HIGHEST, preferred_element_type=F32)
        arow = lax.dot_general(eye_f, acum, (((1,), (1,)), ((), ())), precision=HIGHEST,
                               preferred_element_type=F32)
        dt_x = jnp.dot(dt, expand, precision=HIGHEST, preferred_element_type=F32)
        ac_x = jnp.dot(acum, expand, precision=HIGHEST, preferred_element_type=F32)
        alast_x = ac_x[L - 1:L, :]
        xdt = xs * dt_x
        xdt_b = xdt.astype(BF16)
        xw = (xdt * jnp.exp(alast_x - ac_x)).astype(BF16)
        cb_scores = [_dot_nt(cmat[g], bmat[g]) for g in range(SSD_G)]
        y_off = []
        for g in range(SSD_G):
            gs = slice(g * gw, (g + 1) * gw)
            st = state_ref[sub, g]
            y_off.append(_dot(cmat[g], st.astype(BF16)))
            b_t = _dot_nt(eye_b, bmat[g]).astype(BF16)
            state_ref[sub, g] = jnp.exp(alast_x[:, gs]) * st + _dot(b_t, xw[:, gs])
        y_diag = []
        for hp in range(SSD_H // 2):
            tile_b = xdt_b[:, hp * 128:(hp + 1) * 128]
            acc = None
            for h in (2 * hp, 2 * hp + 1):
                g = h // (SSD_H // SSD_G)
                decay = jnp.exp(jnp.where(tri, acum[:, h:h + 1] - arow[h:h + 1, :], NEG))
                own = (col < SSD_HD) if h % 2 == 0 else (col >= SSD_HD)
                part = _dot((cb_scores[g] * decay).astype(BF16), jnp.where(own, tile_b, jnp.zeros_like(tile_b)))
                acc = part if acc is None else acc + part
            y_diag.append(acc)
        y = jnp.concatenate(y_diag, axis=1) + jnp.concatenate(y_off, axis=1) * jnp.exp(ac_x) + dsk * xs
        if lr < L:
            y = y[0:lr]
        y = y * _silu(z_rows[sub])
        yssd_ref[sub] = _rmsnorm(y, nrm_ref[...]).astype(BF16)

        e1 = pext_ref[sub]
        s2 = e1 + pltpu.roll(e1, 1, 0)
        s4 = s2 + pltpu.roll(s2, 2, 0)
        s8 = s4 + pltpu.roll(s4, 4, 0)
        s16 = s8 + pltpu.roll(s8, 8, 0)
        pooled = jnp.zeros((L, POOL_W), F32)
        for gi, (w, sw) in enumerate(zip(POOL_WINDOWS, (s2, s4, s8, s16))):
            cnt = jnp.minimum(pos, w).astype(F32)
            grp = (plane >= gi * POOL_GC) & (plane < (gi + 1) * POOL_GC)
            pooled = jnp.where(grp, sw[16:16 + L] / cnt, pooled)
        pooled = pooled - e1[16:16 + L]
        yp = _dot(pooled.astype(BF16), pw_ref[...]) * ps_ref[...]
        ypool_ref[sub] = (yp[0:lr] if lr < L else yp).astype(BF16)

    @pl.when(c == nc - 1)
    def _():
        snew_ref[...] = state_ref[...]
        if fused:
            ctail_ref[...] = ext_ref[:, L:L + 8, :]
            ptail_ref[...] = pext_ref[:, L:L + 16, :]


def _ssd_pool(acts, cprev, sprev, pprev, wts, nb, nc, lr, pos0, nsub, fused_in=None):
    cw, cb, dtb, alog, dsk, nrm, pw, ps = wts
    t = nc * lr
    assert nb % nsub == 0

    def tile(w):
        return pl.BlockSpec((nsub, lr, w), lambda b, c: (b, c, 0))

    def full2(a):
        return pl.BlockSpec(a.shape, lambda b, c: (0, 0))

    def rows(w, dtype=F32):
        return jax.ShapeDtypeStruct((nb, t, w), dtype)

    hg = SSD_H // SSD_G
    st_shape = (SSD_G, SSD_N, hg * SSD_HD)
    sprev_t = sprev.reshape(nb, SSD_G, hg, SSD_HD, SSD_N).transpose(0, 1, 4, 2, 3).reshape((nb,) + st_shape)
    state_spec = pl.BlockSpec((nsub,) + st_shape, lambda b, c: (b, 0, 0, 0))
    prev_specs = [pl.BlockSpec((nsub, 8, SSD_CONV_DIM), lambda b, c: (b, 0, 0)), state_spec,
                  pl.BlockSpec((nsub, 16, POOL_W), lambda b, c: (b, 0, 0))]
    w_specs = [full2(cw), full2(cb), full2(dtb), full2(alog), full2(dsk), full2(nrm), full2(pw), full2(ps)]
    out_specs = [tile(SSD_W), tile(POOL_W), state_spec]
    out_shape = [rows(SSD_W, BF16), rows(POOL_W, BF16), jax.ShapeDtypeStruct((nb,) + st_shape, F32)]
    if fused_in is None:
        xbc, z, dtg, u = acts
        in_specs = [tile(SSD_CONV_DIM), tile(SSD_W), tile(128), tile(POOL_W)]
        args = [xbc.reshape(nb, t, SSD_CONV_DIM), z.reshape(nb, t, SSD_W), dtg.reshape(nb, t, 128),
                u.reshape(nb, t, POOL_W)]
    else:
        nw, w_in_p, layer, tab = fused_in
        assert lr == CHUNK
        in_specs = [tile(D_MODEL), pl.BlockSpec((1, D_MODEL), lambda b, c: (0, 0)),
                    pl.BlockSpec((1, D_MODEL, IN_W_PAD), lambda b, c: (layer, 0, 0), pipeline_mode=pl.Buffered(1)),
                    pl.BlockSpec((lr, 512), lambda b, c: (c, 0))]
        args = [acts.reshape(nb, t, D_MODEL), nw, w_in_p, tab]
        out_specs += [tile(NSA_W), tile(4 * NSA_HD), tile(2 * NSA_HD), tile(128),
                      pl.BlockSpec((nsub, 8, SSD_CONV_DIM), lambda b, c: (b, 0, 0)),
                      pl.BlockSpec((nsub, 16, POOL_W), lambda b, c: (b, 0, 0))]
        out_shape += [rows(NSA_W), rows(4 * NSA_HD), rows(2 * NSA_HD), rows(128),
                      jax.ShapeDtypeStruct((nb, 8, SSD_CONV_DIM), F32), jax.ShapeDtypeStruct((nb, 16, POOL_W), F32)]
    outs = pl.pallas_call(
        functools.partial(_ssd_pool_body, lr, nc, pos0, nsub, fused_in is not None),
        grid=(nb // nsub, nc),
        in_specs=in_specs + prev_specs + w_specs,
        out_specs=out_specs,
        out_shape=out_shape,
        scratch_shapes=[pltpu.VMEM((nsub, CHUNK + 8, SSD_CONV_DIM), F32),
                        pltpu.VMEM((nsub,) + st_shape, F32),
                        pltpu.VMEM((nsub, CHUNK + 16, POOL_W), F32)],
        compiler_params=_cparams(("arbitrary", "arbitrary")),
        name="ssd_pool",
    )(*args, cprev, sprev_t, pprev, cw, cb, dtb, alog, dsk, nrm, pw, ps)
    yssd, ypool, snew_t = outs[:3]
    snew = snew_t.reshape(nb, SSD_G, SSD_N, hg, SSD_HD).transpose(0, 1, 3, 4, 2).reshape(nb, SSD_H, SSD_HD, SSD_N)
    flat = [o.reshape(nb * t, o.shape[-1]) for o in outs[3:7]]
    return (yssd.reshape(nb * t, SSD_W), ypool.reshape(nb * t, POOL_W), snew) + tuple(flat) + tuple(outs[7:])


def _compress(load_rows, n_rows, w1_ref, pe_ref, w2_ref, pq_ref):
    acc = jnp.zeros((n_rows, 4 * CMP_HID), F32)
    cacc = jnp.zeros((8, 4 * CMP_HID), F32)
    for l2 in range(CMP_STRIDE // 2):
        w = w1_ref[l2]
        rows = jnp.concatenate([load_rows(2 * l2), load_rows(2 * l2 + 1)], axis=1)
        acc = acc + _dot(rows.astype(BF16), w)
        cacc = cacc + _dot(pe_ref[l2].astype(BF16), w)
    pq_ref[0:n_rows, :] = acc
    pq_ref[n_rows:n_rows + 8, :] = jnp.zeros((8, 4 * CMP_HID), F32)
    const = cacc[0:1, 0:2 * CMP_HID] + cacc[1:2, 2 * CMP_HID:4 * CMP_HID]
    hid = pq_ref[0:n_rows, 0:2 * CMP_HID] + pq_ref[1:n_rows + 1, 2 * CMP_HID:4 * CMP_HID] + const
    return _dot(_gelu_tanh(hid).astype(BF16), w2_ref[...])


def _stack_heads(x):
    return jnp.concatenate([x[:, h * NSA_HD:(h + 1) * NSA_HD] for h in range(NSA_H)], axis=0)


def _gated_mix(dtg, o_cmp, o_slc, o_win, r):
    sig = 1.0 / (1.0 + jnp.exp(-dtg))
    outs = []
    for h in range(NSA_H):
        b = SSD_H + 3 * h
        sl = slice(h * r, (h + 1) * r)
        outs.append(sig[:, b:b + 1] * o_cmp[sl] + sig[:, b + 1:b + 2] * o_slc[sl] + sig[:, b + 2:b + 3] * o_win[sl])
    return jnp.concatenate(outs, axis=1)


def _nsa_cmp_body(kv_ref, w1_ref, pe_ref, w2_ref, ckv_ref, pq_ref):
    n_ch = kv_ref.shape[0] // CMP_STRIDE
    ckv_ref[...] = _compress(lambda l: kv_ref[pl.ds(l, n_ch, stride=CMP_STRIDE), :], n_ch, w1_ref, pe_ref, w2_ref,
                             pq_ref)


def _nsa_cmp(kvc, w1, pe, w2, nb, t):
    n_ch = t // CMP_STRIDE
    return pl.pallas_call(
        _nsa_cmp_body,
        grid=(nb,),
        in_specs=[
            pl.BlockSpec((t, 128), lambda b: (b, 0)),
            pl.BlockSpec(w1.shape, lambda b: (0, 0, 0)),
            pl.BlockSpec(pe.shape, lambda b: (0, 0, 0)),
            pl.BlockSpec(w2.shape, lambda b: (0, 0)),
        ],
        out_specs=pl.BlockSpec((n_ch, 128), lambda b: (b, 0)),
        out_shape=jax.ShapeDtypeStruct((nb * n_ch, 128), F32),
        scratch_shapes=[pltpu.VMEM((n_ch + 8, 4 * CMP_HID), F32)],
        compiler_params=_cparams(("arbitrary",)),
        name="nsa_cmp",
    )(kvc, w1, pe, w2)


def _flash_step_t(carry, st, bias, vt):
    m, l, acc = carry
    sm = st + bias
    m_new = jnp.maximum(m, jnp.max(sm, axis=0, keepdims=True))
    p = jnp.exp2(sm - m_new)
    alpha = jnp.exp2(m - m_new)
    l = alpha * l + jnp.sum(p, axis=0, keepdims=True)
    acc = alpha * acc + _dot(vt, p.astype(BF16))
    return m_new, l, acc


def _flash_init_t(cols):
    return jnp.full((1, cols), NEG, F32), jnp.zeros((1, cols), F32), jnp.zeros((NSA_HD, cols), F32)


def _nsa_attn_body(t, q_ref, dtg_ref, ckv_ref, kvs_ref, kvw_ref, y_ref, ks_ref, kst_ref, kw_ref, kwt_ref, selt_ref):
    s = pl.program_id(1)
    qb = CHUNK
    cols = NSA_H * qb
    n_kc = t // qb
    n_ch = t // CMP_STRIDE
    n_cmp = n_ch - 1
    n_slc = t // SLC_BLOCK
    nb_pad = -(-n_slc // 8) * 8
    n_sel = min(N_SELECT, n_slc)

    kb = ks_ref.shape[1] // qb
    n_wc = WINDOW // qb + 1

    @pl.when(s == 0)
    def _():
        for c in range(n_kc):
            tile = kvs_ref[c * qb:(c + 1) * qb, :]
            ks_ref[c // kb, (c % kb) * qb:(c % kb + 1) * qb, :] = tile.astype(BF16)
            kst_ref[c // kb, :, (c % kb) * qb:(c % kb + 1) * qb] = tile.T.astype(BF16)
            tile = kvw_ref[c * qb:(c + 1) * qb, :]
            kw_ref[c] = tile.astype(BF16)
            kwt_ref[c] = tile.T.astype(BF16)

    q_t = (q_ref[...] * (NSA_HD ** -0.5 * LOG2E)).T
    qs_t = jnp.concatenate([q_t[h * NSA_HD:(h + 1) * NSA_HD, :] for h in range(NSA_H)], axis=1).astype(BF16)
    lane = lax.broadcasted_iota(jnp.int32, (1, cols), 1)
    tpos = s * qb + lane % qb
    tpos_h = tpos[:, 0:qb]

    ckv = ckv_ref[...]
    ck = ckv[:, 0:NSA_HD].astype(BF16)
    cv_t = ckv.T[NSA_HD:2 * NSA_HD, :].astype(BF16)
    sc = _dot(ck, qs_t)
    nrow = lax.broadcasted_iota(jnp.int32, (n_ch, cols), 0)
    cmask = (nrow * CMP_STRIDE + (CMP_LEN - 1) <= tpos) & (nrow < n_cmp)
    mx = jnp.max(jnp.where(cmask, sc, NEG), axis=0, keepdims=True)
    e = jnp.where(cmask, jnp.exp2(sc - mx), 0.0)
    pc = e / jnp.maximum(jnp.sum(e, axis=0, keepdims=True), 1e-30)
    o_cmp = _dot(cv_t, pc.astype(BF16))
    pcsum = pc[:, 0:qb] + pc[:, qb:2 * qb] + pc[:, 2 * qb:3 * qb] + pc[:, 3 * qb:4 * qb]

    jrow = lax.broadcasted_iota(jnp.int32, (nb_pad, n_ch), 0)
    ncol = lax.broadcasted_iota(jnp.int32, (nb_pad, n_ch), 1)
    ov_t = ((ncol * CMP_STRIDE < (jrow + 1) * SLC_BLOCK) & (ncol * CMP_STRIDE + CMP_LEN > jrow * SLC_BLOCK)
            & (ncol < n_cmp)).astype(F32)
    imp = jnp.dot(ov_t, pcsum, precision=HIGHEST, preferred_element_type=F32)
    brow = lax.broadcasted_iota(jnp.int32, (nb_pad, qb), 0)
    tq = s * qb + lax.broadcasted_iota(jnp.int32, (nb_pad, qb), 1)
    cur = tq // SLC_BLOCK
    forced = (brow == 0) | (brow == cur) | (brow == cur - 1)
    imp = jnp.where(forced, jnp.inf, imp)
    imp = jnp.where((brow * SLC_BLOCK <= tq) & (brow < n_slc), imp, -jnp.inf)
    rank = jnp.zeros((nb_pad, qb), F32)
    for i in range(n_slc):
        ri = imp[i:i + 1, :]
        before = (ri > imp) | ((ri == imp) & (brow > i))
        rank = rank + before.astype(F32)
    selt_ref[...] = jnp.where((rank < n_sel) & (imp > -jnp.inf), 0.0, NEG)

    bps = kb * qb // SLC_BLOCK
    krow_s = lax.broadcasted_iota(jnp.int32, (kb * qb, qb), 0)

    def slc_step(c, carry):
        k = ks_ref[c, :, 0:NSA_HD]
        vt = kst_ref[c, NSA_HD:2 * NSA_HD, :]
        sel_rows = selt_ref[pl.ds(pl.multiple_of(c * bps, bps), bps), :]
        bias = jnp.concatenate([jnp.broadcast_to(sel_rows[j:j + 1, :], (SLC_BLOCK, qb)) for j in range(bps)],
                               axis=0)
        bias = jnp.where(c * (kb * qb) + krow_s <= tpos_h, bias, NEG)
        return _flash_step_t(carry, _dot(k, qs_t), jnp.concatenate([bias] * NSA_H, axis=1), vt)

    _, l_s, acc_s = lax.fori_loop(0, (s + kb) // kb, slc_step, _flash_init_t(cols))
    o_slc = acc_s / l_s

    k0 = jnp.maximum(s - (n_wc - 1), 0)
    kw = jnp.concatenate([kw_ref[k0 + i, :, 0:NSA_HD] for i in range(n_wc)], axis=0)
    vwt = jnp.concatenate([kwt_ref[k0 + i, NSA_HD:2 * NSA_HD, :] for i in range(n_wc)], axis=1)
    kpos = k0 * qb + lax.broadcasted_iota(jnp.int32, (n_wc * qb, qb), 0)
    wbias = jnp.where((kpos <= tpos_h) & (kpos > tpos_h - WINDOW), 0.0, NEG)
    sw = _dot(kw, qs_t) + jnp.concatenate([wbias] * NSA_H, axis=1)
    pw = jnp.exp2(sw - jnp.max(sw, axis=0, keepdims=True))
    o_win = _dot(vwt, pw.astype(BF16)) / jnp.sum(pw, axis=0, keepdims=True)

    sig = 1.0 / (1.0 + jnp.exp(-dtg_ref[...].T[0:32, :]))
    outs = []
    for h in range(NSA_H):
        b = SSD_H + 3 * h
        sl = slice(h * qb, (h + 1) * qb)
        outs.append(sig[b:b + 1, :] * o_cmp[:, sl] + sig[b + 1:b + 2, :] * o_slc[:, sl]
                    + sig[b + 2:b + 3, :] * o_win[:, sl])
    y_ref[...] = jnp.concatenate(outs, axis=0).T.astype(BF16)


def _nsa_attn(q, dtg, ckv, kvc, kvw, nb, t):
    nq = t // CHUNK
    n_ch = t // CMP_STRIDE
    kb = 4
    assert n_ch % 128 == 0 and nq % kb == 0 and nq > WINDOW // CHUNK
    nb_pad = -(-(t // SLC_BLOCK) // 8) * 8
    kv_scratch = pltpu.VMEM((nq, CHUNK, 2 * NSA_HD), BF16)
    return pl.pallas_call(
        functools.partial(_nsa_attn_body, t),
        grid=(nb, nq),
        in_specs=[
            pl.BlockSpec((CHUNK, NSA_W), lambda b, s: (b * nq + s, 0)),
            pl.BlockSpec((CHUNK, 128), lambda b, s: (b * nq + s, 0)),
            pl.BlockSpec((n_ch, 128), lambda b, s: (b, 0)),
            pl.BlockSpec((t, 128), lambda b, s: (b, 1)),
            pl.BlockSpec((t, 128), lambda b, s: (b, 0)),
        ],
        out_specs=pl.BlockSpec((CHUNK, NSA_W), lambda b, s: (b * nq + s, 0)),
        out_shape=jax.ShapeDtypeStruct((nb * t, NSA_W), BF16),
        scratch_shapes=[pltpu.VMEM((nq // kb, kb * CHUNK, 2 * NSA_HD), BF16),
                        pltpu.VMEM((nq // kb, 2 * NSA_HD, kb * CHUNK), BF16),
                        kv_scratch, kv_scratch, pltpu.VMEM((nb_pad, CHUNK), F32)],
        compiler_params=_cparams(("arbitrary", "arbitrary")),
        name="nsa_attn",
    )(q, dtg, ckv, kvc, kvw)


def _softmax_pv_nt(sc, mask, vt):
    mx = jnp.max(jnp.where(mask, sc, NEG), axis=1, keepdims=True)
    e = jnp.where(mask, jnp.exp(sc - mx), 0.0)
    l = jnp.sum(e, axis=1, keepdims=True)
    return _dot_nt(e.astype(BF16), vt) / jnp.maximum(l, 1e-30)


def _rows_to_cols(x):
    r, w = x.shape
    return jnp.concatenate([x, jnp.zeros((128 - r, w), F32)], axis=0).T


def _nsa_dec_body(layer, n_phys, n_pages, nb, t, pt_ref, cache_ref, q_ref, dtg_ref, kvn_ref, kwn_ref, win_ref,
                  w1_ref, pe_ref, w2_ref, y_ref, xt_ref, xc_ref, pq_ref, wk_ref, sem):
    b = pl.program_id(0)
    past = n_pages * PAGE_SIZE
    lk = past + t
    n_ch = -(-lk // CMP_STRIDE)
    n_cmp = n_ch - 1
    xrows = xt_ref.shape[1] * PAGE_SIZE
    nch_pad = xrows // CMP_STRIDE
    n_slc = -(-lk // SLC_BLOCK)
    nblk = -(-(xrows // SLC_BLOCK) // 128) * 128
    n_sel = min(N_SELECT, n_slc)
    slot = b % 2

    def page_copy(bb, sl, p):
        phys = pt_ref[bb, p] + layer * n_phys
        return pltpu.make_async_copy(cache_ref.at[phys], xt_ref.at[sl, p], sem.at[sl])

    def start_all(bb, sl):
        def body(p, carry):
            page_copy(bb, sl, p).start()
            return carry
        lax.fori_loop(0, n_pages, body, 0, unroll=8)

    @pl.when(b == 0)
    def _():
        start_all(0, 0)

    @pl.when(b + 1 < nb)
    def _():
        start_all(b + 1, 1 - slot)

    def wait_body(p, carry):
        page_copy(b, slot, p).wait()
        return carry
    lax.fori_loop(0, n_pages, wait_body, 0, unroll=8)

    xt_ref[slot, n_pages] = _rows_to_cols(kvn_ref[...])

    for p in range(n_pages + 1):
        xc_ref[p * PAGE_SIZE:(p + 1) * PAGE_SIZE, :] = xt_ref[slot, p, 0:128, :].T

    rows = NSA_H * t
    qs = _stack_heads(q_ref[...] * (NSA_HD ** -0.5)).astype(BF16)
    ti = lax.broadcasted_iota(jnp.int32, (rows, 1), 0) % t
    tpos = past + ti

    ckv = _compress(lambda l: xc_ref[pl.ds(l, nch_pad, stride=CMP_STRIDE), :], nch_pad, w1_ref, pe_ref,
                    w2_ref, pq_ref)
    nc_use = (n_cmp + 127) // 128 * 128
    ck = ckv[0:nc_use, 0:NSA_HD].astype(BF16)
    cv = ckv[0:nc_use, NSA_HD:2 * NSA_HD].astype(BF16)
    sc = _dot_nt(qs, ck)
    ncol = lax.broadcasted_iota(jnp.int32, (rows, nc_use), 1)
    cmask = (ncol * CMP_STRIDE + (CMP_LEN - 1) <= tpos) & (ncol < n_cmp)
    mx = jnp.max(jnp.where(cmask, sc, NEG), axis=1, keepdims=True)
    e = jnp.where(cmask, jnp.exp(sc - mx), 0.0)
    pc = e / jnp.maximum(jnp.sum(e, axis=1, keepdims=True), 1e-30)
    o_cmp = _dot(pc.astype(BF16), cv)
    pcsum = pc[0:t] + pc[t:2 * t] + pc[2 * t:3 * t] + pc[3 * t:4 * t]

    nrow = lax.broadcasted_iota(jnp.int32, (nc_use, nblk), 0)
    jcol = lax.broadcasted_iota(jnp.int32, (nc_use, nblk), 1)
    ov = ((nrow * CMP_STRIDE < (jcol + 1) * SLC_BLOCK) & (nrow * CMP_STRIDE + CMP_LEN > jcol * SLC_BLOCK)
          & (nrow < n_cmp)).astype(F32)
    imp = jnp.dot(pcsum, ov, precision=HIGHEST, preferred_element_type=F32)
    jl = lax.broadcasted_iota(jnp.int32, (t, nblk), 1)
    tq = past + lax.broadcasted_iota(jnp.int32, (t, nblk), 0)
    cur = tq // SLC_BLOCK
    forced = (jl == 0) | (jl == cur) | (jl == cur - 1)
    imp = jnp.where(forced, jnp.inf, imp)
    imp = jnp.where((jl * SLC_BLOCK <= tq) & (jl < n_slc), imp, -jnp.inf)
    imp_t = jnp.concatenate([imp, jnp.full((128 - t, nblk), -jnp.inf, F32)], axis=0).T
    nb_rows = -(-n_slc // 8) * 8
    irow = lax.broadcasted_iota(jnp.int32, (nb_rows, nblk), 0)
    jlane = lax.broadcasted_iota(jnp.int32, (nb_rows, nblk), 1)
    ranks = []
    for qi in range(t):
        colv = imp_t[0:nb_rows, qi:qi + 1]
        rowv = imp[qi:qi + 1, :]
        before = (colv > rowv) | ((colv == rowv) & (irow < jlane))
        ranks.append(jnp.sum(before.astype(F32), axis=0, keepdims=True))
    rank = jnp.concatenate(ranks, axis=0)
    sel = ((rank < n_sel) & (imp > -jnp.inf)).astype(F32)
    sel4 = jnp.concatenate([sel] * NSA_H, axis=0)

    bpp = PAGE_SIZE // SLC_BLOCK
    plane = lax.broadcasted_iota(jnp.int32, (rows, PAGE_SIZE), 1)
    parts = []
    for p in range(n_pages + 1):
        scp = _dot(qs, xt_ref[slot, p, 2 * NSA_HD:3 * NSA_HD, :].astype(BF16))
        mp = jnp.where(plane < SLC_BLOCK, sel4[:, bpp * p:bpp * p + 1], sel4[:, bpp * p + 1:bpp * p + 2]) > 0.5
        if p == n_pages:
            mp = mp & (past + plane <= tpos)
        parts.append(jnp.where(mp, scp, NEG))
    sc = jnp.concatenate(parts, axis=1)
    e = jnp.exp(sc - jnp.max(sc, axis=1, keepdims=True))
    acc = jnp.zeros((rows, NSA_HD), F32)
    for p in range(n_pages + 1):
        acc = acc + _dot_nt(e[:, p * PAGE_SIZE:(p + 1) * PAGE_SIZE].astype(BF16),
                            xt_ref[slot, p, 3 * NSA_HD:4 * NSA_HD, :].astype(BF16))
    o_slc = acc / jnp.sum(e, axis=1, keepdims=True)

    wrows = wk_ref.shape[1]
    wk_ref[:, 0:WINDOW] = win_ref[0]
    wk_ref[:, WINDOW:wrows] = _rows_to_cols(kwn_ref[...])
    kwt = wk_ref[0:NSA_HD, :].astype(BF16)
    vwt = wk_ref[NSA_HD:2 * NSA_HD, :].astype(BF16)
    wpos = past - WINDOW + lax.broadcasted_iota(jnp.int32, (rows, wrows), 1)
    wmask = (wpos >= 0) & (wpos <= tpos) & (wpos > tpos - WINDOW)
    o_win = _softmax_pv_nt(_dot(qs, kwt), wmask, vwt)

    y_ref[0] = _gated_mix(dtg_ref[...], o_cmp, o_slc, o_win, t).astype(BF16)


def _nsa_dec(page_table, cache_t, q, dtg, kvc, kvw, win_t, w1, pe, w2, layer, n_phys, nb, t):
    n_pages = page_table.shape[1]
    xrows = (n_pages + 1) * PAGE_SIZE
    assert t <= PAGE_SIZE and n_pages % 8 == 0
    grid_spec = pltpu.PrefetchScalarGridSpec(
        num_scalar_prefetch=1,
        grid=(nb,),
        in_specs=[
            pl.BlockSpec(memory_space=pl.ANY),
            pl.BlockSpec((t, NSA_W), lambda b, pt: (b, 0)),
            pl.BlockSpec((t, 128), lambda b, pt: (b, 0)),
            pl.BlockSpec((t, 4 * NSA_HD), lambda b, pt: (b, 0)),
            pl.BlockSpec((t, 2 * NSA_HD), lambda b, pt: (b, 0)),
            pl.BlockSpec((1, 2 * NSA_HD, WINDOW), lambda b, pt: (layer * nb + b, 0, 0)),
            pl.BlockSpec(w1.shape, lambda b, pt: (0, 0, 0)),
            pl.BlockSpec(pe.shape, lambda b, pt: (0, 0, 0)),
            pl.BlockSpec(w2.shape, lambda b, pt: (0, 0)),
        ],
        out_specs=pl.BlockSpec((1, t, NSA_W), lambda b, pt: (b, 0, 0)),
        scratch_shapes=[
            pltpu.VMEM((2, n_pages + 1, 4 * NSA_HD, PAGE_SIZE), F32),
            pltpu.VMEM((xrows, 128), F32),
            pltpu.VMEM((xrows // CMP_STRIDE + 8, 4 * CMP_HID), F32),
            pltpu.VMEM((2 * NSA_HD, WINDOW + 128), F32),
            pltpu.SemaphoreType.DMA((2,)),
        ],
    )
    return pl.pallas_call(
        functools.partial(_nsa_dec_body, layer, n_phys, n_pages, nb, t),
        grid_spec=grid_spec,
        out_shape=jax.ShapeDtypeStruct((nb, t, NSA_W), BF16),
        compiler_params=_cparams(("arbitrary",)),
        name="nsa_dec",
    )(page_table, cache_t, q, dtg, kvc, kvw, win_t, w1, pe, w2).reshape(nb * t, NSA_W)


def _out_proj_body(ys_ref, yp_ref, yn_ref, x_ref, w_ref, npost_ref, nffn_ref, x1_ref, h2_ref):
    tm = x_ref.shape[0]
    n_split = 2 if tm % 32 == 0 else 1
    for i in range(n_split):
        r = slice(i * tm // n_split, (i + 1) * tm // n_split)
        mix = (_dot(ys_ref[r, :], w_ref[0, 0:SSD_W, :]) + _dot(yp_ref[r, :], w_ref[0, SSD_W:SSD_W + POOL_W, :])
               + _dot(yn_ref[r, :], w_ref[0, SSD_W + POOL_W:D_MODEL, :]))
        x1 = x_ref[r, :] + _rmsnorm(mix, npost_ref[...])
        x1_ref[r, :] = x1
        h2_ref[r, :] = _rmsnorm(x1, nffn_ref[...]).astype(BF16)


def _out_proj(ys, yp, yn, x2d, w_out, layer, npost, nffn, tm):
    m = x2d.shape[0]

    def tile(w):
        return pl.BlockSpec((tm, w), lambda i: (i, 0))

    vec = pl.BlockSpec((1, D_MODEL), lambda i: (0, 0))
    return pl.pallas_call(
        _out_proj_body,
        grid=(m // tm,),
        in_specs=[tile(SSD_W), tile(POOL_W), tile(NSA_W), tile(D_MODEL),
                  pl.BlockSpec((1, D_MODEL, D_MODEL), lambda i: (layer, 0, 0)), vec, vec],
        out_specs=[tile(D_MODEL), tile(D_MODEL)],
        out_shape=[jax.ShapeDtypeStruct((m, D_MODEL), F32), jax.ShapeDtypeStruct((m, D_MODEL), BF16)],
        compiler_params=_cparams(("arbitrary",)),
        name="out_proj",
    )(ys, yp, yn, x2d, w_out, npost, nffn)


def _ffn_body(carry_mode, tiles_per_seq, t, tf, *refs):
    if carry_mode:
        (h2_ref, x1_ref, wg_ref, wv_ref, wd_ref, cw_ref, cb_ref, npost_ref,
         x2_ref, gsave_ref, act_ref, gext_ref, carry_ref) = refs
    else:
        (h2_ref, x1_ref, wg_ref, wv_ref, wd_ref, cw_ref, cb_ref, npost_ref, prev_ref,
         x2_ref, gsave_ref, act_ref) = refs
    tm = h2_ref.shape[0]
    h2 = h2_ref[...]
    if carry_mode:
        @pl.when(pl.program_id(0) % tiles_per_seq == 0)
        def _():
            carry_ref[...] = jnp.zeros(carry_ref.shape, F32)
    else:
        r = lax.broadcasted_iota(jnp.int32, (tm, tf), 0) % t
    for f in range(D_FF // tf):
        fs = slice(f * tf, (f + 1) * tf)
        g = _dot(h2, wg_ref[0, :, fs])
        v = _dot(h2, wv_ref[0, :, fs])
        if carry_mode:
            buf = f % 2
            gext_ref[buf, 0:8, :] = carry_ref[:, fs]
            gext_ref[buf, 8:8 + tm, :] = g
            carry_ref[:, fs] = g[tm - 8:tm]
            gsave_ref[0, :, fs] = g[tm - 8:tm]
            g1 = gext_ref[buf, 7:7 + tm, :]
            g2 = gext_ref[buf, 6:6 + tm, :]
        else:
            prev = prev_ref[:, fs]
            g1 = jnp.where(r >= 1, pltpu.roll(g, 1, 0), pltpu.roll(prev, tm - 1, 0))
            g2 = jnp.where(r >= 2, pltpu.roll(g, 2, 0), prev)
            gsave_ref[:, fs] = g
        cw = cw_ref[:, fs]
        gc = g * cw[2:3, :] + g1 * cw[1:2, :] + g2 * cw[0:1, :] + cb_ref[:, fs]
        act_ref[:, fs] = (_gelu_tanh(gc) * v).astype(BF16)
    out = _dot(act_ref[...], wd_ref[0])
    x2_ref[...] = x1_ref[...] + _rmsnorm(out, npost_ref[...])


def _ffn(h2, x1, wg, wv, wd, layer, cw, cb, npost, tm, tf, carry_mode, tiles_per_seq, t, prev_rows=None):
    m = h2.shape[0]
    nm = m // tm
    resident = pl.Buffered(1)
    in_specs = [
        pl.BlockSpec((tm, D_MODEL), lambda i: (i, 0)),
        pl.BlockSpec((tm, D_MODEL), lambda i: (i, 0)),
        pl.BlockSpec((1, D_MODEL, D_FF), lambda i: (layer, 0, 0), pipeline_mode=resident),
        pl.BlockSpec((1, D_MODEL, D_FF), lambda i: (layer, 0, 0), pipeline_mode=resident),
        pl.BlockSpec((1, D_FF, D_MODEL), lambda i: (layer, 0, 0), pipeline_mode=resident),
        pl.BlockSpec((FFN_CONV, D_FF), lambda i: (0, 0)),
        pl.BlockSpec((1, D_FF), lambda i: (0, 0)),
        pl.BlockSpec((1, D_MODEL), lambda i: (0, 0)),
    ]
    args = [h2, x1, wg, wv, wd, cw, cb, npost]
    scratch = [pltpu.VMEM((tm, D_FF), BF16)]
    if carry_mode:
        gsave_shape = jax.ShapeDtypeStruct((nm, 8, D_FF), F32)
        gsave_spec = pl.BlockSpec((1, 8, D_FF), lambda i: (i, 0, 0))
        scratch += [pltpu.VMEM((2, tm + 8, tf), F32), pltpu.VMEM((8, D_FF), F32)]
    else:
        gsave_shape = jax.ShapeDtypeStruct((m, D_FF), F32)
        gsave_spec = pl.BlockSpec((tm, D_FF), lambda i: (i, 0))
        in_specs.append(pl.BlockSpec((tm, D_FF), lambda i: (i, 0)))
        args.append(prev_rows)
    return pl.pallas_call(
        functools.partial(_ffn_body, carry_mode, tiles_per_seq, t, tf),
        grid=(nm,),
        in_specs=in_specs,
        out_specs=[pl.BlockSpec((tm, D_MODEL), lambda i: (i, 0)), gsave_spec],
        out_shape=[jax.ShapeDtypeStruct((m, D_MODEL), F32), gsave_shape],
        scratch_shapes=scratch,
        compiler_params=_cparams(("arbitrary",)),
        name="ffn",
    )(*args)


def _prep_params(w_in, ssd_conv_w, ssd_conv_b, ssd_dt_bias, ssd_a_log, ssd_d, ssd_norm, pool_w, pool_scale,
                 nsa_pe_k, nsa_pe_v, nsa_w1_k, nsa_w1_v, nsa_w2_k, nsa_w2_v, w_out, ffn_w_gate, ffn_w_val,
                 ffn_w_down):
    depth = w_in.shape[0]
    w_in_p = jnp.concatenate(
        [w_in[..., 0:OFF_XBC], w_in[..., OFF_DT:OFF_KV], w_in[..., OFF_XBC:OFF_DT], w_in[..., OFF_KV:IN_W],
         jnp.zeros((depth, D_MODEL, IN_W_PAD - IN_W), F32)], axis=-1).astype(BF16)
    pad_h = ((0, 0), (0, 128 - SSD_H))
    dtb = jnp.pad(ssd_dt_bias, pad_h)[:, None, :]
    alog = jnp.pad(ssd_a_log, pad_h)[:, None, :]
    dsk = jnp.repeat(ssd_d, SSD_HD, axis=1)[:, None, :]
    n_g = len(POOL_WINDOWS)
    eye = jnp.eye(n_g, dtype=F32)
    pw_bd = (pool_w[:, :, :, None, :] * eye[None, :, None, :, None]).reshape(depth, POOL_W, POOL_W).astype(BF16)
    zk = jnp.zeros((depth, CMP_STRIDE, NSA_HD, CMP_HID), F32)
    top = jnp.concatenate([nsa_w1_k[:, :CMP_STRIDE], zk, nsa_w1_k[:, CMP_STRIDE:], zk], axis=-1)
    bot = jnp.concatenate([zk, nsa_w1_v[:, :CMP_STRIDE], zk, nsa_w1_v[:, CMP_STRIDE:]], axis=-1)
    w1 = jnp.concatenate([top, bot], axis=2).astype(BF16).reshape(depth, CMP_STRIDE // 2, 256, 4 * CMP_HID)
    pe_a = jnp.concatenate([nsa_pe_k[:, :CMP_STRIDE], nsa_pe_v[:, :CMP_STRIDE]], axis=-1)
    pe_b = jnp.concatenate([nsa_pe_k[:, CMP_STRIDE:], nsa_pe_v[:, CMP_STRIDE:]], axis=-1)
    pe = jnp.concatenate([pe_a[:, :, None], pe_b[:, :, None], jnp.zeros((depth, CMP_STRIDE, 6, 128), F32)], axis=2)
    pe = pe.reshape(depth, CMP_STRIDE // 2, 2, 8, 128).transpose(0, 1, 3, 2, 4).reshape(depth, CMP_STRIDE // 2, 8, 256)
    zw = jnp.zeros((depth, CMP_HID, NSA_HD), F32)
    w2 = jnp.concatenate([jnp.concatenate([nsa_w2_k, zw], axis=-1), jnp.concatenate([zw, nsa_w2_v], axis=-1)],
                         axis=1).astype(BF16)
    return dict(w_in=w_in_p, cw=ssd_conv_w, cb=ssd_conv_b[:, None, :], dtb=dtb, alog=alog, dsk=dsk,
                nrm=ssd_norm[:, None, :], pw=pw_bd, ps=pool_scale[:, None, :], w1=w1, pe=pe, w2=w2,
                w_out=w_out.astype(BF16), wg=ffn_w_gate.astype(BF16), wv=ffn_w_val.astype(BF16),
                wd=ffn_w_down.astype(BF16))


def _mixer_weights(p, l):
    return (p['cw'][l], p['cb'][l], p['dtb'][l], p['alog'][l], p['dsk'][l], p['nrm'][l], p['pw'][l], p['ps'][l])


def kernel(x_prompt, x_sample, cache_nsa_kv, page_table, state_nsa_win, state_ssd_conv, state_ssm, state_pool, state_ffn_conv, norm_mix_pre, w_in, ssd_conv_w, ssd_conv_b, ssd_dt_bias, ssd_a_log, ssd_d, ssd_norm, pool_w, pool_scale, nsa_pe_k, nsa_pe_v, nsa_w1_k, nsa_w1_v, nsa_w2_k, nsa_w2_v, w_out, norm_mix_post, norm_ffn_pre, ffn_w_gate, ffn_w_val, ffn_conv_w, ffn_conv_b, ffn_w_down, norm_ffn_post):
    depth = w_in.shape[0]
    bp, tp, _ = x_prompt.shape
    bs, ts, _ = x_sample.shape
    n_phys = cache_nsa_kv.shape[1]
    n_pages = page_table.shape[1]
    past = n_pages * PAGE_SIZE
    assert tp % 512 == 0 and tp >= WINDOW and ts == 8 and state_nsa_win.shape[2] == WINDOW
    p = _prep_params(w_in, ssd_conv_w, ssd_conv_b, ssd_dt_bias, ssd_a_log, ssd_d, ssd_norm, pool_w, pool_scale,
                     nsa_pe_k, nsa_pe_v, nsa_w1_k, nsa_w1_v, nsa_w2_k, nsa_w2_v, w_out, ffn_w_gate, ffn_w_val,
                     ffn_w_down)
    ms = bs * ts
    tab_p = _rope_table(jnp.arange(tp, dtype=jnp.int32))
    tab_s = jnp.tile(_rope_table(past + jnp.arange(ts, dtype=jnp.int32)), (bs, 1))
    cache = jnp.transpose(cache_nsa_kv, (0, 1, 3, 4, 2)).reshape(depth * n_phys, 4 * NSA_HD, PAGE_SIZE)
    win_state = jnp.transpose(state_nsa_win, (0, 1, 3, 4, 2)).reshape(depth * bs, 2 * NSA_HD, WINDOW)

    tm_p = 512
    tm_f = 512
    tf = 512
    nc_p = tp // CHUNK
    zeros_c = jnp.zeros((bp, 8, SSD_CONV_DIM), F32)
    zeros_s = jnp.zeros((bp, SSD_H, SSD_HD, SSD_N), F32)
    zeros_p = jnp.zeros((bp, 16, POOL_W), F32)

    xp = x_prompt.reshape(bp * tp, D_MODEL)
    xs = x_sample.reshape(ms, D_MODEL)
    outs_p, outs_s = [], []
    for l in range(depth):
        nw = norm_mix_pre[l][None]
        npost = norm_mix_post[l][None]
        nffn = norm_ffn_pre[l][None]
        nfpost = norm_ffn_post[l][None]
        mw = _mixer_weights(p, l)
        fcw, fcb = ffn_conv_w[l], ffn_conv_b[l][None]

        y_ssd, y_pool, ssm_new, q, kvc, kvw, dtg, ctail, ptail = _ssd_pool(
            xp, zeros_c, zeros_s, zeros_p, mw, bp, nc_p, CHUNK, 0, 2, fused_in=(nw, p['w_in'], l, tab_p))
        ckv = _nsa_cmp(kvc, p['w1'][l], p['pe'][l], p['w2'][l], bp, tp)
        y_nsa = _nsa_attn(q, dtg, ckv, kvc, kvw, bp, tp)
        x1, h2 = _out_proj(y_ssd, y_pool, y_nsa, xp, p['w_out'], l, npost, nffn, tm_p)
        xp, gsave = _ffn(h2, x1, p['wg'], p['wv'], p['wd'], l, fcw, fcb, nfpost, tm_f, tf, True, tp // tm_f, tp)
        outs_p.append((
            kvc.reshape(bp, tp, 4, NSA_HD),
            kvw.reshape(bp, tp, 2, NSA_HD)[:, tp - WINDOW:],
            ctail[:, 8 - (SSD_CONV - 1):],
            ssm_new,
            ptail[:, 16 - POOL_KEEP:],
            gsave.reshape(bp, tp // tm_f, 8, D_FF)[:, -1, 8 - (FFN_CONV - 1):],
        ))

        z, xbc, u, q, kvc, kvw, dtg = _in_proj(xs, nw, p['w_in'], l, tab_s, ms)
        cprev = jnp.pad(state_ssd_conv[l], ((0, 0), (8 - (SSD_CONV - 1), 0), (0, 0)))
        pprev = jnp.pad(state_pool[l], ((0, 0), (16 - POOL_KEEP, 0), (0, 0)))
        y_ssd, y_pool, ssm_new = _ssd_pool((xbc, z, dtg, u), cprev, state_ssm[l], pprev, mw, bs, 1, ts, past, 4)
        y_nsa = _nsa_dec(page_table, cache, q, dtg, kvc, kvw, win_state, p['w1'][l], p['pe'][l], p['w2'][l],
                         l, n_phys, bs, ts)
        x1, h2 = _out_proj(y_ssd, y_pool, y_nsa, xs, p['w_out'], l, npost, nffn, ms)
        fprev = state_ffn_conv[l]
        fprev_rows = jnp.pad(fprev, ((0, 0), (0, ts - (FFN_CONV - 1)), (0, 0))).reshape(ms, D_FF)
        xs, gsave = _ffn(h2, x1, p['wg'], p['wv'], p['wd'], l, fcw, fcb, nfpost, ms, tf, False, 1, ts, fprev_rows)
        kvw3 = kvw.reshape(bs, ts, 2, NSA_HD)
        xbc3 = xbc.reshape(bs, ts, SSD_CONV_DIM)
        u3 = u.reshape(bs, ts, POOL_W)
        g3 = gsave.reshape(bs, ts, D_FF)
        outs_s.append((
            kvc.reshape(bs, ts, 4, NSA_HD),
            jnp.concatenate([state_nsa_win[l], kvw3], axis=1)[:, -WINDOW:],
            jnp.concatenate([state_ssd_conv[l], xbc3], axis=1)[:, -(SSD_CONV - 1):],
            ssm_new,
            jnp.concatenate([state_pool[l], u3], axis=1)[:, -POOL_KEEP:],
            jnp.concatenate([state_ffn_conv[l], g3], axis=1)[:, -(FFN_CONV - 1):],
        ))

    def stk(outs, i):
        return jnp.stack([o[i] for o in outs])

    return (xp.reshape(bp, tp, D_MODEL), xs.reshape(bs, ts, D_MODEL), stk(outs_p, 0), stk(outs_s, 0),
            stk(outs_p, 1), stk(outs_s, 1), stk(outs_p, 2), stk(outs_s, 2), stk(outs_p, 3), stk(outs_s, 3),
            stk(outs_p, 4), stk(outs_s, 4), stk(outs_p, 5), stk(outs_s, 5))
```

```python
import functools
import math

import jax
import jax.numpy as jnp
from jax import lax
from jax.experimental import pallas as pl
from jax.experimental.pallas import tpu as pltpu

F32 = jnp.float32
BF16 = jnp.bfloat16
HIGHEST = lax.Precision.HIGHEST

D_MODEL = 1024
PAGE_SIZE = 128
SSD_W = 512
SSD_HD = 64
SSD_H = 8
SSD_N = 128
SSD_G = 2
SSD_CONV = 4
SSD_CONV_DIM = 1024
POOL_W = 256
POOL_WINDOWS = (2, 4, 8, 16)
POOL_GC = 64
POOL_KEEP = 15
NSA_W = 256
NSA_HD = 64
NSA_H = 4
CMP_STRIDE = 16
CMP_LEN = 32
CMP_HID = 128
SLC_BLOCK = 64
N_SELECT = 16
WINDOW = 512
ROPE_DIM = 16
ROPE_THETA = 500000.0
D_FF = 4096
FFN_CONV = 3
RMS_EPS = 1e-6
OFF_Z = 512
OFF_XBC = 1536
OFF_DT = 1544
OFF_POOL = 1800
OFF_Q = 2056
OFF_KV = 2440
IN_W = 2452
IN_W_PAD = 2560
CHUNK = 128
NEG = -1e30
LOG2E = 1.4426950408889634
VMEM_LIMIT = 56 * 1024 * 1024


def _cparams(sem):
    return pltpu.CompilerParams(dimension_semantics=sem, vmem_limit_bytes=VMEM_LIMIT)


def _rmsnorm(x, w):
    return x * lax.rsqrt(jnp.mean(x * x, axis=-1, keepdims=True) + RMS_EPS) * w


def _silu(x):
    return x / (1.0 + jnp.exp(-x))


def _gelu_tanh(x):
    k = 0.7978845608028654
    hx = 0.5 * x
    return hx + hx * jnp.tanh(x * (k + (k * 0.044715) * (x * x)))


def _dot(a, b):
    return jnp.dot(a, b, preferred_element_type=F32)


def _dot_nt(a, b):
    return lax.dot_general(a, b, (((1,), (1,)), ((), ())), preferred_element_type=F32)


def _split3(x):
    p0 = x.astype(BF16)
    r = x - p0.astype(F32)
    p1 = r.astype(BF16)
    return p0, p1, (r - p1.astype(F32)).astype(BF16)


def _dot_sel(parts, sel):
    return _dot(parts[0], sel) + _dot(parts[1], sel) + _dot(parts[2], sel)


def _in_proj_rows(x, nw, w_ref, tab_ref):
    r = x.shape[0]
    h = _rmsnorm(x, nw).astype(BF16)

    def proj(a, b):
        return _dot(h, w_ref[0, :, a:b])

    lane = lax.broadcasted_iota(jnp.int32, (r, 128), 1)
    first = (lane % NSA_HD) < (ROPE_DIM // 2)

    def rope(v, c, s):
        outs = []
        for j in range(v.shape[1] // 128):
            vj = v[:, j * 128:(j + 1) * 128]
            rot = jnp.where(first, pltpu.roll(vj, 128 - ROPE_DIM // 2, 1), pltpu.roll(vj, ROPE_DIM // 2, 1))
            outs.append(vj * c + rot * s)
        return outs

    q = rope(proj(1792, 2048), tab_ref[:, 0:128], tab_ref[:, 128:256])
    kv = rope(proj(2048, 2432), tab_ref[:, 256:384], tab_ref[:, 384:512])
    return (proj(0, 512), proj(512, 1536), proj(1536, 1792), jnp.concatenate(q, axis=1),
            jnp.concatenate(kv[0:2], axis=1), kv[2], proj(2432, 2560))


def _in_proj_body(x_ref, nw_ref, w_ref, tab_ref, z_ref, xbc_ref, pool_ref, q_ref, kvc_ref, kvw_ref, dtg_ref):
    outs = _in_proj_rows(x_ref[...], nw_ref[...], w_ref, tab_ref)
    for o_ref, o in zip((z_ref, xbc_ref, pool_ref, q_ref, kvc_ref, kvw_ref, dtg_ref), outs):
        o_ref[...] = o


def _in_proj(x2d, nw, w_in_p, layer, tab, tm):
    m = x2d.shape[0]
    tab_blocks = tab.shape[0] // tm
    widths = (512, 1024, 256, 256, 256, 128, 128)
    return pl.pallas_call(
        _in_proj_body,
        grid=(m // tm,),
        in_specs=[
            pl.BlockSpec((tm, D_MODEL), lambda i: (i, 0)),
            pl.BlockSpec((1, D_MODEL), lambda i: (0, 0)),
            pl.BlockSpec((1, D_MODEL, IN_W_PAD), lambda i: (layer, 0, 0)),
            pl.BlockSpec((tm, 512), lambda i: (i % tab_blocks, 0)),
        ],
        out_specs=[pl.BlockSpec((tm, w), lambda i: (i, 0)) for w in widths],
        out_shape=[jax.ShapeDtypeStruct((m, w), F32) for w in widths],
        compiler_params=_cparams(("arbitrary",)),
        name="in_proj",
    )(x2d, nw, w_in_p, tab)


def _rope_table(pos):
    half = ROPE_DIM // 2
    inv = 1.0 / (ROPE_THETA ** (jnp.arange(half, dtype=F32) / half))
    ang = pos.astype(F32)[:, None] * inv
    cos, sin = jnp.cos(ang), jnp.sin(ang)
    t = pos.shape[0]
    c_rot = jnp.concatenate([cos, cos, jnp.ones((t, NSA_HD - ROPE_DIM), F32)], axis=1)
    s_rot = jnp.concatenate([-sin, sin, jnp.zeros((t, NSA_HD - ROPE_DIM), F32)], axis=1)
    one, zero = jnp.ones((t, NSA_HD), F32), jnp.zeros((t, NSA_HD), F32)
    return jnp.concatenate([c_rot, c_rot, s_rot, s_rot, c_rot, one, s_rot, zero], axis=1)


def _ssd_pool_body(lr, nc, pos0, nsub, fused, *refs):
    if fused:
        (x_ref, nw_ref, win_ref, tab_ref, cprev_ref, sprev_ref, pprev_ref,
         cw_ref, cb_ref, dtb_ref, alog_ref, dsk_ref, nrm_ref, pw_ref, ps_ref,
         yssd_ref, ypool_ref, snew_ref, q_ref, kvc_ref, kvw_ref, dtg_ref, ctail_ref, ptail_ref,
         ext_ref, state_ref, pext_ref) = refs
    else:
        (xbc_ref, z_ref, dtg_ref, u_ref, cprev_ref, sprev_ref, pprev_ref,
         cw_ref, cb_ref, dtb_ref, alog_ref, dsk_ref, nrm_ref, pw_ref, ps_ref,
         yssd_ref, ypool_ref, snew_ref, ext_ref, state_ref, pext_ref) = refs
    c = pl.program_id(1)
    L = CHUNK

    @pl.when(c == 0)
    def _():
        ext_ref[:, 0:8, :] = cprev_ref[...]
        pext_ref[:, 0:16, :] = pprev_ref[...]
        state_ref[...] = sprev_ref[...]

    if nc > 1:
        @pl.when(c > 0)
        def _():
            ext_ref[:, 0:8, :] = ext_ref[:, L:L + 8, :]
            pext_ref[:, 0:16, :] = pext_ref[:, L:L + 16, :]

    z_rows, dt_rows = [], []
    for sub in range(nsub):
        if fused:
            z, xbc, u, q, kvc, kvw, dtg = _in_proj_rows(x_ref[sub], nw_ref[...], win_ref, tab_ref)
            ext_ref[sub, 8:8 + lr, :] = xbc
            pext_ref[sub, 16:16 + lr, :] = u
            q_ref[sub] = q
            kvc_ref[sub] = kvc
            kvw_ref[sub] = kvw
            dtg_ref[sub] = dtg
        else:
            z, dtg = z_ref[sub], dtg_ref[sub]
            ext_ref[sub, 8:8 + lr, :] = xbc_ref[sub]
            pext_ref[sub, 16:16 + lr, :] = u_ref[sub]
        z_rows.append(z)
        dt_rows.append(dtg)
    if lr < L:
        ext_ref[:, 8 + lr:8 + L, :] = jnp.zeros((nsub, L - lr, SSD_CONV_DIM), F32)
        pext_ref[:, 16 + lr:16 + L, :] = jnp.zeros((nsub, L - lr, POOL_W), F32)

    cw = cw_ref[...]
    row = lax.broadcasted_iota(jnp.int32, (L, L), 0)
    col = lax.broadcasted_iota(jnp.int32, (L, L), 1)
    tri = row >= col
    tri_b = tri.astype(F32).astype(BF16)
    eye_b = (row == col).astype(F32).astype(BF16)
    gw = (SSD_H // SSD_G) * SSD_HD
    expand = (lax.broadcasted_iota(jnp.int32, (128, SSD_W), 0)
              == lax.broadcasted_iota(jnp.int32, (128, SSD_W), 1) // SSD_HD).astype(F32).astype(BF16)
    neg_a = -jnp.exp(alog_ref[...])
    dsk = dsk_ref[...]
    prow = lax.broadcasted_iota(jnp.int32, (L, POOL_W), 0)
    plane = lax.broadcasted_iota(jnp.int32, (L, POOL_W), 1)
    pos = pos0 + c * L + prow + 1

    for sub in range(nsub):
        conv = (ext_ref[sub, 8:8 + L, :] * cw[3:4, :] + ext_ref[sub, 7:7 + L, :] * cw[2:3, :]
                + ext_ref[sub, 6:6 + L, :] * cw[1:2, :] + ext_ref[sub, 5:5 + L, :] * cw[0:1, :] + cb_ref[...])
        xbc = _silu(conv)
        xs = xbc[:, 0:SSD_W]
        bmat = [xbc[:, SSD_W + g * SSD_N:SSD_W + (g + 1) * SSD_N].astype(BF16) for g in range(SSD_G)]
        cmat = [xbc[:, SSD_W + (SSD_G + g) * SSD_N:SSD_W + (SSD_G + g + 1) * SSD_N].astype(BF16)
                for g in range(SSD_G)]

        if lr == L:
            dtraw = dt_rows[sub]
        else:
            dtraw = jnp.concatenate([dt_rows[sub], jnp.zeros((L - lr, 128), F32)], axis=0)
        xdt = dtraw + dtb_ref[...]
        dt = jnp.maximum(xdt, 0.0) + jnp.log1p(jnp.exp(-jnp.abs(xdt)))
        dt = jnp.where((col < SSD_H) & (row < lr), dt, 0.0)
        da3 = _split3(dt * neg_a)
        acum = _dot(tri_b, da3[0]) + _dot(tri_b, da3[1]) + _dot(tri_b, da3[2])
        ac3 = _split3(acum)
        arow = _dot_nt(eye_b, ac3[0]) + _dot_nt(eye_b, ac3[1]) + _dot_nt(eye_b, ac3[2])
        dt_x = _dot_sel(_split3(dt), expand)
        ac_x = _dot_sel(ac3, expand)
        alast_x = ac_x[L - 1:L, :]
        xdt = xs * dt_x
        xdt_b = xdt.astype(BF16)
        xw = (xdt * jnp.exp(alast_x - ac_x)).astype(BF16)
        cb_scores = [_dot_nt(cmat[g], bmat[g]) for g in range(SSD_G)]
        y_off = []
        for g in range(SSD_G):
            gs = slice(g * gw, (g + 1) * gw)
            st = state_ref[sub, g]
            y_off.append(_dot(cmat[g], st.astype(BF16)))
            b_t = _dot_nt(eye_b, bmat[g]).astype(BF16)
            state_ref[sub, g] = jnp.exp(alast_x[:, gs]) * st + _dot(b_t, xw[:, gs])
        y_diag = []
        for hp in range(SSD_H // 2):
            tile_b = xdt_b[:, hp * 128:(hp + 1) * 128]
            acc = None
            for h in (2 * hp, 2 * hp + 1):
                g = h // (SSD_H // SSD_G)
                decay = jnp.exp(jnp.where(tri, acum[:, h:h + 1] - arow[h:h + 1, :], NEG))
                own = (col < SSD_HD) if h % 2 == 0 else (col >= SSD_HD)
                part = _dot((cb_scores[g] * decay).astype(BF16), jnp.where(own, tile_b, jnp.zeros_like(tile_b)))
                acc = part if acc is None else acc + part
            y_diag.append(acc)
        y = jnp.concatenate(y_diag, axis=1) + jnp.concatenate(y_off, axis=1) * jnp.exp(ac_x) + dsk * xs
        if lr < L:
            y = y[0:lr]
        y = y * _silu(z_rows[sub])
        yssd_ref[sub] = _rmsnorm(y, nrm_ref[...]).astype(BF16)

        e1 = pext_ref[sub]
        s2 = e1 + pltpu.roll(e1, 1, 0)
        s4 = s2 + pltpu.roll(s2, 2, 0)
        s8 = s4 + pltpu.roll(s4, 4, 0)
        s16 = s8 + pltpu.roll(s8, 8, 0)
        pooled = jnp.zeros((L, POOL_W), F32)
        for gi, (w, sw) in enumerate(zip(POOL_WINDOWS, (s2, s4, s8, s16))):
            cnt = jnp.minimum(pos, w).astype(F32)
            grp = (plane >= gi * POOL_GC) & (plane < (gi + 1) * POOL_GC)
            pooled = jnp.where(grp, sw[16:16 + L] / cnt, pooled)
        pooled = pooled - e1[16:16 + L]
        yp = _dot(pooled.astype(BF16), pw_ref[...]) * ps_ref[...]
        ypool_ref[sub] = (yp[0:lr] if lr < L else yp).astype(BF16)

    @pl.when(c == nc - 1)
    def _():
        snew_ref[...] = state_ref[...]
        if fused:
            ctail_ref[...] = ext_ref[:, L:L + 8, :]
            ptail_ref[...] = pext_ref[:, L:L + 16, :]


def _ssd_pool(acts, cprev, sprev, pprev, wts, nb, nc, lr, pos0, nsub, fused_in=None):
    cw, cb, dtb, alog, dsk, nrm, pw, ps = wts
    t = nc * lr
    assert nb % nsub == 0

    def tile(w):
        return pl.BlockSpec((nsub, lr, w), lambda b, c: (b, c, 0))

    def full2(a):
        return pl.BlockSpec(a.shape, lambda b, c: (0, 0))

    def rows(w, dtype=F32):
        return jax.ShapeDtypeStruct((nb, t, w), dtype)

    hg = SSD_H // SSD_G
    st_shape = (SSD_G, SSD_N, hg * SSD_HD)
    sprev_t = sprev.reshape(nb, SSD_G, hg, SSD_HD, SSD_N).transpose(0, 1, 4, 2, 3).reshape((nb,) + st_shape)
    state_spec = pl.BlockSpec((nsub,) + st_shape, lambda b, c: (b, 0, 0, 0))
    prev_specs = [pl.BlockSpec((nsub, 8, SSD_CONV_DIM), lambda b, c: (b, 0, 0)), state_spec,
                  pl.BlockSpec((nsub, 16, POOL_W), lambda b, c: (b, 0, 0))]
    w_specs = [full2(cw), full2(cb), full2(dtb), full2(alog), full2(dsk), full2(nrm), full2(pw), full2(ps)]
    out_specs = [tile(SSD_W), tile(POOL_W), state_spec]
    out_shape = [rows(SSD_W, BF16), rows(POOL_W, BF16), jax.ShapeDtypeStruct((nb,) + st_shape, F32)]
    if fused_in is None:
        xbc, z, dtg, u = acts
        in_specs = [tile(SSD_CONV_DIM), tile(SSD_W), tile(128), tile(POOL_W)]
        args = [xbc.reshape(nb, t, SSD_CONV_DIM), z.reshape(nb, t, SSD_W), dtg.reshape(nb, t, 128),
                u.reshape(nb, t, POOL_W)]
    else:
        nw, w_in_p, layer, tab = fused_in
        assert lr == CHUNK
        in_specs = [tile(D_MODEL), pl.BlockSpec((1, D_MODEL), lambda b, c: (0, 0)),
                    pl.BlockSpec((1, D_MODEL, IN_W_PAD), lambda b, c: (layer, 0, 0), pipeline_mode=pl.Buffered(1)),
                    pl.BlockSpec((lr, 512), lambda b, c: (c, 0))]
        args = [acts.reshape(nb, t, D_MODEL), nw, w_in_p, tab]
        out_specs += [tile(NSA_W), tile(4 * NSA_HD), tile(2 * NSA_HD), tile(128),
                      pl.BlockSpec((nsub, 8, SSD_CONV_DIM), lambda b, c: (b, 0, 0)),
                      pl.BlockSpec((nsub, 16, POOL_W), lambda b, c: (b, 0, 0))]
        out_shape += [rows(NSA_W), rows(4 * NSA_HD), rows(2 * NSA_HD), rows(128),
                      jax.ShapeDtypeStruct((nb, 8, SSD_CONV_DIM), F32), jax.ShapeDtypeStruct((nb, 16, POOL_W), F32)]
    outs = pl.pallas_call(
        functools.partial(_ssd_pool_body, lr, nc, pos0, nsub, fused_in is not None),
        grid=(nb // nsub, nc),
        in_specs=in_specs + prev_specs + w_specs,
        out_specs=out_specs,
        out_shape=out_shape,
        scratch_shapes=[pltpu.VMEM((nsub, CHUNK + 8, SSD_CONV_DIM), F32),
                        pltpu.VMEM((nsub,) + st_shape, F32),
                        pltpu.VMEM((nsub, CHUNK + 16, POOL_W), F32)],
        compiler_params=_cparams(("arbitrary", "arbitrary")),
        name="ssd_pool",
    )(*args, cprev, sprev_t, pprev, cw, cb, dtb, alog, dsk, nrm, pw, ps)
    yssd, ypool, snew_t = outs[:3]
    snew = snew_t.reshape(nb, SSD_G, SSD_N, hg, SSD_HD).transpose(0, 1, 3, 4, 2).reshape(nb, SSD_H, SSD_HD, SSD_N)
    flat = [o.reshape(nb * t, o.shape[-1]) for o in outs[3:7]]
    return (yssd.reshape(nb * t, SSD_W), ypool.reshape(nb * t, POOL_W), snew) + tuple(flat) + tuple(outs[7:])


def _compress(load_rows, n_rows, w1_ref, pe_ref, w2_ref, pq_ref):
    acc = jnp.zeros((n_rows, 4 * CMP_HID), F32)
    cacc = jnp.zeros((8, 4 * CMP_HID), F32)
    for l2 in range(CMP_STRIDE // 2):
        w = w1_ref[l2]
        rows = jnp.concatenate([load_rows(2 * l2), load_rows(2 * l2 + 1)], axis=1)
        acc = acc + _dot(rows.astype(BF16), w)
        cacc = cacc + _dot(pe_ref[l2].astype(BF16), w)
    pq_ref[0:n_rows, :] = acc
    pq_ref[n_rows:n_rows + 8, :] = jnp.zeros((8, 4 * CMP_HID), F32)
    const = cacc[0:1, 0:2 * CMP_HID] + cacc[1:2, 2 * CMP_HID:4 * CMP_HID]
    hid = pq_ref[0:n_rows, 0:2 * CMP_HID] + pq_ref[1:n_rows + 1, 2 * CMP_HID:4 * CMP_HID] + const
    return _dot(_gelu_tanh(hid).astype(BF16), w2_ref[...])


def _stack_heads(x):
    return jnp.concatenate([x[:, h * NSA_HD:(h + 1) * NSA_HD] for h in range(NSA_H)], axis=0)


def _gated_mix(dtg, o_cmp, o_slc, o_win, r):
    sig = 1.0 / (1.0 + jnp.exp(-dtg))
    outs = []
    for h in range(NSA_H):
        b = SSD_H + 3 * h
        sl = slice(h * r, (h + 1) * r)
        outs.append(sig[:, b:b + 1] * o_cmp[sl] + sig[:, b + 1:b + 2] * o_slc[sl] + sig[:, b + 2:b + 3] * o_win[sl])
    return jnp.concatenate(outs, axis=1)


def _nsa_cmp_body(kv_ref, w1_ref, pe_ref, w2_ref, ckv_ref, pq_ref):
    n_ch = kv_ref.shape[0] // CMP_STRIDE
    ckv_ref[...] = _compress(lambda l: kv_ref[pl.ds(l, n_ch, stride=CMP_STRIDE), :], n_ch, w1_ref, pe_ref, w2_ref,
                             pq_ref)


def _nsa_cmp(kvc, w1, pe, w2, nb, t):
    n_ch = t // CMP_STRIDE
    return pl.pallas_call(
        _nsa_cmp_body,
        grid=(nb,),
        in_specs=[
            pl.BlockSpec((t, 128), lambda b: (b, 0)),
            pl.BlockSpec(w1.shape, lambda b: (0, 0, 0)),
            pl.BlockSpec(pe.shape, lambda b: (0, 0, 0)),
            pl.BlockSpec(w2.shape, lambda b: (0, 0)),
        ],
        out_specs=pl.BlockSpec((n_ch, 128), lambda b: (b, 0)),
        out_shape=jax.ShapeDtypeStruct((nb * n_ch, 128), F32),
        scratch_shapes=[pltpu.VMEM((n_ch + 8, 4 * CMP_HID), F32)],
        compiler_params=_cparams(("arbitrary",)),
        name="nsa_cmp",
    )(kvc, w1, pe, w2)


def _flash_step_t(carry, st, bias, vt):
    m, l, acc = carry
    sm = st + bias
    m_new = jnp.maximum(m, jnp.max(sm, axis=0, keepdims=True))
    p = jnp.exp2(sm - m_new)
    alpha = jnp.exp2(m - m_new)
    l = alpha * l + jnp.sum(p, axis=0, keepdims=True)
    acc = alpha * acc + _dot(vt, p.astype(BF16))
    return m_new, l, acc


def _flash_init_t(cols):
    return jnp.full((1, cols), NEG, F32), jnp.zeros((1, cols), F32), jnp.zeros((NSA_HD, cols), F32)


def _nsa_attn_body(t, q_ref, dtg_ref, ckv_ref, kvs_ref, kvw_ref, y_ref, ks_ref, kst_ref, kw_ref, kwt_ref, selt_ref):
    s = pl.program_id(1)
    qb = CHUNK
    cols = NSA_H * qb
    n_kc = t // qb
    n_ch = t // CMP_STRIDE
    n_cmp = n_ch - 1
    n_slc = t // SLC_BLOCK
    nb_pad = -(-n_slc // 8) * 8
    n_sel = min(N_SELECT, n_slc)

    kb = ks_ref.shape[1] // qb
    n_wc = WINDOW // qb + 1

    @pl.when(s == 0)
    def _():
        for c in range(n_kc):
            tile = kvs_ref[c * qb:(c + 1) * qb, :]
            ks_ref[c // kb, (c % kb) * qb:(c % kb + 1) * qb, :] = tile.astype(BF16)
            kst_ref[c // kb, :, (c % kb) * qb:(c % kb + 1) * qb] = tile.T.astype(BF16)
            tile = kvw_ref[c * qb:(c + 1) * qb, :]
            kw_ref[c] = tile.astype(BF16)
            kwt_ref[c] = tile.T.astype(BF16)

    q_t = (q_ref[...] * (NSA_HD ** -0.5 * LOG2E)).T
    qs_t = jnp.concatenate([q_t[h * NSA_HD:(h + 1) * NSA_HD, :] for h in range(NSA_H)], axis=1).astype(BF16)
    lane = lax.broadcasted_iota(jnp.int32, (1, cols), 1)
    tpos = s * qb + lane % qb
    tpos_h = tpos[:, 0:qb]

    k0 = jnp.maximum(s - (n_wc - 1), 0)
    kw = jnp.concatenate([kw_ref[k0 + i, :, 0:NSA_HD] for i in range(n_wc)], axis=0)
    vwt = jnp.concatenate([kwt_ref[k0 + i, NSA_HD:2 * NSA_HD, :] for i in range(n_wc)], axis=1)
    kpos = k0 * qb + lax.broadcasted_iota(jnp.int32, (n_wc * qb, qb), 0)
    wbias = jnp.where((kpos <= tpos_h) & (kpos > tpos_h - WINDOW), 0.0, NEG)
    sw = _dot(kw, qs_t) + jnp.concatenate([wbias] * NSA_H, axis=1)
    pw = jnp.exp2(sw - jnp.max(sw, axis=0, keepdims=True))
    o_win = _dot(vwt, pw.astype(BF16)) / jnp.sum(pw, axis=0, keepdims=True)
    sig = 1.0 / (1.0 + jnp.exp(-dtg_ref[...].T[0:32, :]))

    ckv = ckv_ref[...]
    ck = ckv[:, 0:NSA_HD].astype(BF16)
    cv_t = ckv.T[NSA_HD:2 * NSA_HD, :].astype(BF16)
    sc = _dot(ck, qs_t)
    nrow = lax.broadcasted_iota(jnp.int32, (n_ch, cols), 0)
    cmask = (nrow * CMP_STRIDE + (CMP_LEN - 1) <= tpos) & (nrow < n_cmp)
    mx = jnp.max(jnp.where(cmask, sc, NEG), axis=0, keepdims=True)
    e = jnp.where(cmask, jnp.exp2(sc - mx), 0.0)
    pc = e / jnp.maximum(jnp.sum(e, axis=0, keepdims=True), 1e-30)
    o_cmp = _dot(cv_t, pc.astype(BF16))
    pcsum = pc[:, 0:qb] + pc[:, qb:2 * qb] + pc[:, 2 * qb:3 * qb] + pc[:, 3 * qb:4 * qb]

    jrow = lax.broadcasted_iota(jnp.int32, (nb_pad, n_ch), 0)
    ncol = lax.broadcasted_iota(jnp.int32, (nb_pad, n_ch), 1)
    ov_t = ((ncol * CMP_STRIDE < (jrow + 1) * SLC_BLOCK) & (ncol * CMP_STRIDE + CMP_LEN > jrow * SLC_BLOCK)
            & (ncol < n_cmp)).astype(F32)
    imp = jnp.dot(ov_t, pcsum, precision=HIGHEST, preferred_element_type=F32)
    brow = lax.broadcasted_iota(jnp.int32, (nb_pad, qb), 0)
    tq = s * qb + lax.broadcasted_iota(jnp.int32, (nb_pad, qb), 1)
    cur = tq // SLC_BLOCK
    forced = (brow == 0) | (brow == cur) | (brow == cur - 1)
    imp = jnp.where(forced, jnp.inf, imp)
    imp = jnp.where((brow * SLC_BLOCK <= tq) & (brow < n_slc), imp, -jnp.inf)
    rank = jnp.zeros((nb_pad, qb), F32)
    for i in range(n_slc):
        ri = imp[i:i + 1, :]
        before = (ri > imp) | ((ri == imp) & (brow > i))
        rank = rank + before.astype(F32)
    selt_ref[...] = jnp.where((rank < n_sel) & (imp > -jnp.inf), 0.0, NEG)

    bps = kb * qb // SLC_BLOCK
    krow_s = lax.broadcasted_iota(jnp.int32, (kb * qb, qb), 0)

    def slc_step(c, carry):
        k = ks_ref[c, :, 0:NSA_HD]
        vt = kst_ref[c, NSA_HD:2 * NSA_HD, :]
        sel_rows = selt_ref[pl.ds(pl.multiple_of(c * bps, bps), bps), :]
        bias = jnp.concatenate([jnp.broadcast_to(sel_rows[j:j + 1, :], (SLC_BLOCK, qb)) for j in range(bps)],
                               axis=0)
        bias = jnp.where(c * (kb * qb) + krow_s <= tpos_h, bias, NEG)
        return _flash_step_t(carry, _dot(k, qs_t), jnp.concatenate([bias] * NSA_H, axis=1), vt)

    _, l_s, acc_s = lax.fori_loop(0, (s + kb) // kb, slc_step, _flash_init_t(cols))
    o_slc = acc_s / l_s

    outs = []
    for h in range(NSA_H):
        b = SSD_H + 3 * h
        sl = slice(h * qb, (h + 1) * qb)
        outs.append(sig[b:b + 1, :] * o_cmp[:, sl] + sig[b + 1:b + 2, :] * o_slc[:, sl]
                    + sig[b + 2:b + 3, :] * o_win[:, sl])
    y_ref[...] = jnp.concatenate(outs, axis=0).T.astype(BF16)


def _nsa_attn(q, dtg, ckv, kvc, kvw, nb, t):
    nq = t // CHUNK
    n_ch = t // CMP_STRIDE
    kb = 4
    assert n_ch % 128 == 0 and nq % kb == 0 and nq > WINDOW // CHUNK
    nb_pad = -(-(t // SLC_BLOCK) // 8) * 8
    kv_scratch = pltpu.VMEM((nq, CHUNK, 2 * NSA_HD), BF16)
    return pl.pallas_call(
        functools.partial(_nsa_attn_body, t),
        grid=(nb, nq),
        in_specs=[
            pl.BlockSpec((CHUNK, NSA_W), lambda b, s: (b * nq + s, 0)),
            pl.BlockSpec((CHUNK, 128), lambda b, s: (b * nq + s, 0)),
            pl.BlockSpec((n_ch, 128), lambda b, s: (b, 0)),
            pl.BlockSpec((t, 128), lambda b, s: (b, 1)),
            pl.BlockSpec((t, 128), lambda b, s: (b, 0)),
        ],
        out_specs=pl.BlockSpec((CHUNK, NSA_W), lambda b, s: (b * nq + s, 0)),
        out_shape=jax.ShapeDtypeStruct((nb * t, NSA_W), BF16),
        scratch_shapes=[pltpu.VMEM((nq // kb, kb * CHUNK, 2 * NSA_HD), BF16),
                        pltpu.VMEM((nq // kb, 2 * NSA_HD, kb * CHUNK), BF16),
                        kv_scratch, kv_scratch, pltpu.VMEM((nb_pad, CHUNK), F32)],
        compiler_params=_cparams(("arbitrary", "arbitrary")),
        name="nsa_attn",
    )(q, dtg, ckv, kvc, kvw)


def _softmax_pv_nt(sc, mask, vt):
    mx = jnp.max(jnp.where(mask, sc, NEG), axis=1, keepdims=True)
    e = jnp.where(mask, jnp.exp(sc - mx), 0.0)
    l = jnp.sum(e, axis=1, keepdims=True)
    return _dot_nt(e.astype(BF16), vt) / jnp.maximum(l, 1e-30)


def _rows_to_cols(x):
    r, w = x.shape
    return jnp.concatenate([x, jnp.zeros((128 - r, w), F32)], axis=0).T


def _nsa_dec_body(layer, n_phys, n_pages, nb, t, pt_ref, cache_ref, q_ref, dtg_ref, kvn_ref, kwn_ref, win_ref,
                  w1_ref, pe_ref, w2_ref, y_ref, xt_ref, xc_ref, pq_ref, wk_ref, sem):
    b = pl.program_id(0)
    past = n_pages * PAGE_SIZE
    lk = past + t
    n_ch = -(-lk // CMP_STRIDE)
    n_cmp = n_ch - 1
    xrows = xt_ref.shape[1] * PAGE_SIZE
    nch_pad = xrows // CMP_STRIDE
    n_slc = -(-lk // SLC_BLOCK)
    nblk = -(-(xrows // SLC_BLOCK) // 128) * 128
    n_sel = min(N_SELECT, n_slc)
    slot = b % 2

    def page_copy(bb, sl, p):
        phys = pt_ref[bb, p] + layer * n_phys
        return pltpu.make_async_copy(cache_ref.at[phys], xt_ref.at[sl, p], sem.at[sl])

    def start_all(bb, sl):
        def body(p, carry):
            page_copy(bb, sl, p).start()
            return carry
        lax.fori_loop(0, n_pages, body, 0, unroll=8)

    @pl.when(b == 0)
    def _():
        start_all(0, 0)

    @pl.when(b + 1 < nb)
    def _():
        start_all(b + 1, 1 - slot)

    def wait_body(p, carry):
        page_copy(b, slot, p).wait()
        return carry
    lax.fori_loop(0, n_pages, wait_body, 0, unroll=8)

    xt_ref[slot, n_pages] = _rows_to_cols(kvn_ref[...])

    for p in range(n_pages + 1):
        xc_ref[p * PAGE_SIZE:(p + 1) * PAGE_SIZE, :] = xt_ref[slot, p, 0:128, :].T

    rows = NSA_H * t
    qs = _stack_heads(q_ref[...] * (NSA_HD ** -0.5)).astype(BF16)
    ti = lax.broadcasted_iota(jnp.int32, (rows, 1), 0) % t
    tpos = past + ti

    ckv = _compress(lambda l: xc_ref[pl.ds(l, nch_pad, stride=CMP_STRIDE), :], nch_pad, w1_ref, pe_ref,
                    w2_ref, pq_ref)
    nc_use = (n_cmp + 127) // 128 * 128
    ck = ckv[0:nc_use, 0:NSA_HD].astype(BF16)
    cv = ckv[0:nc_use, NSA_HD:2 * NSA_HD].astype(BF16)
    sc = _dot_nt(qs, ck)
    ncol = lax.broadcasted_iota(jnp.int32, (rows, nc_use), 1)
    cmask = (ncol * CMP_STRIDE + (CMP_LEN - 1) <= tpos) & (ncol < n_cmp)
    mx = jnp.max(jnp.where(cmask, sc, NEG), axis=1, keepdims=True)
    e = jnp.where(cmask, jnp.exp(sc - mx), 0.0)
    pc = e / jnp.maximum(jnp.sum(e, axis=1, keepdims=True), 1e-30)
    o_cmp = _dot(pc.astype(BF16), cv)
    pcsum = pc[0:t] + pc[t:2 * t] + pc[2 * t:3 * t] + pc[3 * t:4 * t]

    nrow = lax.broadcasted_iota(jnp.int32, (nc_use, nblk), 0)
    jcol = lax.broadcasted_iota(jnp.int32, (nc_use, nblk), 1)
    ov = ((nrow * CMP_STRIDE < (jcol + 1) * SLC_BLOCK) & (nrow * CMP_STRIDE + CMP_LEN > jcol * SLC_BLOCK)
          & (nrow < n_cmp)).astype(F32)
    imp = jnp.dot(pcsum, ov, precision=HIGHEST, preferred_element_type=F32)
    jl = lax.broadcasted_iota(jnp.int32, (t, nblk), 1)
    tq = past + lax.broadcasted_iota(jnp.int32, (t, nblk), 0)
    cur = tq // SLC_BLOCK
    forced = (jl == 0) | (jl == cur) | (jl == cur - 1)
    imp = jnp.where(forced, jnp.inf, imp)
    imp = jnp.where((jl * SLC_BLOCK <= tq) & (jl < n_slc), imp, -jnp.inf)
    imp_t = jnp.concatenate([imp, jnp.full((128 - t, nblk), -jnp.inf, F32)], axis=0).T
    nb_rows = -(-n_slc // 8) * 8
    irow = lax.broadcasted_iota(jnp.int32, (nb_rows, nblk), 0)
    jlane = lax.broadcasted_iota(jnp.int32, (nb_rows, nblk), 1)
    ranks = []
    for qi in range(t):
        colv = imp_t[0:nb_rows, qi:qi + 1]
        rowv = imp[qi:qi + 1, :]
        before = (colv > rowv) | ((colv == rowv) & (irow < jlane))
        ranks.append(jnp.sum(before.astype(F32), axis=0, keepdims=True))
    rank = jnp.concatenate(ranks, axis=0)
    sel = ((rank < n_sel) & (imp > -jnp.inf)).astype(F32)
    sel4 = jnp.concatenate([sel] * NSA_H, axis=0)

    bpp = PAGE_SIZE // SLC_BLOCK
    plane = lax.broadcasted_iota(jnp.int32, (rows, PAGE_SIZE), 1)
    parts = []
    for p in range(n_pages + 1):
        scp = _dot(qs, xt_ref[slot, p, 2 * NSA_HD:3 * NSA_HD, :].astype(BF16))
        mp = jnp.where(plane < SLC_BLOCK, sel4[:, bpp * p:bpp * p + 1], sel4[:, bpp * p + 1:bpp * p + 2]) > 0.5
        if p == n_pages:
            mp = mp & (past + plane <= tpos)
        parts.append(jnp.where(mp, scp, NEG))
    sc = jnp.concatenate(parts, axis=1)
    e = jnp.exp(sc - jnp.max(sc, axis=1, keepdims=True))
    acc = jnp.zeros((rows, NSA_HD), F32)
    for p in range(n_pages + 1):
        acc = acc + _dot_nt(e[:, p * PAGE_SIZE:(p + 1) * PAGE_SIZE].astype(BF16),
                            xt_ref[slot, p, 3 * NSA_HD:4 * NSA_HD, :].astype(BF16))
    o_slc = acc / jnp.sum(e, axis=1, keepdims=True)

    wrows = wk_ref.shape[1]
    wk_ref[:, 0:WINDOW] = win_ref[0]
    wk_ref[:, WINDOW:wrows] = _rows_to_cols(kwn_ref[...])
    kwt = wk_ref[0:NSA_HD, :].astype(BF16)
    vwt = wk_ref[NSA_HD:2 * NSA_HD, :].astype(BF16)
    wpos = past - WINDOW + lax.broadcasted_iota(jnp.int32, (rows, wrows), 1)
    wmask = (wpos >= 0) & (wpos <= tpos) & (wpos > tpos - WINDOW)
    o_win = _softmax_pv_nt(_dot(qs, kwt), wmask, vwt)

    y_ref[0] = _gated_mix(dtg_ref[...], o_cmp, o_slc, o_win, t).astype(BF16)


def _nsa_dec(page_table, cache_t, q, dtg, kvc, kvw, win_t, w1, pe, w2, layer, n_phys, nb, t):
    n_pages = page_table.shape[1]
    xrows = (n_pages + 1) * PAGE_SIZE
    assert t <= PAGE_SIZE and n_pages % 8 == 0
    grid_spec = pltpu.PrefetchScalarGridSpec(
        num_scalar_prefetch=1,
        grid=(nb,),
        in_specs=[
            pl.BlockSpec(memory_space=pl.ANY),
            pl.BlockSpec((t, NSA_W), lambda b, pt: (b, 0)),
            pl.BlockSpec((t, 128), lambda b, pt: (b, 0)),
            pl.BlockSpec((t, 4 * NSA_HD), lambda b, pt: (b, 0)),
            pl.BlockSpec((t, 2 * NSA_HD), lambda b, pt: (b, 0)),
            pl.BlockSpec((1, 2 * NSA_HD, WINDOW), lambda b, pt: (layer * nb + b, 0, 0)),
            pl.BlockSpec(w1.shape, lambda b, pt: (0, 0, 0)),
            pl.BlockSpec(pe.shape, lambda b, pt: (0, 0, 0)),
            pl.BlockSpec(w2.shape, lambda b, pt: (0, 0)),
        ],
        out_specs=pl.BlockSpec((1, t, NSA_W), lambda b, pt: (b, 0, 0)),
        scratch_shapes=[
            pltpu.VMEM((2, n_pages + 1, 4 * NSA_HD, PAGE_SIZE), F32),
            pltpu.VMEM((xrows, 128), F32),
            pltpu.VMEM((xrows // CMP_STRIDE + 8, 4 * CMP_HID), F32),
            pltpu.VMEM((2 * NSA_HD, WINDOW + 128), F32),
            pltpu.SemaphoreType.DMA((2,)),
        ],
    )
    return pl.pallas_call(
        functools.partial(_nsa_dec_body, layer, n_phys, n_pages, nb, t),
        grid_spec=grid_spec,
        out_shape=jax.ShapeDtypeStruct((nb, t, NSA_W), BF16),
        compiler_params=_cparams(("arbitrary",)),
        name="nsa_dec",
    )(page_table, cache_t, q, dtg, kvc, kvw, win_t, w1, pe, w2).reshape(nb * t, NSA_W)


def _out_proj_body(ys_ref, yp_ref, yn_ref, x_ref, w_ref, npost_ref, nffn_ref, x1_ref, h2_ref):
    tm = x_ref.shape[0]
    n_split = 2 if tm % 32 == 0 else 1
    for i in range(n_split):
        r = slice(i * tm // n_split, (i + 1) * tm // n_split)
        mix = (_dot(ys_ref[r, :], w_ref[0, 0:SSD_W, :]) + _dot(yp_ref[r, :], w_ref[0, SSD_W:SSD_W + POOL_W, :])
               + _dot(yn_ref[r, :], w_ref[0, SSD_W + POOL_W:D_MODEL, :]))
        x1 = x_ref[r, :] + _rmsnorm(mix, npost_ref[...])
        x1_ref[r, :] = x1
        h2_ref[r, :] = _rmsnorm(x1, nffn_ref[...]).astype(BF16)


def _out_proj(ys, yp, yn, x2d, w_out, layer, npost, nffn, tm):
    m = x2d.shape[0]

    def tile(w):
        return pl.BlockSpec((tm, w), lambda i: (i, 0))

    vec = pl.BlockSpec((1, D_MODEL), lambda i: (0, 0))
    return pl.pallas_call(
        _out_proj_body,
        grid=(m // tm,),
        in_specs=[tile(SSD_W), tile(POOL_W), tile(NSA_W), tile(D_MODEL),
                  pl.BlockSpec((1, D_MODEL, D_MODEL), lambda i: (layer, 0, 0)), vec, vec],
        out_specs=[tile(D_MODEL), tile(D_MODEL)],
        out_shape=[jax.ShapeDtypeStruct((m, D_MODEL), F32), jax.ShapeDtypeStruct((m, D_MODEL), BF16)],
        compiler_params=_cparams(("arbitrary",)),
        name="out_proj",
    )(ys, yp, yn, x2d, w_out, npost, nffn)


def _ffn_body(carry_mode, tiles_per_seq, t, tf, *refs):
    if carry_mode:
        (h2_ref, x1_ref, wg_ref, wv_ref, wd_ref, cw_ref, cb_ref, npost_ref,
         x2_ref, gsave_ref, act_ref, gext_ref, carry_ref) = refs
    else:
        (h2_ref, x1_ref, wg_ref, wv_ref, wd_ref, cw_ref, cb_ref, npost_ref, prev_ref,
         x2_ref, gsave_ref, act_ref) = refs
    tm = h2_ref.shape[0]
    h2 = h2_ref[...]
    if carry_mode:
        @pl.when(pl.program_id(0) % tiles_per_seq == 0)
        def _():
            carry_ref[...] = jnp.zeros(carry_ref.shape, F32)
    else:
        r = lax.broadcasted_iota(jnp.int32, (tm, tf), 0) % t
    for f in range(D_FF // tf):
        fs = slice(f * tf, (f + 1) * tf)
        g = _dot(h2, wg_ref[0, :, fs])
        v = _dot(h2, wv_ref[0, :, fs])
        if carry_mode:
            buf = f % 2
            gext_ref[buf, 0:8, :] = carry_ref[:, fs]
            gext_ref[buf, 8:8 + tm, :] = g
            carry_ref[:, fs] = g[tm - 8:tm]
            gsave_ref[0, :, fs] = g[tm - 8:tm]
            g1 = gext_ref[buf, 7:7 + tm, :]
            g2 = gext_ref[buf, 6:6 + tm, :]
        else:
            prev = prev_ref[:, fs]
            g1 = jnp.where(r >= 1, pltpu.roll(g, 1, 0), pltpu.roll(prev, tm - 1, 0))
            g2 = jnp.where(r >= 2, pltpu.roll(g, 2, 0), prev)
            gsave_ref[:, fs] = g
        cw = cw_ref[:, fs]
        gc = g * cw[2:3, :] + g1 * cw[1:2, :] + g2 * cw[0:1, :] + cb_ref[:, fs]
        act_ref[:, fs] = (_gelu_tanh(gc) * v).astype(BF16)
    out = _dot(act_ref[...], wd_ref[0])
    x2_ref[...] = x1_ref[...] + _rmsnorm(out, npost_ref[...])


def _ffn(h2, x1, wg, wv, wd, layer, cw, cb, npost, tm, tf, carry_mode, tiles_per_seq, t, prev_rows=None):
    m = h2.shape[0]
    nm = m // tm
    resident = pl.Buffered(1)
    in_specs = [
        pl.BlockSpec((tm, D_MODEL), lambda i: (i, 0)),
        pl.BlockSpec((tm, D_MODEL), lambda i: (i, 0)),
        pl.BlockSpec((1, D_MODEL, D_FF), lambda i: (layer, 0, 0), pipeline_mode=resident),
        pl.BlockSpec((1, D_MODEL, D_FF), lambda i: (layer, 0, 0), pipeline_mode=resident),
        pl.BlockSpec((1, D_FF, D_MODEL), lambda i: (layer, 0, 0), pipeline_mode=resident),
        pl.BlockSpec((FFN_CONV, D_FF), lambda i: (0, 0)),
        pl.BlockSpec((1, D_FF), lambda i: (0, 0)),
        pl.BlockSpec((1, D_MODEL), lambda i: (0, 0)),
    ]
    args = [h2, x1, wg, wv, wd, cw, cb, npost]
    scratch = [pltpu.VMEM((tm, D_FF), BF16)]
    if carry_mode:
        gsave_shape = jax.ShapeDtypeStruct((nm, 8, D_FF), F32)
        gsave_spec = pl.BlockSpec((1, 8, D_FF), lambda i: (i, 0, 0))
        scratch += [pltpu.VMEM((2, tm + 8, tf), F32), pltpu.VMEM((8, D_FF), F32)]
    else:
        gsave_shape = jax.ShapeDtypeStruct((m, D_FF), F32)
        gsave_spec = pl.BlockSpec((tm, D_FF), lambda i: (i, 0))
        in_specs.append(pl.BlockSpec((tm, D_FF), lambda i: (i, 0)))
        args.append(prev_rows)
    return pl.pallas_call(
        functools.partial(_ffn_body, carry_mode, tiles_per_seq, t, tf),
        grid=(nm,),
        in_specs=in_specs,
        out_specs=[pl.BlockSpec((tm, D_MODEL), lambda i: (i, 0)), gsave_spec],
        out_shape=[jax.ShapeDtypeStruct((m, D_MODEL), F32), gsave_shape],
        scratch_shapes=scratch,
        compiler_params=_cparams(("arbitrary",)),
        name="ffn",
    )(*args)


def _prep_params(w_in, ssd_conv_w, ssd_conv_b, ssd_dt_bias, ssd_a_log, ssd_d, ssd_norm, pool_w, pool_scale,
                 nsa_pe_k, nsa_pe_v, nsa_w1_k, nsa_w1_v, nsa_w2_k, nsa_w2_v, w_out, ffn_w_gate, ffn_w_val,
                 ffn_w_down):
    depth = w_in.shape[0]
    w_in_p = jnp.concatenate(
        [w_in[..., 0:OFF_XBC], w_in[..., OFF_DT:OFF_KV], w_in[..., OFF_XBC:OFF_DT], w_in[..., OFF_KV:IN_W],
         jnp.zeros((depth, D_MODEL, IN_W_PAD - IN_W), F32)], axis=-1).astype(BF16)
    pad_h = ((0, 0), (0, 128 - SSD_H))
    dtb = jnp.pad(ssd_dt_bias, pad_h)[:, None, :]
    alog = jnp.pad(ssd_a_log, pad_h)[:, None, :]
    dsk = jnp.repeat(ssd_d, SSD_HD, axis=1)[:, None, :]
    n_g = len(POOL_WINDOWS)
    eye = jnp.eye(n_g, dtype=F32)
    pw_bd = (pool_w[:, :, :, None, :] * eye[None, :, None, :, None]).reshape(depth, POOL_W, POOL_W).astype(BF16)
    zk = jnp.zeros((depth, CMP_STRIDE, NSA_HD, CMP_HID), F32)
    top = jnp.concatenate([nsa_w1_k[:, :CMP_STRIDE], zk, nsa_w1_k[:, CMP_STRIDE:], zk], axis=-1)
    bot = jnp.concatenate([zk, nsa_w1_v[:, :CMP_STRIDE], zk, nsa_w1_v[:, CMP_STRIDE:]], axis=-1)
    w1 = jnp.concatenate([top, bot], axis=2).astype(BF16).reshape(depth, CMP_STRIDE // 2, 256, 4 * CMP_HID)
    pe_a = jnp.concatenate([nsa_pe_k[:, :CMP_STRIDE], nsa_pe_v[:, :CMP_STRIDE]], axis=-1)
    pe_b = jnp.concatenate([nsa_pe_k[:, CMP_STRIDE:], nsa_pe_v[:, CMP_STRIDE:]], axis=-1)
    pe = jnp.concatenate([pe_a[:, :, None], pe_b[:, :, None], jnp.zeros((depth, CMP_STRIDE, 6, 128), F32)], axis=2)
    pe = pe.reshape(depth, CMP_STRIDE // 2, 2, 8, 128).transpose(0, 1, 3, 2, 4).reshape(depth, CMP_STRIDE // 2, 8, 256)
    zw = jnp.zeros((depth, CMP_HID, NSA_HD), F32)
    w2 = jnp.concatenate([jnp.concatenate([nsa_w2_k, zw], axis=-1), jnp.concatenate([zw, nsa_w2_v], axis=-1)],
                         axis=1).astype(BF16)
    return dict(w_in=w_in_p, cw=ssd_conv_w, cb=ssd_conv_b[:, None, :], dtb=dtb, alog=alog, dsk=dsk,
                nrm=ssd_norm[:, None, :], pw=pw_bd, ps=pool_scale[:, None, :], w1=w1, pe=pe, w2=w2,
                w_out=w_out.astype(BF16), wg=ffn_w_gate.astype(BF16), wv=ffn_w_val.astype(BF16),
                wd=ffn_w_down.astype(BF16))


def _mixer_weights(p, l):
    return (p['cw'][l], p['cb'][l], p['dtb'][l], p['alog'][l], p['dsk'][l], p['nrm'][l], p['pw'][l], p['ps'][l])


def kernel(x_prompt, x_sample, cache_nsa_kv, page_table, state_nsa_win, state_ssd_conv, state_ssm, state_pool, state_ffn_conv, norm_mix_pre, w_in, ssd_conv_w, ssd_conv_b, ssd_dt_bias, ssd_a_log, ssd_d, ssd_norm, pool_w, pool_scale, nsa_pe_k, nsa_pe_v, nsa_w1_k, nsa_w1_v, nsa_w2_k, nsa_w2_v, w_out, norm_mix_post, norm_ffn_pre, ffn_w_gate, ffn_w_val, ffn_conv_w, ffn_conv_b, ffn_w_down, norm_ffn_post):
    depth = w_in.shape[0]
    bp, tp, _ = x_prompt.shape
    bs, ts, _ = x_sample.shape
    n_phys = cache_nsa_kv.shape[1]
    n_pages = page_table.shape[1]
    past = n_pages * PAGE_SIZE
    assert tp % 512 == 0 and tp >= WINDOW and ts == 8 and state_nsa_win.shape[2] == WINDOW
    p = _prep_params(w_in, ssd_conv_w, ssd_conv_b, ssd_dt_bias, ssd_a_log, ssd_d, ssd_norm, pool_w, pool_scale,
                     nsa_pe_k, nsa_pe_v, nsa_w1_k, nsa_w1_v, nsa_w2_k, nsa_w2_v, w_out, ffn_w_gate, ffn_w_val,
                     ffn_w_down)
    ms = bs * ts
    tab_p = _rope_table(jnp.arange(tp, dtype=jnp.int32))
    tab_s = jnp.tile(_rope_table(past + jnp.arange(ts, dtype=jnp.int32)), (bs, 1))
    cache = jnp.transpose(cache_nsa_kv, (0, 1, 3, 4, 2)).reshape(depth * n_phys, 4 * NSA_HD, PAGE_SIZE)
    win_state = jnp.transpose(state_nsa_win, (0, 1, 3, 4, 2)).reshape(depth * bs, 2 * NSA_HD, WINDOW)

    tm_p = 512
    tm_f = 512
    tf = 512
    nc_p = tp // CHUNK
    zeros_c = jnp.zeros((bp, 8, SSD_CONV_DIM), F32)
    zeros_s = jnp.zeros((bp, SSD_H, SSD_HD, SSD_N), F32)
    zeros_p = jnp.zeros((bp, 16, POOL_W), F32)

    xp = x_prompt.reshape(bp * tp, D_MODEL)
    xs = x_sample.reshape(ms, D_MODEL)
    outs_p, outs_s = [], []
    for l in range(depth):
        nw = norm_mix_pre[l][None]
        npost = norm_mix_post[l][None]
        nffn = norm_ffn_pre[l][None]
        nfpost = norm_ffn_post[l][None]
        mw = _mixer_weights(p, l)
        fcw, fcb = ffn_conv_w[l], ffn_conv_b[l][None]

        y_ssd, y_pool, ssm_new, q, kvc, kvw, dtg, ctail, ptail = _ssd_pool(
            xp, zeros_c, zeros_s, zeros_p, mw, bp, nc_p, CHUNK, 0, 2, fused_in=(nw, p['w_in'], l, tab_p))
        ckv = _nsa_cmp(kvc, p['w1'][l], p['pe'][l], p['w2'][l], bp, tp)
        y_nsa = _nsa_attn(q, dtg, ckv, kvc, kvw, bp, tp)
        x1, h2 = _out_proj(y_ssd, y_pool, y_nsa, xp, p['w_out'], l, npost, nffn, tm_p)
        xp, gsave = _ffn(h2, x1, p['wg'], p['wv'], p['wd'], l, fcw, fcb, nfpost, tm_f, tf, True, tp // tm_f, tp)
        outs_p.append((
            kvc.reshape(bp, tp, 4, NSA_HD),
            kvw.reshape(bp, tp, 2, NSA_HD)[:, tp - WINDOW:],
            ctail[:, 8 - (SSD_CONV - 1):],
            ssm_new,
            ptail[:, 16 - POOL_KEEP:],
            gsave.reshape(bp, tp // tm_f, 8, D_FF)[:, -1, 8 - (FFN_CONV - 1):],
        ))

        z, xbc, u, q, kvc, kvw, dtg = _in_proj(xs, nw, p['w_in'], l, tab_s, ms)
        cprev = jnp.pad(state_ssd_conv[l], ((0, 0), (8 - (SSD_CONV - 1), 0), (0, 0)))
        pprev = jnp.pad(state_pool[l], ((0, 0), (16 - POOL_KEEP, 0), (0, 0)))
        y_ssd, y_pool, ssm_new = _ssd_pool((xbc, z, dtg, u), cprev, state_ssm[l], pprev, mw, bs, 1, ts, past, 4)
        y_nsa = _nsa_dec(page_table, cache, q, dtg, kvc, kvw, win_state, p['w1'][l], p['pe'][l], p['w2'][l],
                         l, n_phys, bs, ts)
        x1, h2 = _out_proj(y_ssd, y_pool, y_nsa, xs, p['w_out'], l, npost, nffn, ms)
        fprev = state_ffn_conv[l]
        fprev_rows = jnp.pad(fprev, ((0, 0), (0, ts - (FFN_CONV - 1)), (0, 0))).reshape(ms, D_FF)
        xs, gsave = _ffn(h2, x1, p['wg'], p['wv'], p['wd'], l, fcw, fcb, nfpost, ms, tf, False, 1, ts, fprev_rows)
        kvw3 = kvw.reshape(bs, ts, 2, NSA_HD)
        xbc3 = xbc.reshape(bs, ts, SSD_CONV_DIM)
        u3 = u.reshape(bs, ts, POOL_W)
        g3 = gsave.reshape(bs, ts, D_FF)
        outs_s.append((
            kvc.reshape(bs, ts, 4, NSA_HD),
            jnp.concatenate([state_nsa_win[l], kvw3], axis=1)[:, -WINDOW:],
            jnp.concatenate([state_ssd_conv[l], xbc3], axis=1)[:, -(SSD_CONV - 1):],
            ssm_new,
            jnp.concatenate([state_pool[l], u3], axis=1)[:, -POOL_KEEP:],
            jnp.concatenate([state_ffn_conv[l], g3], axis=1)[:, -(FFN_CONV - 1):],
        ))

    def stk(outs, i):
        return jnp.stack([o[i] for o in outs])

    return (xp.reshape(bp, tp, D_MODEL), xs.reshape(bs, ts, D_MODEL), stk(outs_p, 0), stk(outs_s, 0),
            stk(outs_p, 1), stk(outs_s, 1), stk(outs_p, 2), stk(outs_s, 2), stk(outs_p, 3), stk(outs_s, 3),
            stk(outs_p, 4), stk(outs_s, 4), stk(outs_p, 5), stk(outs_s, 5))
```

```python
import functools
import math

import jax
import jax.numpy as jnp
from jax import lax
from jax.experimental import pallas as pl
from jax.experimental.pallas import tpu as pltpu

F32 = jnp.float32
BF16 = jnp.bfloat16
HIGHEST = lax.Precision.HIGHEST

D_MODEL = 1024
PAGE_SIZE = 128
SSD_W = 512
SSD_HD = 64
SSD_H = 8
SSD_N = 128
SSD_G = 2
SSD_CONV = 4
SSD_CONV_DIM = 1024
POOL_W = 256
POOL_WINDOWS = (2, 4, 8, 16)
POOL_GC = 64
POOL_KEEP = 15
NSA_W = 256
NSA_HD = 64
NSA_H = 4
CMP_STRIDE = 16
CMP_LEN = 32
CMP_HID = 128
SLC_BLOCK = 64
N_SELECT = 16
WINDOW = 512
ROPE_DIM = 16
ROPE_THETA = 500000.0
D_FF = 4096
FFN_CONV = 3
RMS_EPS = 1e-6
OFF_Z = 512
OFF_XBC = 1536
OFF_DT = 1544
OFF_POOL = 1800
OFF_Q = 2056
OFF_KV = 2440
IN_W = 2452
IN_W_PAD = 2560
CHUNK = 128
NEG = -1e30
LOG2E = 1.4426950408889634
VMEM_LIMIT = 56 * 1024 * 1024


def _cparams(sem):
    return pltpu.CompilerParams(dimension_semantics=sem, vmem_limit_bytes=VMEM_LIMIT)


def _rmsnorm(x, w):
    return x * lax.rsqrt(jnp.mean(x * x, axis=-1, keepdims=True) + RMS_EPS) * w


def _silu(x):
    return x / (1.0 + jnp.exp(-x))


def _gelu_tanh(x):
    k = 0.7978845608028654
    hx = 0.5 * x
    return hx + hx * jnp.tanh(x * (k + (k * 0.044715) * (x * x)))


def _dot(a, b):
    return jnp.dot(a, b, preferred_element_type=F32)


def _dot_nt(a, b):
    return lax.dot_general(a, b, (((1,), (1,)), ((), ())), preferred_element_type=F32)


def _split3(x):
    p0 = x.astype(BF16)
    r = x - p0.astype(F32)
    p1 = r.astype(BF16)
    return p0, p1, (r - p1.astype(F32)).astype(BF16)


def _dot_sel(parts, sel):
    return _dot(parts[0], sel) + _dot(parts[1], sel) + _dot(parts[2], sel)


def _in_proj_rows(x, nw, w_ref, tab_ref):
    r = x.shape[0]
    h = _rmsnorm(x, nw).astype(BF16)

    def proj(a, b):
        return _dot(h, w_ref[0, :, a:b])

    lane = lax.broadcasted_iota(jnp.int32, (r, 128), 1)
    first = (lane % NSA_HD) < (ROPE_DIM // 2)

    def rope(v, c, s):
        outs = []
        for j in range(v.shape[1] // 128):
            vj = v[:, j * 128:(j + 1) * 128]
            rot = jnp.where(first, pltpu.roll(vj, 128 - ROPE_DIM // 2, 1), pltpu.roll(vj, ROPE_DIM // 2, 1))
            outs.append(vj * c + rot * s)
        return outs

    q = rope(proj(1792, 2048), tab_ref[:, 0:128], tab_ref[:, 128:256])
    kv = rope(proj(2048, 2432), tab_ref[:, 256:384], tab_ref[:, 384:512])
    return (proj(0, 512), proj(512, 1536), proj(1536, 1792), jnp.concatenate(q, axis=1),
            jnp.concatenate(kv[0:2], axis=1), kv[2], proj(2432, 2560))


def _in_proj_body(x_ref, nw_ref, w_ref, tab_ref, z_ref, xbc_ref, pool_ref, q_ref, kvc_ref, kvw_ref, dtg_ref):
    outs = _in_proj_rows(x_ref[...], nw_ref[...], w_ref, tab_ref)
    for o_ref, o in zip((z_ref, xbc_ref, pool_ref, q_ref, kvc_ref, kvw_ref, dtg_ref), outs):
        o_ref[...] = o


def _in_proj(x2d, nw, w_in_p, layer, tab, tm):
    m = x2d.shape[0]
    tab_blocks = tab.shape[0] // tm
    widths = (512, 1024, 256, 256, 256, 128, 128)
    return pl.pallas_call(
        _in_proj_body,
        grid=(m // tm,),
        in_specs=[
            pl.BlockSpec((tm, D_MODEL), lambda i: (i, 0)),
            pl.BlockSpec((1, D_MODEL), lambda i: (0, 0)),
            pl.BlockSpec((1, D_MODEL, IN_W_PAD), lambda i: (layer, 0, 0)),
            pl.BlockSpec((tm, 512), lambda i: (i % tab_blocks, 0)),
        ],
        out_specs=[pl.BlockSpec((tm, w), lambda i: (i, 0)) for w in widths],
        out_shape=[jax.ShapeDtypeStruct((m, w), F32) for w in widths],
        compiler_params=_cparams(("arbitrary",)),
        name="in_proj",
    )(x2d, nw, w_in_p, tab)


def _rope_table(pos):
    half = ROPE_DIM // 2
    inv = 1.0 / (ROPE_THETA ** (jnp.arange(half, dtype=F32) / half))
    ang = pos.astype(F32)[:, None] * inv
    cos, sin = jnp.cos(ang), jnp.sin(ang)
    t = pos.shape[0]
    c_rot = jnp.concatenate([cos, cos, jnp.ones((t, NSA_HD - ROPE_DIM), F32)], axis=1)
    s_rot = jnp.concatenate([-sin, sin, jnp.zeros((t, NSA_HD - ROPE_DIM), F32)], axis=1)
    one, zero = jnp.ones((t, NSA_HD), F32), jnp.zeros((t, NSA_HD), F32)
    return jnp.concatenate([c_rot, c_rot, s_rot, s_rot, c_rot, one, s_rot, zero], axis=1)


def _ssd_pool_body(lr, nc, pos0, nsub, fused, *refs):
    if fused:
        (x_ref, nw_ref, win_ref, tab_ref, cprev_ref, sprev_ref, pprev_ref,
         cw_ref, cb_ref, dtb_ref, alog_ref, dsk_ref, nrm_ref, pw_ref, ps_ref,
         yssd_ref, ypool_ref, snew_ref, q_ref, kvc_ref, kvw_ref, dtg_ref, ctail_ref, ptail_ref,
         ext_ref, state_ref, pext_ref) = refs
    else:
        (xbc_ref, z_ref, dtg_ref, u_ref, cprev_ref, sprev_ref, pprev_ref,
         cw_ref, cb_ref, dtb_ref, alog_ref, dsk_ref, nrm_ref, pw_ref, ps_ref,
         yssd_ref, ypool_ref, snew_ref, ext_ref, state_ref, pext_ref) = refs
    c = pl.program_id(1)
    L = CHUNK

    @pl.when(c == 0)
    def _():
        ext_ref[:, 0:8, :] = cprev_ref[...]
        pext_ref[:, 0:16, :] = pprev_ref[...]
        state_ref[...] = sprev_ref[...]

    if nc > 1:
        @pl.when(c > 0)
        def _():
            ext_ref[:, 0:8, :] = ext_ref[:, L:L + 8, :]
            pext_ref[:, 0:16, :] = pext_ref[:, L:L + 16, :]

    z_rows, dt_rows = [], []
    for sub in range(nsub):
        if fused:
            z, xbc, u, q, kvc, kvw, dtg = _in_proj_rows(x_ref[sub], nw_ref[...], win_ref, tab_ref)
            ext_ref[sub, 8:8 + lr, :] = xbc
            pext_ref[sub, 16:16 + lr, :] = u
            q_ref[sub] = q
            kvc_ref[sub] = kvc
            kvw_ref[sub] = kvw
            dtg_ref[sub] = dtg
        else:
            z, dtg = z_ref[sub], dtg_ref[sub]
            ext_ref[sub, 8:8 + lr, :] = xbc_ref[sub]
            pext_ref[sub, 16:16 + lr, :] = u_ref[sub]
        z_rows.append(z)
        dt_rows.append(dtg)
    if lr < L:
        ext_ref[:, 8 + lr:8 + L, :] = jnp.zeros((nsub, L - lr, SSD_CONV_DIM), F32)
        pext_ref[:, 16 + lr:16 + L, :] = jnp.zeros((nsub, L - lr, POOL_W), F32)

    cw = cw_ref[...]
    row = lax.broadcasted_iota(jnp.int32, (L, L), 0)
    col = lax.broadcasted_iota(jnp.int32, (L, L), 1)
    tri = row >= col
    tri_b = tri.astype(F32).astype(BF16)
    eye_b = (row == col).astype(F32).astype(BF16)
    gw = (SSD_H // SSD_G) * SSD_HD
    expand = (lax.broadcasted_iota(jnp.int32, (128, SSD_W), 0)
              == lax.broadcasted_iota(jnp.int32, (128, SSD_W), 1) // SSD_HD).astype(F32).astype(BF16)
    neg_a = -jnp.exp(alog_ref[...])
    dsk = dsk_ref[...]
    prow = lax.broadcasted_iota(jnp.int32, (L, POOL_W), 0)
    plane = lax.broadcasted_iota(jnp.int32, (L, POOL_W), 1)
    pos = pos0 + c * L + prow + 1

    for sub in range(nsub):
        conv = (ext_ref[sub, 8:8 + L, :] * cw[3:4, :] + ext_ref[sub, 7:7 + L, :] * cw[2:3, :]
                + ext_ref[sub, 6:6 + L, :] * cw[1:2, :] + ext_ref[sub, 5:5 + L, :] * cw[0:1, :] + cb_ref[...])
        xbc = _silu(conv)
        xs = xbc[:, 0:SSD_W]
        bmat = [xbc[:, SSD_W + g * SSD_N:SSD_W + (g + 1) * SSD_N].astype(BF16) for g in range(SSD_G)]
        cmat = [xbc[:, SSD_W + (SSD_G + g) * SSD_N:SSD_W + (SSD_G + g + 1) * SSD_N].astype(BF16)
                for g in range(SSD_G)]

        if lr == L:
            dtraw = dt_rows[sub]
        else:
            dtraw = jnp.concatenate([dt_rows[sub], jnp.zeros((L - lr, 128), F32)], axis=0)
        xdt = dtraw + dtb_ref[...]
        dt = jnp.maximum(xdt, 0.0) + jnp.log1p(jnp.exp(-jnp.abs(xdt)))
        dt = jnp.where((col < SSD_H) & (row < lr), dt, 0.0)
        da3 = _split3(dt * neg_a)
        acum = _dot(tri_b, da3[0]) + _dot(tri_b, da3[1]) + _dot(tri_b, da3[2])
        ac3 = _split3(acum)
        arow = _dot_nt(eye_b, ac3[0]) + _dot_nt(eye_b, ac3[1]) + _dot_nt(eye_b, ac3[2])
        dt_x = _dot_sel(_split3(dt), expand)
        ac_x = _dot_sel(ac3, expand)
        alast_x = ac_x[L - 1:L, :]
        xdt = xs * dt_x
        xdt_b = xdt.astype(BF16)
        xw = (xdt * jnp.exp(alast_x - ac_x)).astype(BF16)
        cb_scores = [_dot_nt(cmat[g], bmat[g]) for g in range(SSD_G)]
        y_off = []
        for g in range(SSD_G):
            gs = slice(g * gw, (g + 1) * gw)
            st = state_ref[sub, g]
            y_off.append(_dot(cmat[g], st.astype(BF16)))
            b_t = _dot_nt(eye_b, bmat[g]).astype(BF16)
            state_ref[sub, g] = jnp.exp(alast_x[:, gs]) * st + _dot(b_t, xw[:, gs])
        y_diag = []
        for hp in range(SSD_H // 2):
            tile_b = xdt_b[:, hp * 128:(hp + 1) * 128]
            acc = None
            for h in (2 * hp, 2 * hp + 1):
                g = h // (SSD_H // SSD_G)
                decay = jnp.exp(jnp.where(tri, acum[:, h:h + 1] - arow[h:h + 1, :], NEG))
                own = (col < SSD_HD) if h % 2 == 0 else (col >= SSD_HD)
                part = _dot((cb_scores[g] * decay).astype(BF16), jnp.where(own, tile_b, jnp.zeros_like(tile_b)))
                acc = part if acc is None else acc + part
            y_diag.append(acc)
        y = jnp.concatenate(y_diag, axis=1) + jnp.concatenate(y_off, axis=1) * jnp.exp(ac_x) + dsk * xs
        if lr < L:
            y = y[0:lr]
        y = y * _silu(z_rows[sub])
        yssd_ref[sub] = _rmsnorm(y, nrm_ref[...]).astype(BF16)

        e1 = pext_ref[sub]
        s2 = e1 + pltpu.roll(e1, 1, 0)
        s4 = s2 + pltpu.roll(s2, 2, 0)
        s8 = s4 + pltpu.roll(s4, 4, 0)
        s16 = s8 + pltpu.roll(s8, 8, 0)
        pooled = jnp.zeros((L, POOL_W), F32)
        for gi, (w, sw) in enumerate(zip(POOL_WINDOWS, (s2, s4, s8, s16))):
            cnt = jnp.minimum(pos, w).astype(F32)
            grp = (plane >= gi * POOL_GC) & (plane < (gi + 1) * POOL_GC)
            pooled = jnp.where(grp, sw[16:16 + L] / cnt, pooled)
        pooled = pooled - e1[16:16 + L]
        yp = _dot(pooled.astype(BF16), pw_ref[...]) * ps_ref[...]
        ypool_ref[sub] = (yp[0:lr] if lr < L else yp).astype(BF16)

    @pl.when(c == nc - 1)
    def _():
        snew_ref[...] = state_ref[...]
        if fused:
            ctail_ref[...] = ext_ref[:, L:L + 8, :]
            ptail_ref[...] = pext_ref[:, L:L + 16, :]


def _ssd_pool(acts, cprev, sprev, pprev, wts, nb, nc, lr, pos0, nsub, fused_in=None):
    cw, cb, dtb, alog, dsk, nrm, pw, ps = wts
    t = nc * lr
    assert nb % nsub == 0

    def tile(w):
        return pl.BlockSpec((nsub, lr, w), lambda b, c: (b, c, 0))

    def full2(a):
        return pl.BlockSpec(a.shape, lambda b, c: (0, 0))

    def rows(w, dtype=F32):
        return jax.ShapeDtypeStruct((nb, t, w), dtype)

    hg = SSD_H // SSD_G
    st_shape = (SSD_G, SSD_N, hg * SSD_HD)
    sprev_t = sprev.reshape(nb, SSD_G, hg, SSD_HD, SSD_N).transpose(0, 1, 4, 2, 3).reshape((nb,) + st_shape)
    state_spec = pl.BlockSpec((nsub,) + st_shape, lambda b, c: (b, 0, 0, 0))
    prev_specs = [pl.BlockSpec((nsub, 8, SSD_CONV_DIM), lambda b, c: (b, 0, 0)), state_spec,
                  pl.BlockSpec((nsub, 16, POOL_W), lambda b, c: (b, 0, 0))]
    w_specs = [full2(cw), full2(cb), full2(dtb), full2(alog), full2(dsk), full2(nrm), full2(pw), full2(ps)]
    out_specs = [tile(SSD_W), tile(POOL_W), state_spec]
    out_shape = [rows(SSD_W, BF16), rows(POOL_W, BF16), jax.ShapeDtypeStruct((nb,) + st_shape, F32)]
    if fused_in is None:
        xbc, z, dtg, u = acts
        in_specs = [tile(SSD_CONV_DIM), tile(SSD_W), tile(128), tile(POOL_W)]
        args = [xbc.reshape(nb, t, SSD_CONV_DIM), z.reshape(nb, t, SSD_W), dtg.reshape(nb, t, 128),
                u.reshape(nb, t, POOL_W)]
    else:
        nw, w_in_p, layer, tab = fused_in
        assert lr == CHUNK
        in_specs = [tile(D_MODEL), pl.BlockSpec((1, D_MODEL), lambda b, c: (0, 0)),
                    pl.BlockSpec((1, D_MODEL, IN_W_PAD), lambda b, c: (layer, 0, 0), pipeline_mode=pl.Buffered(1)),
                    pl.BlockSpec((lr, 512), lambda b, c: (c, 0))]
        args = [acts.reshape(nb, t, D_MODEL), nw, w_in_p, tab]
        out_specs += [tile(NSA_W), tile(4 * NSA_HD), tile(2 * NSA_HD), tile(128),
                      pl.BlockSpec((nsub, 8, SSD_CONV_DIM), lambda b, c: (b, 0, 0)),
                      pl.BlockSpec((nsub, 16, POOL_W), lambda b, c: (b, 0, 0))]
        out_shape += [rows(NSA_W), rows(4 * NSA_HD), rows(2 * NSA_HD), rows(128),
                      jax.ShapeDtypeStruct((nb, 8, SSD_CONV_DIM), F32), jax.ShapeDtypeStruct((nb, 16, POOL_W), F32)]
    outs = pl.pallas_call(
        functools.partial(_ssd_pool_body, lr, nc, pos0, nsub, fused_in is not None),
        grid=(nb // nsub, nc),
        in_specs=in_specs + prev_specs + w_specs,
        out_specs=out_specs,
        out_shape=out_shape,
        scratch_shapes=[pltpu.VMEM((nsub, CHUNK + 8, SSD_CONV_DIM), F32),
                        pltpu.VMEM((nsub,) + st_shape, F32),
                        pltpu.VMEM((nsub, CHUNK + 16, POOL_W), F32)],
        compiler_params=_cparams(("arbitrary", "arbitrary")),
        name="ssd_pool",
    )(*args, cprev, sprev_t, pprev, cw, cb, dtb, alog, dsk, nrm, pw, ps)
    yssd, ypool, snew_t = outs[:3]
    snew = snew_t.reshape(nb, SSD_G, SSD_N, hg, SSD_HD).transpose(0, 1, 3, 4, 2).reshape(nb, SSD_H, SSD_HD, SSD_N)
    flat = [o.reshape(nb * t, o.shape[-1]) for o in outs[3:7]]
    return (yssd.reshape(nb * t, SSD_W), ypool.reshape(nb * t, POOL_W), snew) + tuple(flat) + tuple(outs[7:])


def _compress(load_rows, n_rows, w1_ref, pe_ref, w2_ref, pq_ref):
    acc = jnp.zeros((n_rows, 4 * CMP_HID), F32)
    cacc = jnp.zeros((8, 4 * CMP_HID), F32)
    for l2 in range(CMP_STRIDE // 2):
        w = w1_ref[l2]
        rows = jnp.concatenate([load_rows(2 * l2), load_rows(2 * l2 + 1)], axis=1)
        acc = acc + _dot(rows.astype(BF16), w)
        cacc = cacc + _dot(pe_ref[l2].astype(BF16), w)
    pq_ref[0:n_rows, :] = acc
    pq_ref[n_rows:n_rows + 8, :] = jnp.zeros((8, 4 * CMP_HID), F32)
    const = cacc[0:1, 0:2 * CMP_HID] + cacc[1:2, 2 * CMP_HID:4 * CMP_HID]
    hid = pq_ref[0:n_rows, 0:2 * CMP_HID] + pq_ref[1:n_rows + 1, 2 * CMP_HID:4 * CMP_HID] + const
    return _dot(_gelu_tanh(hid).astype(BF16), w2_ref[...])


def _stack_heads(x):
    return jnp.concatenate([x[:, h * NSA_HD:(h + 1) * NSA_HD] for h in range(NSA_H)], axis=0)


def _gated_mix(dtg, o_cmp, o_slc, o_win, r):
    sig = 1.0 / (1.0 + jnp.exp(-dtg))
    outs = []
    for h in range(NSA_H):
        b = SSD_H + 3 * h
        sl = slice(h * r, (h + 1) * r)
        outs.append(sig[:, b:b + 1] * o_cmp[sl] + sig[:, b + 1:b + 2] * o_slc[sl] + sig[:, b + 2:b + 3] * o_win[sl])
    return jnp.concatenate(outs, axis=1)


def _nsa_cmp_body(kv_ref, w1_ref, pe_ref, w2_ref, ckv_ref, pq_ref):
    n_ch = kv_ref.shape[0] // CMP_STRIDE
    ckv_ref[...] = _compress(lambda l: kv_ref[pl.ds(l, n_ch, stride=CMP_STRIDE), :], n_ch, w1_ref, pe_ref, w2_ref,
                             pq_ref)


def _nsa_cmp(kvc, w1, pe, w2, nb, t):
    n_ch = t // CMP_STRIDE
    return pl.pallas_call(
        _nsa_cmp_body,
        grid=(nb,),
        in_specs=[
            pl.BlockSpec((t, 128), lambda b: (b, 0)),
            pl.BlockSpec(w1.shape, lambda b: (0, 0, 0)),
            pl.BlockSpec(pe.shape, lambda b: (0, 0, 0)),
            pl.BlockSpec(w2.shape, lambda b: (0, 0)),
        ],
        out_specs=pl.BlockSpec((n_ch, 128), lambda b: (b, 0)),
        out_shape=jax.ShapeDtypeStruct((nb * n_ch, 128), F32),
        scratch_shapes=[pltpu.VMEM((n_ch + 8, 4 * CMP_HID), F32)],
        compiler_params=_cparams(("arbitrary",)),
        name="nsa_cmp",
    )(kvc, w1, pe, w2)


def _flash_step_t(carry, st, bias, vt):
    m, l, acc = carry
    sm = st + bias
    m_new = jnp.maximum(m, jnp.max(sm, axis=0, keepdims=True))
    p = jnp.exp2(sm - m_new)
    alpha = jnp.exp2(m - m_new)
    l = alpha * l + jnp.sum(p, axis=0, keepdims=True)
    acc = alpha * acc + _dot(vt, p.astype(BF16))
    return m_new, l, acc


def _flash_init_t(cols):
    return jnp.full((1, cols), NEG, F32), jnp.zeros((1, cols), F32), jnp.zeros((NSA_HD, cols), F32)


def _nsa_attn_body(t, q_ref, dtg_ref, ckv_ref, kvs_ref, kvw_ref, y_ref, ks_ref, kst_ref, kw_ref, kwt_ref, selt_ref):
    s = pl.program_id(1)
    qb = CHUNK
    cols = NSA_H * qb
    n_kc = t // qb
    n_ch = t // CMP_STRIDE
    n_cmp = n_ch - 1
    n_slc = t // SLC_BLOCK
    nb_pad = -(-n_slc // 8) * 8
    n_sel = min(N_SELECT, n_slc)

    kb = ks_ref.shape[1] // qb
    n_wc = WINDOW // qb + 1

    @pl.when(s == 0)
    def _():
        for c in range(n_kc):
            tile = kvs_ref[c * qb:(c + 1) * qb, :]
            ks_ref[c // kb, (c % kb) * qb:(c % kb + 1) * qb, :] = tile.astype(BF16)
            kst_ref[c // kb, :, (c % kb) * qb:(c % kb + 1) * qb] = tile.T.astype(BF16)
            tile = kvw_ref[c * qb:(c + 1) * qb, :]
            kw_ref[c] = tile.astype(BF16)
            kwt_ref[c] = tile.T.astype(BF16)

    q_t = (q_ref[...] * (NSA_HD ** -0.5 * LOG2E)).T
    qs_t = jnp.concatenate([q_t[h * NSA_HD:(h + 1) * NSA_HD, :] for h in range(NSA_H)], axis=1).astype(BF16)
    lane = lax.broadcasted_iota(jnp.int32, (1, cols), 1)
    tpos = s * qb + lane % qb
    tpos_h = tpos[:, 0:qb]

    k0 = jnp.maximum(s - (n_wc - 1), 0)
    kw = jnp.concatenate([kw_ref[k0 + i, :, 0:NSA_HD] for i in range(n_wc)], axis=0)
    vwt = jnp.concatenate([kwt_ref[k0 + i, NSA_HD:2 * NSA_HD, :] for i in range(n_wc)], axis=1)
    kpos = k0 * qb + lax.broadcasted_iota(jnp.int32, (n_wc * qb, qb), 0)
    wbias = jnp.where((kpos <= tpos_h) & (kpos > tpos_h - WINDOW), 0.0, NEG)
    sw = _dot(kw, qs_t) + jnp.concatenate([wbias] * NSA_H, axis=1)
    pw = jnp.exp2(sw - jnp.max(sw, axis=0, keepdims=True))
    o_win = _dot(vwt, pw.astype(BF16)) / jnp.sum(pw, axis=0, keepdims=True)
    sig = 1.0 / (1.0 + jnp.exp(-dtg_ref[...].T[0:32, :]))

    ckv = ckv_ref[...]
    ck = ckv[:, 0:NSA_HD].astype(BF16)
    cv_t = ckv.T[NSA_HD:2 * NSA_HD, :].astype(BF16)
    sc = _dot(ck, qs_t)
    nrow = lax.broadcasted_iota(jnp.int32, (n_ch, cols), 0)
    cmask = (nrow * CMP_STRIDE + (CMP_LEN - 1) <= tpos) & (nrow < n_cmp)
    mx = jnp.max(jnp.where(cmask, sc, NEG), axis=0, keepdims=True)
    e = jnp.where(cmask, jnp.exp2(sc - mx), 0.0)
    pc = e / jnp.maximum(jnp.sum(e, axis=0, keepdims=True), 1e-30)
    o_cmp = _dot(cv_t, pc.astype(BF16))
    pcsum = pc[:, 0:qb] + pc[:, qb:2 * qb] + pc[:, 2 * qb:3 * qb] + pc[:, 3 * qb:4 * qb]

    jrow = lax.broadcasted_iota(jnp.int32, (nb_pad, n_ch), 0)
    ncol = lax.broadcasted_iota(jnp.int32, (nb_pad, n_ch), 1)
    ov_t = ((ncol * CMP_STRIDE < (jrow + 1) * SLC_BLOCK) & (ncol * CMP_STRIDE + CMP_LEN > jrow * SLC_BLOCK)
            & (ncol < n_cmp)).astype(F32)
    imp = jnp.dot(ov_t, pcsum, precision=HIGHEST, preferred_element_type=F32)
    brow = lax.broadcasted_iota(jnp.int32, (nb_pad, qb), 0)
    tq = s * qb + lax.broadcasted_iota(jnp.int32, (nb_pad, qb), 1)
    cur = tq // SLC_BLOCK
    forced = (brow == 0) | (brow == cur) | (brow == cur - 1)
    imp = jnp.where(forced, jnp.inf, imp)
    imp = jnp.where((brow * SLC_BLOCK <= tq) & (brow < n_slc), imp, -jnp.inf)
    rank = jnp.zeros((nb_pad, qb), F32)
    for i in range(n_slc):
        ri = imp[i:i + 1, :]
        before = (ri > imp) | ((ri == imp) & (brow > i))
        rank = rank + before.astype(F32)
    selt_ref[...] = jnp.where((rank < n_sel) & (imp > -jnp.inf), 0.0, NEG)

    bps = kb * qb // SLC_BLOCK
    krow_s = lax.broadcasted_iota(jnp.int32, (kb * qb, qb), 0)

    def slc_step(c, carry):
        k = ks_ref[c, :, 0:NSA_HD]
        vt = kst_ref[c, NSA_HD:2 * NSA_HD, :]
        sel_rows = selt_ref[pl.ds(pl.multiple_of(c * bps, bps), bps), :]
        bias = jnp.concatenate([jnp.broadcast_to(sel_rows[j:j + 1, :], (SLC_BLOCK, qb)) for j in range(bps)],
                               axis=0)
        bias = jnp.where(c * (kb * qb) + krow_s <= tpos_h, bias, NEG)
        return _flash_step_t(carry, _dot(k, qs_t), jnp.concatenate([bias] * NSA_H, axis=1), vt)

    carry = slc_step(0, _flash_init_t(cols))
    for c in range(1, n_kc // kb):
        carry = lax.cond(s >= c * kb, functools.partial(slc_step, c), lambda cr: cr, carry)
    _, l_s, acc_s = carry
    o_slc = acc_s / l_s

    outs = []
    for h in range(NSA_H):
        b = SSD_H + 3 * h
        sl = slice(h * qb, (h + 1) * qb)
        outs.append(sig[b:b + 1, :] * o_cmp[:, sl] + sig[b + 1:b + 2, :] * o_slc[:, sl]
                    + sig[b + 2:b + 3, :] * o_win[:, sl])
    y_ref[...] = jnp.concatenate(outs, axis=0).T.astype(BF16)


def _nsa_attn(q, dtg, ckv, kvc, kvw, nb, t):
    nq = t // CHUNK
    n_ch = t // CMP_STRIDE
    kb = 8
    assert n_ch % 128 == 0 and nq % kb == 0 and nq > WINDOW // CHUNK
    nb_pad = -(-(t // SLC_BLOCK) // 8) * 8
    kv_scratch = pltpu.VMEM((nq, CHUNK, 2 * NSA_HD), BF16)
    return pl.pallas_call(
        functools.partial(_nsa_attn_body, t),
        grid=(nb, nq),
        in_specs=[
            pl.BlockSpec((CHUNK, NSA_W), lambda b, s: (b * nq + s, 0)),
            pl.BlockSpec((CHUNK, 128), lambda b, s: (b * nq + s, 0)),
            pl.BlockSpec((n_ch, 128), lambda b, s: (b, 0)),
            pl.BlockSpec((t, 128), lambda b, s: (b, 1)),
            pl.BlockSpec((t, 128), lambda b, s: (b, 0)),
        ],
        out_specs=pl.BlockSpec((CHUNK, NSA_W), lambda b, s: (b * nq + s, 0)),
        out_shape=jax.ShapeDtypeStruct((nb * t, NSA_W), BF16),
        scratch_shapes=[pltpu.VMEM((nq // kb, kb * CHUNK, 2 * NSA_HD), BF16),
                        pltpu.VMEM((nq // kb, 2 * NSA_HD, kb * CHUNK), BF16),
                        kv_scratch, kv_scratch, pltpu.VMEM((nb_pad, CHUNK), F32)],
        compiler_params=_cparams(("arbitrary", "arbitrary")),
        name="nsa_attn",
    )(q, dtg, ckv, kvc, kvw)


def _softmax_pv_nt(sc, mask, vt):
    mx = jnp.max(jnp.where(mask, sc, NEG), axis=1, keepdims=True)
    e = jnp.where(mask, jnp.exp(sc - mx), 0.0)
    l = jnp.sum(e, axis=1, keepdims=True)
    return _dot_nt(e.astype(BF16), vt) / jnp.maximum(l, 1e-30)


def _rows_to_cols(x):
    r, w = x.shape
    return jnp.concatenate([x, jnp.zeros((128 - r, w), F32)], axis=0).T


def _nsa_dec_body(layer, n_phys, n_pages, nb, t, pt_ref, cache_ref, q_ref, dtg_ref, kvn_ref, kwn_ref, win_ref,
                  w1_ref, pe_ref, w2_ref, y_ref, xt_ref, xc_ref, pq_ref, wk_ref, sc_ref, sem):
    b = pl.program_id(0)
    past = n_pages * PAGE_SIZE
    lk = past + t
    n_ch = -(-lk // CMP_STRIDE)
    n_cmp = n_ch - 1
    xrows = xt_ref.shape[1] * PAGE_SIZE
    nch_pad = xrows // CMP_STRIDE
    n_slc = -(-lk // SLC_BLOCK)
    nblk = -(-(xrows // SLC_BLOCK) // 128) * 128
    n_sel = min(N_SELECT, n_slc)
    slot = b % 2

    def page_copy(bb, sl, p):
        phys = pt_ref[bb, p] + layer * n_phys
        return pltpu.make_async_copy(cache_ref.at[phys], xt_ref.at[sl, p], sem.at[sl])

    def start_all(bb, sl):
        def body(p, carry):
            page_copy(bb, sl, p).start()
            return carry
        lax.fori_loop(0, n_pages, body, 0, unroll=8)

    @pl.when(b == 0)
    def _():
        start_all(0, 0)

    @pl.when(b + 1 < nb)
    def _():
        start_all(b + 1, 1 - slot)

    def wait_body(p, carry):
        page_copy(b, slot, p).wait()
        return carry
    lax.fori_loop(0, n_pages, wait_body, 0, unroll=8)

    xt_ref[slot, n_pages] = _rows_to_cols(kvn_ref[...])

    rows = NSA_H * t
    qs = _stack_heads(q_ref[...] * (NSA_HD ** -0.5)).astype(BF16)
    ti = lax.broadcasted_iota(jnp.int32, (rows, 1), 0) % t
    tpos = past + ti

    wrows = wk_ref.shape[1]
    wk_ref[:, 0:WINDOW] = win_ref[0]
    wk_ref[:, WINDOW:wrows] = _rows_to_cols(kwn_ref[...])
    kwt = wk_ref[0:NSA_HD, :].astype(BF16)
    vwt = wk_ref[NSA_HD:2 * NSA_HD, :].astype(BF16)
    wpos = past - WINDOW + lax.broadcasted_iota(jnp.int32, (rows, wrows), 1)
    wmask = (wpos >= 0) & (wpos <= tpos) & (wpos > tpos - WINDOW)
    o_win = _softmax_pv_nt(_dot(qs, kwt), wmask, vwt)

    for p in range(n_pages + 1):
        xc_ref[p * PAGE_SIZE:(p + 1) * PAGE_SIZE, :] = xt_ref[slot, p, 0:128, :].T
        sc_ref[:, p * PAGE_SIZE:(p + 1) * PAGE_SIZE] = _dot(qs, xt_ref[slot, p, 2 * NSA_HD:3 * NSA_HD, :].astype(BF16))

    ckv = _compress(lambda l: xc_ref[pl.ds(l, nch_pad, stride=CMP_STRIDE), :], nch_pad, w1_ref, pe_ref,
                    w2_ref, pq_ref)
    nc_use = (n_cmp + 127) // 128 * 128
    ck = ckv[0:nc_use, 0:NSA_HD].astype(BF16)
    cv = ckv[0:nc_use, NSA_HD:2 * NSA_HD].astype(BF16)
    sc = _dot_nt(qs, ck)
    ncol = lax.broadcasted_iota(jnp.int32, (rows, nc_use), 1)
    cmask = (ncol * CMP_STRIDE + (CMP_LEN - 1) <= tpos) & (ncol < n_cmp)
    mx = jnp.max(jnp.where(cmask, sc, NEG), axis=1, keepdims=True)
    e = jnp.where(cmask, jnp.exp(sc - mx), 0.0)
    pc = e / jnp.maximum(jnp.sum(e, axis=1, keepdims=True), 1e-30)
    o_cmp = _dot(pc.astype(BF16), cv)
    pcsum = pc[0:t] + pc[t:2 * t] + pc[2 * t:3 * t] + pc[3 * t:4 * t]

    nrow = lax.broadcasted_iota(jnp.int32, (nc_use, nblk), 0)
    jcol = lax.broadcasted_iota(jnp.int32, (nc_use, nblk), 1)
    ov = ((nrow * CMP_STRIDE < (jcol + 1) * SLC_BLOCK) & (nrow * CMP_STRIDE + CMP_LEN > jcol * SLC_BLOCK)
          & (nrow < n_cmp)).astype(F32)
    imp = jnp.dot(pcsum, ov, precision=HIGHEST, preferred_element_type=F32)
    jl = lax.broadcasted_iota(jnp.int32, (t, nblk), 1)
    tq = past + lax.broadcasted_iota(jnp.int32, (t, nblk), 0)
    cur = tq // SLC_BLOCK
    forced = (jl == 0) | (jl == cur) | (jl == cur - 1)
    imp = jnp.where(forced, jnp.inf, imp)
    imp = jnp.where((jl * SLC_BLOCK <= tq) & (jl < n_slc), imp, -jnp.inf)
    imp_t = jnp.concatenate([imp, jnp.full((128 - t, nblk), -jnp.inf, F32)], axis=0).T
    nb_rows = -(-n_slc // 8) * 8
    irow = lax.broadcasted_iota(jnp.int32, (nb_rows, nblk), 0)
    jlane = lax.broadcasted_iota(jnp.int32, (nb_rows, nblk), 1)
    ranks = []
    for qi in range(t):
        colv = imp_t[0:nb_rows, qi:qi + 1]
        rowv = imp[qi:qi + 1, :]
        before = (colv > rowv) | ((colv == rowv) & (irow < jlane))
        ranks.append(jnp.sum(before.astype(F32), axis=0, keepdims=True))
    rank = jnp.concatenate(ranks, axis=0)
    sel = ((rank < n_sel) & (imp > -jnp.inf)).astype(F32)
    sel4 = jnp.concatenate([sel] * NSA_H, axis=0)

    bpp = PAGE_SIZE // SLC_BLOCK
    plane = lax.broadcasted_iota(jnp.int32, (rows, PAGE_SIZE), 1)
    parts = []
    for p in range(n_pages + 1):
        scp = sc_ref[:, p * PAGE_SIZE:(p + 1) * PAGE_SIZE]
        mp = jnp.where(plane < SLC_BLOCK, sel4[:, bpp * p:bpp * p + 1], sel4[:, bpp * p + 1:bpp * p + 2]) > 0.5
        if p == n_pages:
            mp = mp & (past + plane <= tpos)
        parts.append(jnp.where(mp, scp, NEG))
    sc = jnp.concatenate(parts, axis=1)
    e = jnp.exp(sc - jnp.max(sc, axis=1, keepdims=True))
    acc = jnp.zeros((rows, NSA_HD), F32)
    for p in range(n_pages + 1):
        acc = acc + _dot_nt(e[:, p * PAGE_SIZE:(p + 1) * PAGE_SIZE].astype(BF16),
                            xt_ref[slot, p, 3 * NSA_HD:4 * NSA_HD, :].astype(BF16))
    o_slc = acc / jnp.sum(e, axis=1, keepdims=True)

    y_ref[0] = _gated_mix(dtg_ref[...], o_cmp, o_slc, o_win, t).astype(BF16)


def _nsa_dec(page_table, cache_t, q, dtg, kvc, kvw, win_t, w1, pe, w2, layer, n_phys, nb, t):
    n_pages = page_table.shape[1]
    xrows = (n_pages + 1) * PAGE_SIZE
    assert t <= PAGE_SIZE and n_pages % 8 == 0
    grid_spec = pltpu.PrefetchScalarGridSpec(
        num_scalar_prefetch=1,
        grid=(nb,),
        in_specs=[
            pl.BlockSpec(memory_space=pl.ANY),
            pl.BlockSpec((t, NSA_W), lambda b, pt: (b, 0)),
            pl.BlockSpec((t, 128), lambda b, pt: (b, 0)),
            pl.BlockSpec((t, 4 * NSA_HD), lambda b, pt: (b, 0)),
            pl.BlockSpec((t, 2 * NSA_HD), lambda b, pt: (b, 0)),
            pl.BlockSpec((1, 2 * NSA_HD, WINDOW), lambda b, pt: (layer * nb + b, 0, 0)),
            pl.BlockSpec(w1.shape, lambda b, pt: (0, 0, 0)),
            pl.BlockSpec(pe.shape, lambda b, pt: (0, 0, 0)),
            pl.BlockSpec(w2.shape, lambda b, pt: (0, 0)),
        ],
        out_specs=pl.BlockSpec((1, t, NSA_W), lambda b, pt: (b, 0, 0)),
        scratch_shapes=[
            pltpu.VMEM((2, n_pages + 1, 4 * NSA_HD, PAGE_SIZE), F32),
            pltpu.VMEM((xrows, 128), F32),
            pltpu.VMEM((xrows // CMP_STRIDE + 8, 4 * CMP_HID), F32),
            pltpu.VMEM((2 * NSA_HD, WINDOW + 128), F32),
            pltpu.VMEM((NSA_H * t, xrows), F32),
            pltpu.SemaphoreType.DMA((2,)),
        ],
    )
    return pl.pallas_call(
        functools.partial(_nsa_dec_body, layer, n_phys, n_pages, nb, t),
        grid_spec=grid_spec,
        out_shape=jax.ShapeDtypeStruct((nb, t, NSA_W), BF16),
        compiler_params=_cparams(("arbitrary",)),
        name="nsa_dec",
    )(page_table, cache_t, q, dtg, kvc, kvw, win_t, w1, pe, w2).reshape(nb * t, NSA_W)


def _out_proj_body(ys_ref, yp_ref, yn_ref, x_ref, w_ref, npost_ref, nffn_ref, x1_ref, h2_ref):
    tm = x_ref.shape[0]
    n_split = 2 if tm % 32 == 0 else 1
    for i in range(n_split):
        r = slice(i * tm // n_split, (i + 1) * tm // n_split)
        mix = (_dot(ys_ref[r, :], w_ref[0, 0:SSD_W, :]) + _dot(yp_ref[r, :], w_ref[0, SSD_W:SSD_W + POOL_W, :])
               + _dot(yn_ref[r, :], w_ref[0, SSD_W + POOL_W:D_MODEL, :]))
        x1 = x_ref[r, :] + _rmsnorm(mix, npost_ref[...])
        x1_ref[r, :] = x1
        h2_ref[r, :] = _rmsnorm(x1, nffn_ref[...]).astype(BF16)


def _out_proj(ys, yp, yn, x2d, w_out, layer, npost, nffn, tm):
    m = x2d.shape[0]

    def tile(w):
        return pl.BlockSpec((tm, w), lambda i: (i, 0))

    vec = pl.BlockSpec((1, D_MODEL), lambda i: (0, 0))
    return pl.pallas_call(
        _out_proj_body,
        grid=(m // tm,),
        in_specs=[tile(SSD_W), tile(POOL_W), tile(NSA_W), tile(D_MODEL),
                  pl.BlockSpec((1, D_MODEL, D_MODEL), lambda i: (layer, 0, 0)), vec, vec],
        out_specs=[tile(D_MODEL), tile(D_MODEL)],
        out_shape=[jax.ShapeDtypeStruct((m, D_MODEL), F32), jax.ShapeDtypeStruct((m, D_MODEL), BF16)],
        compiler_params=_cparams(("arbitrary",)),
        name="out_proj",
    )(ys, yp, yn, x2d, w_out, npost, nffn)


def _ffn_body(carry_mode, tiles_per_seq, t, tf, *refs):
    if carry_mode:
        (h2_ref, x1_ref, wg_ref, wv_ref, wd_ref, cw_ref, cb_ref, npost_ref,
         x2_ref, gsave_ref, act_ref, gext_ref, carry_ref) = refs
    else:
        (h2_ref, x1_ref, wg_ref, wv_ref, wd_ref, cw_ref, cb_ref, npost_ref, prev_ref,
         x2_ref, gsave_ref, act_ref) = refs
    tm = h2_ref.shape[0]
    h2 = h2_ref[...]
    if carry_mode:
        @pl.when(pl.program_id(0) % tiles_per_seq == 0)
        def _():
            carry_ref[...] = jnp.zeros(carry_ref.shape, F32)
    else:
        r = lax.broadcasted_iota(jnp.int32, (tm, tf), 0) % t
    for f in range(D_FF // tf):
        fs = slice(f * tf, (f + 1) * tf)
        g = _dot(h2, wg_ref[0, :, fs])
        v = _dot(h2, wv_ref[0, :, fs])
        if carry_mode:
            buf = f % 2
            gext_ref[buf, 0:8, :] = carry_ref[:, fs]
            gext_ref[buf, 8:8 + tm, :] = g
            carry_ref[:, fs] = g[tm - 8:tm]
            gsave_ref[0, :, fs] = g[tm - 8:tm]
            g1 = gext_ref[buf, 7:7 + tm, :]
            g2 = gext_ref[buf, 6:6 + tm, :]
        else:
            prev = prev_ref[:, fs]
            g1 = jnp.where(r >= 1, pltpu.roll(g, 1, 0), pltpu.roll(prev, tm - 1, 0))
            g2 = jnp.where(r >= 2, pltpu.roll(g, 2, 0), prev)
            gsave_ref[:, fs] = g
        cw = cw_ref[:, fs]
        gc = g * cw[2:3, :] + g1 * cw[1:2, :] + g2 * cw[0:1, :] + cb_ref[:, fs]
        act_ref[:, fs] = (_gelu_tanh(gc) * v).astype(BF16)
    out = _dot(act_ref[...], wd_ref[0])
    x2_ref[...] = x1_ref[...] + _rmsnorm(out, npost_ref[...])


def _ffn(h2, x1, wg, wv, wd, layer, cw, cb, npost, tm, tf, carry_mode, tiles_per_seq, t, prev_rows=None):
    m = h2.shape[0]
    nm = m // tm
    resident = pl.Buffered(1)
    in_specs = [
        pl.BlockSpec((tm, D_MODEL), lambda i: (i, 0)),
        pl.BlockSpec((tm, D_MODEL), lambda i: (i, 0)),
        pl.BlockSpec((1, D_MODEL, D_FF), lambda i: (layer, 0, 0), pipeline_mode=resident),
        pl.BlockSpec((1, D_MODEL, D_FF), lambda i: (layer, 0, 0), pipeline_mode=resident),
        pl.BlockSpec((1, D_FF, D_MODEL), lambda i: (layer, 0, 0), pipeline_mode=resident),
        pl.BlockSpec((FFN_CONV, D_FF), lambda i: (0, 0)),
        pl.BlockSpec((1, D_FF), lambda i: (0, 0)),
        pl.BlockSpec((1, D_MODEL), lambda i: (0, 0)),
    ]
    args = [h2, x1, wg, wv, wd, cw, cb, npost]
    scratch = [pltpu.VMEM((tm, D_FF), BF16)]
    if carry_mode:
        gsave_shape = jax.ShapeDtypeStruct((nm, 8, D_FF), F32)
        gsave_spec = pl.BlockSpec((1, 8, D_FF), lambda i: (i, 0, 0))
        scratch += [pltpu.VMEM((2, tm + 8, tf), F32), pltpu.VMEM((8, D_FF), F32)]
    else:
        gsave_shape = jax.ShapeDtypeStruct((m, D_FF), F32)
        gsave_spec = pl.BlockSpec((tm, D_FF), lambda i: (i, 0))
        in_specs.append(pl.BlockSpec((tm, D_FF), lambda i: (i, 0)))
        args.append(prev_rows)
    return pl.pallas_call(
        functools.partial(_ffn_body, carry_mode, tiles_per_seq, t, tf),
        grid=(nm,),
        in_specs=in_specs,
        out_specs=[pl.BlockSpec((tm, D_MODEL), lambda i: (i, 0)), gsave_spec],
        out_shape=[jax.ShapeDtypeStruct((m, D_MODEL), F32), gsave_shape],
        scratch_shapes=scratch,
        compiler_params=_cparams(("arbitrary",)),
        name="ffn",
    )(*args)


def _prep_params(w_in, ssd_conv_w, ssd_conv_b, ssd_dt_bias, ssd_a_log, ssd_d, ssd_norm, pool_w, pool_scale,
                 nsa_pe_k, nsa_pe_v, nsa_w1_k, nsa_w1_v, nsa_w2_k, nsa_w2_v, w_out, ffn_w_gate, ffn_w_val,
                 ffn_w_down):
    depth = w_in.shape[0]
    w_in_p = jnp.concatenate(
        [w_in[..., 0:OFF_XBC], w_in[..., OFF_DT:OFF_KV], w_in[..., OFF_XBC:OFF_DT], w_in[..., OFF_KV:IN_W],
         jnp.zeros((depth, D_MODEL, IN_W_PAD - IN_W), F32)], axis=-1).astype(BF16)
    pad_h = ((0, 0), (0, 128 - SSD_H))
    dtb = jnp.pad(ssd_dt_bias, pad_h)[:, None, :]
    alog = jnp.pad(ssd_a_log, pad_h)[:, None, :]
    dsk = jnp.repeat(ssd_d, SSD_HD, axis=1)[:, None, :]
    n_g = len(POOL_WINDOWS)
    eye = jnp.eye(n_g, dtype=F32)
    pw_bd = (pool_w[:, :, :, None, :] * eye[None, :, None, :, None]).reshape(depth, POOL_W, POOL_W).astype(BF16)
    zk = jnp.zeros((depth, CMP_STRIDE, NSA_HD, CMP_HID), F32)
    top = jnp.concatenate([nsa_w1_k[:, :CMP_STRIDE], zk, nsa_w1_k[:, CMP_STRIDE:], zk], axis=-1)
    bot = jnp.concatenate([zk, nsa_w1_v[:, :CMP_STRIDE], zk, nsa_w1_v[:, CMP_STRIDE:]], axis=-1)
    w1 = jnp.concatenate([top, bot], axis=2).astype(BF16).reshape(depth, CMP_STRIDE // 2, 256, 4 * CMP_HID)
    pe_a = jnp.concatenate([nsa_pe_k[:, :CMP_STRIDE], nsa_pe_v[:, :CMP_STRIDE]], axis=-1)
    pe_b = jnp.concatenate([nsa_pe_k[:, CMP_STRIDE:], nsa_pe_v[:, CMP_STRIDE:]], axis=-1)
    pe = jnp.concatenate([pe_a[:, :, None], pe_b[:, :, None], jnp.zeros((depth, CMP_STRIDE, 6, 128), F32)], axis=2)
    pe = pe.reshape(depth, CMP_STRIDE // 2, 2, 8, 128).transpose(0, 1, 3, 2, 4).reshape(depth, CMP_STRIDE // 2, 8, 256)
    zw = jnp.zeros((depth, CMP_HID, NSA_HD), F32)
    w2 = jnp.concatenate([jnp.concatenate([nsa_w2_k, zw], axis=-1), jnp.concatenate([zw, nsa_w2_v], axis=-1)],
                         axis=1).astype(BF16)
    return dict(w_in=w_in_p, cw=ssd_conv_w, cb=ssd_conv_b[:, None, :], dtb=dtb, alog=alog, dsk=dsk,
                nrm=ssd_norm[:, None, :], pw=pw_bd, ps=pool_scale[:, None, :], w1=w1, pe=pe, w2=w2,
                w_out=w_out.astype(BF16), wg=ffn_w_gate.astype(BF16), wv=ffn_w_val.astype(BF16),
                wd=ffn_w_down.astype(BF16))


def _mixer_weights(p, l):
    return (p['cw'][l], p['cb'][l], p['dtb'][l], p['alog'][l], p['dsk'][l], p['nrm'][l], p['pw'][l], p['ps'][l])


def kernel(x_prompt, x_sample, cache_nsa_kv, page_table, state_nsa_win, state_ssd_conv, state_ssm, state_pool, state_ffn_conv, norm_mix_pre, w_in, ssd_conv_w, ssd_conv_b, ssd_dt_bias, ssd_a_log, ssd_d, ssd_norm, pool_w, pool_scale, nsa_pe_k, nsa_pe_v, nsa_w1_k, nsa_w1_v, nsa_w2_k, nsa_w2_v, w_out, norm_mix_post, norm_ffn_pre, ffn_w_gate, ffn_w_val, ffn_conv_w, ffn_conv_b, ffn_w_down, norm_ffn_post):
    depth = w_in.shape[0]
    bp, tp, _ = x_prompt.shape
    bs, ts, _ = x_sample.shape
    n_phys = cache_nsa_kv.shape[1]
    n_pages = page_table.shape[1]
    past = n_pages * PAGE_SIZE
    assert tp % 512 == 0 and tp >= WINDOW and ts == 8 and state_nsa_win.shape[2] == WINDOW
    p = _prep_params(w_in, ssd_conv_w, ssd_conv_b, ssd_dt_bias, ssd_a_log, ssd_d, ssd_norm, pool_w, pool_scale,
                     nsa_pe_k, nsa_pe_v, nsa_w1_k, nsa_w1_v, nsa_w2_k, nsa_w2_v, w_out, ffn_w_gate, ffn_w_val,
                     ffn_w_down)
    ms = bs * ts
    tab_p = _rope_table(jnp.arange(tp, dtype=jnp.int32))
    tab_s = jnp.tile(_rope_table(past + jnp.arange(ts, dtype=jnp.int32)), (bs, 1))
    cache = jnp.transpose(cache_nsa_kv, (0, 1, 3, 4, 2)).reshape(depth * n_phys, 4 * NSA_HD, PAGE_SIZE)
    win_state = jnp.transpose(state_nsa_win, (0, 1, 3, 4, 2)).reshape(depth * bs, 2 * NSA_HD, WINDOW)

    tm_p = 512
    tm_f = 512
    tf = 512
    nc_p = tp // CHUNK
    zeros_c = jnp.zeros((bp, 8, SSD_CONV_DIM), F32)
    zeros_s = jnp.zeros((bp, SSD_H, SSD_HD, SSD_N), F32)
    zeros_p = jnp.zeros((bp, 16, POOL_W), F32)

    xp = x_prompt.reshape(bp * tp, D_MODEL)
    xs = x_sample.reshape(ms, D_MODEL)
    outs_p, outs_s = [], []
    for l in range(depth):
        nw = norm_mix_pre[l][None]
        npost = norm_mix_post[l][None]
        nffn = norm_ffn_pre[l][None]
        nfpost = norm_ffn_post[l][None]
        mw = _mixer_weights(p, l)
        fcw, fcb = ffn_conv_w[l], ffn_conv_b[l][None]

        y_ssd, y_pool, ssm_new, q, kvc, kvw, dtg, ctail, ptail = _ssd_pool(
            xp, zeros_c, zeros_s, zeros_p, mw, bp, nc_p, CHUNK, 0, 2, fused_in=(nw, p['w_in'], l, tab_p))
        ckv = _nsa_cmp(kvc, p['w1'][l], p['pe'][l], p['w2'][l], bp, tp)
        y_nsa = _nsa_attn(q, dtg, ckv, kvc, kvw, bp, tp)
        x1, h2 = _out_proj(y_ssd, y_pool, y_nsa, xp, p['w_out'], l, npost, nffn, tm_p)
        xp, gsave = _ffn(h2, x1, p['wg'], p['wv'], p['wd'], l, fcw, fcb, nfpost, tm_f, tf, True, tp // tm_f, tp)
        outs_p.append((
            kvc.reshape(bp, tp, 4, NSA_HD),
            kvw.reshape(bp, tp, 2, NSA_HD)[:, tp - WINDOW:],
            ctail[:, 8 - (SSD_CONV - 1):],
            ssm_new,
            ptail[:, 16 - POOL_KEEP:],
            gsave.reshape(bp, tp // tm_f, 8, D_FF)[:, -1, 8 - (FFN_CONV - 1):],
        ))

        z, xbc, u, q, kvc, kvw, dtg = _in_proj(xs, nw, p['w_in'], l, tab_s, ms)
        cprev = jnp.pad(state_ssd_conv[l], ((0, 0), (8 - (SSD_CONV - 1), 0), (0, 0)))
        pprev = jnp.pad(state_pool[l], ((0, 0), (16 - POOL_KEEP, 0), (0, 0)))
        y_ssd, y_pool, ssm_new = _ssd_pool((xbc, z, dtg, u), cprev, state_ssm[l], pprev, mw, bs, 1, ts, past, 4)
        y_nsa = _nsa_dec(page_table, cache, q, dtg, kvc, kvw, win_state, p['w1'][l], p['pe'][l], p['w2'][l],
                         l, n_phys, bs, ts)
        x1, h2 = _out_proj(y_ssd, y_pool, y_nsa, xs, p['w_out'], l, npost, nffn, ms)
        fprev = state_ffn_conv[l]
        fprev_rows = jnp.pad(fprev, ((0, 0), (0, ts - (FFN_CONV - 1)), (0, 0))).reshape(ms, D_FF)
        xs, gsave = _ffn(h2, x1, p['wg'], p['wv'], p['wd'], l, fcw, fcb, nfpost, ms, tf, False, 1, ts, fprev_rows)
        kvw3 = kvw.reshape(bs, ts, 2, NSA_HD)
        xbc3 = xbc.reshape(bs, ts, SSD_CONV_DIM)
        u3 = u.reshape(bs, ts, POOL_W)
        g3 = gsave.reshape(bs, ts, D_FF)
        outs_s.append((
            kvc.reshape(bs, ts, 4, NSA_HD),
            jnp.concatenate([state_nsa_win[l], kvw3], axis=1)[:, -WINDOW:],
            jnp.concatenate([state_ssd_conv[l], xbc3], axis=1)[:, -(SSD_CONV - 1):],
            ssm_new,
            jnp.concatenate([state_pool[l], u3], axis=1)[:, -POOL_KEEP:],
            jnp.concatenate([state_ffn_conv[l], g3], axis=1)[:, -(FFN_CONV - 1):],
        ))

    def stk(outs, i):
        return jnp.stack([o[i] for o in outs])

    return (xp.reshape(bp, tp, D_MODEL), xs.reshape(bs, ts, D_MODEL), stk(outs_p, 0), stk(outs_s, 0),
            stk(outs_p, 1), stk(outs_s, 1), stk(outs_p, 2), stk(outs_s, 2), stk(outs_p, 3), stk(outs_s, 3),
            stk(outs_p, 4), stk(outs_s, 4), stk(outs_p, 5), stk(outs_s, 5))
```

```python
import functools
import math

import jax
import jax.numpy as jnp
from jax import lax
from jax.experimental import pallas as pl
from jax.experimental.pallas import tpu as pltpu

F32 = jnp.float32
BF16 = jnp.bfloat16

D_MODEL = 1024
PAGE_SIZE = 128
SSD_W = 512
SSD_HD = 64
SSD_H = 8
SSD_N = 128
SSD_G = 2
SSD_CONV = 4
SSD_CONV_DIM = 1024
POOL_W = 256
POOL_WINDOWS = (2, 4, 8, 16)
POOL_GC = 64
POOL_KEEP = 15
NSA_W = 256
NSA_HD = 64
NSA_H = 4
CMP_STRIDE = 16
CMP_LEN = 32
CMP_HID = 128
SLC_BLOCK = 64
N_SELECT = 16
WINDOW = 512
ROPE_DIM = 16
ROPE_THETA = 500000.0
D_FF = 4096
FFN_CONV = 3
RMS_EPS = 1e-6
OFF_Z = 512
OFF_XBC = 1536
OFF_DT = 1544
OFF_POOL = 1800
OFF_Q = 2056
OFF_KV = 2440
IN_W = 2452
IN_W_PAD = 2560
CHUNK = 128
NEG = -1e30
LOG2E = 1.4426950408889634
VMEM_LIMIT = 56 * 1024 * 1024


def _cparams(sem):
    return pltpu.CompilerParams(dimension_semantics=sem, vmem_limit_bytes=VMEM_LIMIT)


def _rmsnorm(x, w):
    return x * lax.rsqrt(jnp.mean(x * x, axis=-1, keepdims=True) + RMS_EPS) * w


def _silu(x):
    return x / (1.0 + jnp.exp(-x))


def _gelu_tanh(x):
    k = 0.7978845608028654
    hx = 0.5 * x
    return hx + hx * jnp.tanh(x * (k + (k * 0.044715) * (x * x)))


def _dot(a, b):
    return jnp.dot(a, b, preferred_element_type=F32)


def _dot_nt(a, b):
    return lax.dot_general(a, b, (((1,), (1,)), ((), ())), preferred_element_type=F32)


def _split3(x):
    p0 = x.astype(BF16)
    r = x - p0.astype(F32)
    p1 = r.astype(BF16)
    return p0, p1, (r - p1.astype(F32)).astype(BF16)


def _dot_sel(parts, sel):
    return _dot(parts[0], sel) + _dot(parts[1], sel) + _dot(parts[2], sel)


def _in_proj_rows(x, nw, w_ref, tab_ref):
    r = x.shape[0]
    h = _rmsnorm(x, nw).astype(BF16)

    def proj(a, b):
        return _dot(h, w_ref[0, :, a:b])

    lane = lax.broadcasted_iota(jnp.int32, (r, 128), 1)
    first = (lane % NSA_HD) < (ROPE_DIM // 2)

    def rope(v, c, s):
        outs = []
        for j in range(v.shape[1] // 128):
            vj = v[:, j * 128:(j + 1) * 128]
            rot = jnp.where(first, pltpu.roll(vj, 128 - ROPE_DIM // 2, 1), pltpu.roll(vj, ROPE_DIM // 2, 1))
            outs.append(vj * c + rot * s)
        return outs

    q = rope(proj(1792, 2048), tab_ref[:, 0:128], tab_ref[:, 128:256])
    kv = rope(proj(2048, 2432), tab_ref[:, 256:384], tab_ref[:, 384:512])
    return (proj(0, 512), proj(512, 1536), proj(1536, 1792), jnp.concatenate(q, axis=1),
            jnp.concatenate(kv[0:2], axis=1), kv[2], proj(2432, 2560))


def _in_proj_body(x_ref, nw_ref, w_ref, tab_ref, z_ref, xbc_ref, pool_ref, q_ref, kvc_ref, kvw_ref, dtg_ref):
    outs = _in_proj_rows(x_ref[...], nw_ref[...], w_ref, tab_ref)
    for o_ref, o in zip((z_ref, xbc_ref, pool_ref, q_ref, kvc_ref, kvw_ref, dtg_ref), outs):
        o_ref[...] = o


def _in_proj(x2d, nw, w_in_p, layer, tab, tm):
    m = x2d.shape[0]
    tab_blocks = tab.shape[0] // tm
    widths = (512, 1024, 256, 256, 256, 128, 128)
    return pl.pallas_call(
        _in_proj_body,
        grid=(m // tm,),
        in_specs=[
            pl.BlockSpec((tm, D_MODEL), lambda i: (i, 0)),
            pl.BlockSpec((1, D_MODEL), lambda i: (0, 0)),
            pl.BlockSpec((1, D_MODEL, IN_W_PAD), lambda i: (layer, 0, 0)),
            pl.BlockSpec((tm, 512), lambda i: (i % tab_blocks, 0)),
        ],
        out_specs=[pl.BlockSpec((tm, w), lambda i: (i, 0)) for w in widths],
        out_shape=[jax.ShapeDtypeStruct((m, w), F32) for w in widths],
        compiler_params=_cparams(("arbitrary",)),
        name="in_proj",
    )(x2d, nw, w_in_p, tab)


def _rope_table(pos):
    half = ROPE_DIM // 2
    inv = 1.0 / (ROPE_THETA ** (jnp.arange(half, dtype=F32) / half))
    ang = pos.astype(F32)[:, None] * inv
    cos, sin = jnp.cos(ang), jnp.sin(ang)
    t = pos.shape[0]
    c_rot = jnp.concatenate([cos, cos, jnp.ones((t, NSA_HD - ROPE_DIM), F32)], axis=1)
    s_rot = jnp.concatenate([-sin, sin, jnp.zeros((t, NSA_HD - ROPE_DIM), F32)], axis=1)
    one, zero = jnp.ones((t, NSA_HD), F32), jnp.zeros((t, NSA_HD), F32)
    return jnp.concatenate([c_rot, c_rot, s_rot, s_rot, c_rot, one, s_rot, zero], axis=1)


def _ssd_pool_body(lr, nc, pos0, nsub, fused, *refs):
    if fused:
        (x_ref, nw_ref, win_ref, tab_ref, cprev_ref, sprev_ref, pprev_ref,
         cw_ref, cb_ref, dtb_ref, alog_ref, dsk_ref, nrm_ref, pw_ref, ps_ref,
         yssd_ref, ypool_ref, snew_ref, q_ref, kvc_ref, kvw_ref, dtg_ref, ctail_ref, ptail_ref,
         ext_ref, state_ref, pext_ref) = refs
    else:
        (xbc_ref, z_ref, dtg_ref, u_ref, cprev_ref, sprev_ref, pprev_ref,
         cw_ref, cb_ref, dtb_ref, alog_ref, dsk_ref, nrm_ref, pw_ref, ps_ref,
         yssd_ref, ypool_ref, snew_ref, ext_ref, state_ref, pext_ref) = refs
    c = pl.program_id(1)
    L = CHUNK

    @pl.when(c == 0)
    def _():
        ext_ref[:, 0:8, :] = cprev_ref[...]
        pext_ref[:, 0:16, :] = pprev_ref[...]
        state_ref[...] = sprev_ref[...]

    if nc > 1:
        @pl.when(c > 0)
        def _():
            ext_ref[:, 0:8, :] = ext_ref[:, L:L + 8, :]
            pext_ref[:, 0:16, :] = pext_ref[:, L:L + 16, :]

    z_rows, dt_rows = [], []
    for sub in range(nsub):
        if fused:
            z, xbc, u, q, kvc, kvw, dtg = _in_proj_rows(x_ref[sub], nw_ref[...], win_ref, tab_ref)
            ext_ref[sub, 8:8 + lr, :] = xbc
            pext_ref[sub, 16:16 + lr, :] = u
            q_ref[sub] = q
            kvc_ref[sub] = kvc
            kvw_ref[sub] = kvw
            dtg_ref[sub] = dtg
        else:
            z, dtg = z_ref[sub], dtg_ref[sub]
            ext_ref[sub, 8:8 + lr, :] = xbc_ref[sub]
            pext_ref[sub, 16:16 + lr, :] = u_ref[sub]
        z_rows.append(z)
        dt_rows.append(dtg)
    if lr < L:
        ext_ref[:, 8 + lr:8 + L, :] = jnp.zeros((nsub, L - lr, SSD_CONV_DIM), F32)
        pext_ref[:, 16 + lr:16 + L, :] = jnp.zeros((nsub, L - lr, POOL_W), F32)

    cw = cw_ref[...]
    row = lax.broadcasted_iota(jnp.int32, (L, L), 0)
    col = lax.broadcasted_iota(jnp.int32, (L, L), 1)
    tri = row >= col
    tri_b = tri.astype(F32).astype(BF16)
    eye_b = (row == col).astype(F32).astype(BF16)
    gw = (SSD_H // SSD_G) * SSD_HD
    expand = (lax.broadcasted_iota(jnp.int32, (128, SSD_W), 0)
              == lax.broadcasted_iota(jnp.int32, (128, SSD_W), 1) // SSD_HD).astype(F32).astype(BF16)
    neg_a = -jnp.exp(alog_ref[...])
    dsk = dsk_ref[...]
    prow = lax.broadcasted_iota(jnp.int32, (L, POOL_W), 0)
    plane = lax.broadcasted_iota(jnp.int32, (L, POOL_W), 1)
    pos = pos0 + c * L + prow + 1

    def stage_conv(sub, k):
        conv = (ext_ref[sub, 8:8 + L, :] * cw[3:4, :] + ext_ref[sub, 7:7 + L, :] * cw[2:3, :]
                + ext_ref[sub, 6:6 + L, :] * cw[1:2, :] + ext_ref[sub, 5:5 + L, :] * cw[0:1, :] + cb_ref[...])
        xbc = _silu(conv)
        k['xs'] = xbc[:, 0:SSD_W]
        k['bmat'] = [xbc[:, SSD_W + g * SSD_N:SSD_W + (g + 1) * SSD_N].astype(BF16) for g in range(SSD_G)]
        k['cmat'] = [xbc[:, SSD_W + (SSD_G + g) * SSD_N:SSD_W + (SSD_G + g + 1) * SSD_N].astype(BF16)
                     for g in range(SSD_G)]

    def stage_decay(sub, k):
        if lr == L:
            dtraw = dt_rows[sub]
        else:
            dtraw = jnp.concatenate([dt_rows[sub], jnp.zeros((L - lr, 128), F32)], axis=0)
        xdt = dtraw + dtb_ref[...]
        dt = jnp.maximum(xdt, 0.0) + jnp.log1p(jnp.exp(-jnp.abs(xdt)))
        dt = jnp.where((col < SSD_H) & (row < lr), dt, 0.0)
        da3 = _split3(dt * neg_a)
        acum = _dot(tri_b, da3[0]) + _dot(tri_b, da3[1]) + _dot(tri_b, da3[2])
        ac3 = _split3(acum)
        k['acum'] = acum
        k['arow'] = _dot_nt(eye_b, ac3[0]) + _dot_nt(eye_b, ac3[1]) + _dot_nt(eye_b, ac3[2])
        dt_x = _dot_sel(_split3(dt), expand)
        k['ac_x'] = ac_x = _dot_sel(ac3, expand)
        k['alast_x'] = alast_x = ac_x[L - 1:L, :]
        xdt = k['xs'] * dt_x
        k['xdt_b'] = xdt.astype(BF16)
        k['xw'] = (xdt * jnp.exp(alast_x - ac_x)).astype(BF16)

    def stage_state(sub, k):
        cmat, bmat = k['cmat'], k['bmat']
        k['cb_scores'] = [_dot_nt(cmat[g], bmat[g]) for g in range(SSD_G)]
        y_off = []
        for g in range(SSD_G):
            gs = slice(g * gw, (g + 1) * gw)
            st = state_ref[sub, g]
            y_off.append(_dot(cmat[g], st.astype(BF16)))
            b_t = _dot_nt(eye_b, bmat[g]).astype(BF16)
            state_ref[sub, g] = jnp.exp(k['alast_x'][:, gs]) * st + _dot(b_t, k['xw'][:, gs])
        k['y_off'] = y_off

    def stage_diag(sub, k):
        y_diag = []
        for hp in range(SSD_H // 2):
            tile_b = k['xdt_b'][:, hp * 128:(hp + 1) * 128]
            acc = None
            for h in (2 * hp, 2 * hp + 1):
                g = h // (SSD_H // SSD_G)
                decay = jnp.exp(jnp.where(tri, k['acum'][:, h:h + 1] - k['arow'][h:h + 1, :], NEG))
                own = (col < SSD_HD) if h % 2 == 0 else (col >= SSD_HD)
                part = _dot((k['cb_scores'][g] * decay).astype(BF16),
                            jnp.where(own, tile_b, jnp.zeros_like(tile_b)))
                acc = part if acc is None else acc + part
            y_diag.append(acc)
        k['y_diag'] = y_diag

    def stage_out(sub, k):
        y = (jnp.concatenate(k['y_diag'], axis=1) + jnp.concatenate(k['y_off'], axis=1) * jnp.exp(k['ac_x'])
             + dsk * k['xs'])
        if lr < L:
            y = y[0:lr]
        y = y * _silu(z_rows[sub])
        yssd_ref[sub] = _rmsnorm(y, nrm_ref[...]).astype(BF16)

    def stage_pool(sub, k):
        e1 = pext_ref[sub]
        s2 = e1 + pltpu.roll(e1, 1, 0)
        s4 = s2 + pltpu.roll(s2, 2, 0)
        s8 = s4 + pltpu.roll(s4, 4, 0)
        s16 = s8 + pltpu.roll(s8, 8, 0)
        pooled = jnp.zeros((L, POOL_W), F32)
        for gi, (w, sw) in enumerate(zip(POOL_WINDOWS, (s2, s4, s8, s16))):
            cnt = jnp.minimum(pos, w).astype(F32)
            grp = (plane >= gi * POOL_GC) & (plane < (gi + 1) * POOL_GC)
            pooled = jnp.where(grp, sw[16:16 + L] / cnt, pooled)
        pooled = pooled - e1[16:16 + L]
        yp = _dot(pooled.astype(BF16), pw_ref[...]) * ps_ref[...]
        ypool_ref[sub] = (yp[0:lr] if lr < L else yp).astype(BF16)

    keep = [dict() for _ in range(nsub)]
    for stage in (stage_conv, stage_decay, stage_state, stage_diag, stage_out, stage_pool):
        for sub in range(nsub):
            stage(sub, keep[sub])

    @pl.when(c == nc - 1)
    def _():
        snew_ref[...] = state_ref[...]
        if fused:
            ctail_ref[...] = ext_ref[:, L:L + 8, :]
            ptail_ref[...] = pext_ref[:, L:L + 16, :]


def _ssd_pool(acts, cprev, sprev, pprev, wts, nb, nc, lr, pos0, nsub, fused_in=None):
    cw, cb, dtb, alog, dsk, nrm, pw, ps = wts
    t = nc * lr
    assert nb % nsub == 0

    def tile(w):
        return pl.BlockSpec((nsub, lr, w), lambda b, c: (b, c, 0))

    def full2(a):
        return pl.BlockSpec(a.shape, lambda b, c: (0, 0))

    def rows(w, dtype=F32):
        return jax.ShapeDtypeStruct((nb, t, w), dtype)

    hg = SSD_H // SSD_G
    st_shape = (SSD_G, SSD_N, hg * SSD_HD)
    sprev_t = sprev.reshape(nb, SSD_G, hg, SSD_HD, SSD_N).transpose(0, 1, 4, 2, 3).reshape((nb,) + st_shape)
    state_spec = pl.BlockSpec((nsub,) + st_shape, lambda b, c: (b, 0, 0, 0))
    prev_specs = [pl.BlockSpec((nsub, 8, SSD_CONV_DIM), lambda b, c: (b, 0, 0)), state_spec,
                  pl.BlockSpec((nsub, 16, POOL_W), lambda b, c: (b, 0, 0))]
    w_specs = [full2(cw), full2(cb), full2(dtb), full2(alog), full2(dsk), full2(nrm), full2(pw), full2(ps)]
    out_specs = [tile(SSD_W), tile(POOL_W), state_spec]
    out_shape = [rows(SSD_W, BF16), rows(POOL_W, BF16), jax.ShapeDtypeStruct((nb,) + st_shape, F32)]
    if fused_in is None:
        xbc, z, dtg, u = acts
        in_specs = [tile(SSD_CONV_DIM), tile(SSD_W), tile(128), tile(POOL_W)]
        args = [xbc.reshape(nb, t, SSD_CONV_DIM), z.reshape(nb, t, SSD_W), dtg.reshape(nb, t, 128),
                u.reshape(nb, t, POOL_W)]
    else:
        nw, w_in_p, layer, tab = fused_in
        assert lr == CHUNK
        in_specs = [tile(D_MODEL), pl.BlockSpec((1, D_MODEL), lambda b, c: (0, 0)),
                    pl.BlockSpec((1, D_MODEL, IN_W_PAD), lambda b, c: (layer, 0, 0), pipeline_mode=pl.Buffered(1)),
                    pl.BlockSpec((lr, 512), lambda b, c: (c, 0))]
        args = [acts.reshape(nb, t, D_MODEL), nw, w_in_p, tab]
        out_specs += [tile(NSA_W), tile(4 * NSA_HD), tile(2 * NSA_HD), tile(128),
                      pl.BlockSpec((nsub, 8, SSD_CONV_DIM), lambda b, c: (b, 0, 0)),
                      pl.BlockSpec((nsub, 16, POOL_W), lambda b, c: (b, 0, 0))]
        out_shape += [rows(NSA_W), rows(4 * NSA_HD), rows(2 * NSA_HD), rows(128),
                      jax.ShapeDtypeStruct((nb, 8, SSD_CONV_DIM), F32), jax.ShapeDtypeStruct((nb, 16, POOL_W), F32)]
    outs = pl.pallas_call(
        functools.partial(_ssd_pool_body, lr, nc, pos0, nsub, fused_in is not None),
        grid=(nb // nsub, nc),
        in_specs=in_specs + prev_specs + w_specs,
        out_specs=out_specs,
        out_shape=out_shape,
        scratch_shapes=[pltpu.VMEM((nsub, CHUNK + 8, SSD_CONV_DIM), F32),
                        pltpu.VMEM((nsub,) + st_shape, F32),
                        pltpu.VMEM((nsub, CHUNK + 16, POOL_W), F32)],
        compiler_params=_cparams(("arbitrary", "arbitrary")),
        name="ssd_pool",
    )(*args, cprev, sprev_t, pprev, cw, cb, dtb, alog, dsk, nrm, pw, ps)
    yssd, ypool, snew_t = outs[:3]
    snew = snew_t.reshape(nb, SSD_G, SSD_N, hg, SSD_HD).transpose(0, 1, 3, 4, 2).reshape(nb, SSD_H, SSD_HD, SSD_N)
    flat = [o.reshape(nb * t, o.shape[-1]) for o in outs[3:7]]
    return (yssd.reshape(nb * t, SSD_W), ypool.reshape(nb * t, POOL_W), snew) + tuple(flat) + tuple(outs[7:])


def _compress(load_rows, n_rows, w1_ref, pe_ref, w2_ref, pq_ref):
    acc = jnp.zeros((n_rows, 4 * CMP_HID), F32)
    cacc = jnp.zeros((8, 4 * CMP_HID), F32)
    for l2 in range(CMP_STRIDE // 2):
        w = w1_ref[l2]
        rows = jnp.concatenate([load_rows(2 * l2), load_rows(2 * l2 + 1)], axis=1)
        acc = acc + _dot(rows.astype(BF16), w)
        cacc = cacc + _dot(pe_ref[l2].astype(BF16), w)
    pq_ref[0:n_rows, :] = acc
    pq_ref[n_rows:n_rows + 8, :] = jnp.zeros((8, 4 * CMP_HID), F32)
    const = cacc[0:1, 0:2 * CMP_HID] + cacc[1:2, 2 * CMP_HID:4 * CMP_HID]
    hid = pq_ref[0:n_rows, 0:2 * CMP_HID] + pq_ref[1:n_rows + 1, 2 * CMP_HID:4 * CMP_HID] + const
    return _dot(_gelu_tanh(hid).astype(BF16), w2_ref[...])


def _stack_heads(x):
    return jnp.concatenate([x[:, h * NSA_HD:(h + 1) * NSA_HD] for h in range(NSA_H)], axis=0)


def _gated_mix(dtg, o_cmp, o_slc, o_win, r):
    sig = 1.0 / (1.0 + jnp.exp(-dtg))
    outs = []
    for h in range(NSA_H):
        b = SSD_H + 3 * h
        sl = slice(h * r, (h + 1) * r)
        outs.append(sig[:, b:b + 1] * o_cmp[sl] + sig[:, b + 1:b + 2] * o_slc[sl] + sig[:, b + 2:b + 3] * o_win[sl])
    return jnp.concatenate(outs, axis=1)


def _nsa_cmp_body(kv_ref, w1_ref, pe_ref, w2_ref, ckv_ref, pq_ref):
    n_ch = kv_ref.shape[0] // CMP_STRIDE
    ckv_ref[...] = _compress(lambda l: kv_ref[pl.ds(l, n_ch, stride=CMP_STRIDE), :], n_ch, w1_ref, pe_ref, w2_ref,
                             pq_ref)


def _nsa_cmp(kvc, w1, pe, w2, nb, t):
    n_ch = t // CMP_STRIDE
    return pl.pallas_call(
        _nsa_cmp_body,
        grid=(nb,),
        in_specs=[
            pl.BlockSpec((t, 128), lambda b: (b, 0)),
            pl.BlockSpec(w1.shape, lambda b: (0, 0, 0)),
            pl.BlockSpec(pe.shape, lambda b: (0, 0, 0)),
            pl.BlockSpec(w2.shape, lambda b: (0, 0)),
        ],
        out_specs=pl.BlockSpec((n_ch, 128), lambda b: (b, 0)),
        out_shape=jax.ShapeDtypeStruct((nb * n_ch, 128), F32),
        scratch_shapes=[pltpu.VMEM((n_ch + 8, 4 * CMP_HID), F32)],
        compiler_params=_cparams(("arbitrary",)),
        name="nsa_cmp",
    )(kvc, w1, pe, w2)


def _flash_step_t(carry, st, bias, vt):
    m, l, acc = carry
    sm = st + bias
    m_new = jnp.maximum(m, jnp.max(sm, axis=0, keepdims=True))
    p = jnp.exp2(sm - m_new)
    alpha = jnp.exp2(m - m_new)
    l = alpha * l + jnp.sum(p, axis=0, keepdims=True)
    acc = alpha * acc + _dot(vt, p.astype(BF16))
    return m_new, l, acc


def _flash_init_t(cols):
    return jnp.full((1, cols), NEG, F32), jnp.zeros((1, cols), F32), jnp.zeros((NSA_HD, cols), F32)


def _nsa_attn_body(t, q_ref, dtg_ref, ckv_ref, kvs_ref, kvw_ref, y_ref, ks_ref, kst_ref, kw_ref, kwt_ref, selt_ref):
    s = pl.program_id(1)
    qb = CHUNK
    cols = NSA_H * qb
    n_kc = t // qb
    n_ch = t // CMP_STRIDE
    n_cmp = n_ch - 1
    n_slc = t // SLC_BLOCK
    nb_pad = -(-n_slc // 8) * 8
    n_sel = min(N_SELECT, n_slc)

    kb = ks_ref.shape[1] // qb
    n_wc = WINDOW // qb + 1

    @pl.when(s == 0)
    def _():
        for c in range(n_kc):
            tile = kvs_ref[c * qb:(c + 1) * qb, :]
            ks_ref[c // kb, (c % kb) * qb:(c % kb + 1) * qb, :] = tile.astype(BF16)
            kst_ref[c // kb, :, (c % kb) * qb:(c % kb + 1) * qb] = tile.T.astype(BF16)
            tile = kvw_ref[c * qb:(c + 1) * qb, :]
            kw_ref[c] = tile.astype(BF16)
            kwt_ref[c] = tile.T.astype(BF16)

    q_t = (q_ref[...] * (NSA_HD ** -0.5 * LOG2E)).T
    qs_t = jnp.concatenate([q_t[h * NSA_HD:(h + 1) * NSA_HD, :] for h in range(NSA_H)], axis=1).astype(BF16)
    lane = lax.broadcasted_iota(jnp.int32, (1, cols), 1)
    tpos = s * qb + lane % qb
    tpos_h = tpos[:, 0:qb]

    k0 = jnp.maximum(s - (n_wc - 1), 0)
    kw = jnp.concatenate([kw_ref[k0 + i, :, 0:NSA_HD] for i in range(n_wc)], axis=0)
    vwt = jnp.concatenate([kwt_ref[k0 + i, NSA_HD:2 * NSA_HD, :] for i in range(n_wc)], axis=1)
    kpos = k0 * qb + lax.broadcasted_iota(jnp.int32, (n_wc * qb, qb), 0)
    wbias = jnp.where((kpos <= tpos_h) & (kpos > tpos_h - WINDOW), 0.0, NEG)
    sw = _dot(kw, qs_t) + jnp.concatenate([wbias] * NSA_H, axis=1)
    pw = jnp.exp2(sw - jnp.max(sw, axis=0, keepdims=True))
    o_win = _dot(vwt, pw.astype(BF16)) / jnp.sum(pw, axis=0, keepdims=True)
    sig = 1.0 / (1.0 + jnp.exp(-dtg_ref[...].T[0:32, :]))

    ckv = ckv_ref[...]
    ck = ckv[:, 0:NSA_HD].astype(BF16)
    cv_t = ckv.T[NSA_HD:2 * NSA_HD, :].astype(BF16)
    sc = _dot(ck, qs_t)
    nrow = lax.broadcasted_iota(jnp.int32, (n_ch, cols), 0)
    cmask = (nrow * CMP_STRIDE + (CMP_LEN - 1) <= tpos) & (nrow < n_cmp)
    mx = jnp.max(jnp.where(cmask, sc, NEG), axis=0, keepdims=True)
    e = jnp.where(cmask, jnp.exp2(sc - mx), 0.0)
    pc = e / jnp.maximum(jnp.sum(e, axis=0, keepdims=True), 1e-30)
    o_cmp = _dot(cv_t, pc.astype(BF16))
    pcsum = pc[:, 0:qb] + pc[:, qb:2 * qb] + pc[:, 2 * qb:3 * qb] + pc[:, 3 * qb:4 * qb]

    jrow = lax.broadcasted_iota(jnp.int32, (nb_pad, n_ch), 0)
    ncol = lax.broadcasted_iota(jnp.int32, (nb_pad, n_ch), 1)
    ov_t = ((ncol * CMP_STRIDE < (jrow + 1) * SLC_BLOCK) & (ncol * CMP_STRIDE + CMP_LEN > jrow * SLC_BLOCK)
            & (ncol < n_cmp)).astype(F32).astype(BF16)
    pc3 = _split3(pcsum)
    imp = _dot(ov_t, pc3[0]) + _dot(ov_t, pc3[1]) + _dot(ov_t, pc3[2])
    brow = lax.broadcasted_iota(jnp.int32, (nb_pad, qb), 0)
    tq = s * qb + lax.broadcasted_iota(jnp.int32, (nb_pad, qb), 1)
    cur = tq // SLC_BLOCK
    forced = (brow == 0) | (brow == cur) | (brow == cur - 1)
    imp = jnp.where(forced, jnp.inf, imp)
    imp = jnp.where((brow * SLC_BLOCK <= tq) & (brow < n_slc), imp, -jnp.inf)
    rank = jnp.zeros((nb_pad, qb), F32)
    for i in range(n_slc):
        ri = imp[i:i + 1, :]
        before = (ri > imp) | ((ri == imp) & (brow > i))
        rank = rank + before.astype(F32)
    selt_ref[...] = jnp.where((rank < n_sel) & (imp > -jnp.inf), 0.0, NEG)

    bps = kb * qb // SLC_BLOCK
    krow_s = lax.broadcasted_iota(jnp.int32, (kb * qb, qb), 0)

    def slc_step(c, carry):
        k = ks_ref[c, :, 0:NSA_HD]
        vt = kst_ref[c, NSA_HD:2 * NSA_HD, :]
        sel_rows = selt_ref[pl.ds(pl.multiple_of(c * bps, bps), bps), :]
        bias = jnp.concatenate([jnp.broadcast_to(sel_rows[j:j + 1, :], (SLC_BLOCK, qb)) for j in range(bps)],
                               axis=0)
        bias = jnp.where(c * (kb * qb) + krow_s <= tpos_h, bias, NEG)
        return _flash_step_t(carry, _dot(k, qs_t), jnp.concatenate([bias] * NSA_H, axis=1), vt)

    carry = slc_step(0, _flash_init_t(cols))
    for c in range(1, n_kc // kb):
        carry = lax.cond(s >= c * kb, functools.partial(slc_step, c), lambda cr: cr, carry)
    _, l_s, acc_s = carry
    o_slc = acc_s / l_s

    outs = []
    for h in range(NSA_H):
        b = SSD_H + 3 * h
        sl = slice(h * qb, (h + 1) * qb)
        outs.append(sig[b:b + 1, :] * o_cmp[:, sl] + sig[b + 1:b + 2, :] * o_slc[:, sl]
                    + sig[b + 2:b + 3, :] * o_win[:, sl])
    y_ref[...] = jnp.concatenate(outs, axis=0).T.astype(BF16)


def _nsa_attn(q, dtg, ckv, kvc, kvw, nb, t):
    nq = t // CHUNK
    n_ch = t // CMP_STRIDE
    kb = 8
    assert n_ch % 128 == 0 and nq % kb == 0 and nq > WINDOW // CHUNK
    nb_pad = -(-(t // SLC_BLOCK) // 8) * 8
    kv_scratch = pltpu.VMEM((nq, CHUNK, 2 * NSA_HD), BF16)
    return pl.pallas_call(
        functools.partial(_nsa_attn_body, t),
        grid=(nb, nq),
        in_specs=[
            pl.BlockSpec((CHUNK, NSA_W), lambda b, s: (b * nq + s, 0)),
            pl.BlockSpec((CHUNK, 128), lambda b, s: (b * nq + s, 0)),
            pl.BlockSpec((n_ch, 128), lambda b, s: (b, 0)),
            pl.BlockSpec((t, 128), lambda b, s: (b, 1)),
            pl.BlockSpec((t, 128), lambda b, s: (b, 0)),
        ],
        out_specs=pl.BlockSpec((CHUNK, NSA_W), lambda b, s: (b * nq + s, 0)),
        out_shape=jax.ShapeDtypeStruct((nb * t, NSA_W), BF16),
        scratch_shapes=[pltpu.VMEM((nq // kb, kb * CHUNK, 2 * NSA_HD), BF16),
                        pltpu.VMEM((nq // kb, 2 * NSA_HD, kb * CHUNK), BF16),
                        kv_scratch, kv_scratch, pltpu.VMEM((nb_pad, CHUNK), F32)],
        compiler_params=_cparams(("arbitrary", "arbitrary")),
        name="nsa_attn",
    )(q, dtg, ckv, kvc, kvw)


def _softmax_pv_nt(sc, mask, vt):
    mx = jnp.max(jnp.where(mask, sc, NEG), axis=1, keepdims=True)
    e = jnp.where(mask, jnp.exp(sc - mx), 0.0)
    l = jnp.sum(e, axis=1, keepdims=True)
    return _dot_nt(e.astype(BF16), vt) / jnp.maximum(l, 1e-30)


def _rows_to_cols(x):
    r, w = x.shape
    return jnp.concatenate([x, jnp.zeros((128 - r, w), F32)], axis=0).T


def _nsa_dec_body(layer, n_phys, n_pages, nb, t, pt_ref, cache_ref, q_ref, dtg_ref, kvn_ref, kwn_ref, win_ref,
                  w1_ref, pe_ref, w2_ref, y_ref, xt_ref, xc_ref, pq_ref, wk_ref, sc_ref, sem):
    b = pl.program_id(0)
    past = n_pages * PAGE_SIZE
    lk = past + t
    n_ch = -(-lk // CMP_STRIDE)
    n_cmp = n_ch - 1
    xrows = xt_ref.shape[1] * PAGE_SIZE
    nch_pad = xrows // CMP_STRIDE
    n_slc = -(-lk // SLC_BLOCK)
    nblk = -(-(xrows // SLC_BLOCK) // 128) * 128
    n_sel = min(N_SELECT, n_slc)
    slot = b % 2

    def page_copy(bb, sl, p):
        phys = pt_ref[bb, p] + layer * n_phys
        return pltpu.make_async_copy(cache_ref.at[phys], xt_ref.at[sl, p], sem.at[sl])

    def start_all(bb, sl):
        def body(p, carry):
            page_copy(bb, sl, p).start()
            return carry
        lax.fori_loop(0, n_pages, body, 0, unroll=8)

    @pl.when(b == 0)
    def _():
        start_all(0, 0)

    @pl.when(b + 1 < nb)
    def _():
        start_all(b + 1, 1 - slot)

    def wait_body(p, carry):
        page_copy(b, slot, p).wait()
        return carry
    lax.fori_loop(0, n_pages, wait_body, 0, unroll=8)

    xt_ref[slot, n_pages] = _rows_to_cols(kvn_ref[...])

    rows = NSA_H * t
    qs = _stack_heads(q_ref[...] * (NSA_HD ** -0.5)).astype(BF16)
    ti = lax.broadcasted_iota(jnp.int32, (rows, 1), 0) % t
    tpos = past + ti

    wrows = wk_ref.shape[1]
    wk_ref[:, 0:WINDOW] = win_ref[0]
    wk_ref[:, WINDOW:wrows] = _rows_to_cols(kwn_ref[...])
    kwt = wk_ref[0:NSA_HD, :].astype(BF16)
    vwt = wk_ref[NSA_HD:2 * NSA_HD, :].astype(BF16)
    wpos = past - WINDOW + lax.broadcasted_iota(jnp.int32, (rows, wrows), 1)
    wmask = (wpos >= 0) & (wpos <= tpos) & (wpos > tpos - WINDOW)
    o_win = _softmax_pv_nt(_dot(qs, kwt), wmask, vwt)

    for p in range(n_pages + 1):
        xc_ref[p * PAGE_SIZE:(p + 1) * PAGE_SIZE, :] = xt_ref[slot, p, 0:128, :].T
        sc_ref[:, p * PAGE_SIZE:(p + 1) * PAGE_SIZE] = _dot(qs, xt_ref[slot, p, 2 * NSA_HD:3 * NSA_HD, :].astype(BF16))

    ckv = _compress(lambda l: xc_ref[pl.ds(l, nch_pad, stride=CMP_STRIDE), :], nch_pad, w1_ref, pe_ref,
                    w2_ref, pq_ref)
    nc_use = (n_cmp + 127) // 128 * 128
    ck = ckv[0:nc_use, 0:NSA_HD].astype(BF16)
    cv = ckv[0:nc_use, NSA_HD:2 * NSA_HD].astype(BF16)
    sc = _dot_nt(qs, ck)
    ncol = lax.broadcasted_iota(jnp.int32, (rows, nc_use), 1)
    cmask = (ncol * CMP_STRIDE + (CMP_LEN - 1) <= tpos) & (ncol < n_cmp)
    mx = jnp.max(jnp.where(cmask, sc, NEG), axis=1, keepdims=True)
    e = jnp.where(cmask, jnp.exp(sc - mx), 0.0)
    pc = e / jnp.maximum(jnp.sum(e, axis=1, keepdims=True), 1e-30)
    o_cmp = _dot(pc.astype(BF16), cv)
    pcsum = pc[0:t] + pc[t:2 * t] + pc[2 * t:3 * t] + pc[3 * t:4 * t]

    nrow = lax.broadcasted_iota(jnp.int32, (nc_use, nblk), 0)
    jcol = lax.broadcasted_iota(jnp.int32, (nc_use, nblk), 1)
    ov = ((nrow * CMP_STRIDE < (jcol + 1) * SLC_BLOCK) & (nrow * CMP_STRIDE + CMP_LEN > jcol * SLC_BLOCK)
          & (nrow < n_cmp)).astype(F32).astype(BF16)
    imp = _dot_sel(_split3(pcsum), ov)
    jl = lax.broadcasted_iota(jnp.int32, (t, nblk), 1)
    tq = past + lax.broadcasted_iota(jnp.int32, (t, nblk), 0)
    cur = tq // SLC_BLOCK
    forced = (jl == 0) | (jl == cur) | (jl == cur - 1)
    imp = jnp.where(forced, jnp.inf, imp)
    imp = jnp.where((jl * SLC_BLOCK <= tq) & (jl < n_slc), imp, -jnp.inf)
    imp_t = jnp.concatenate([imp, jnp.full((128 - t, nblk), -jnp.inf, F32)], axis=0).T
    nb_rows = -(-n_slc // 8) * 8
    irow = lax.broadcasted_iota(jnp.int32, (nb_rows, nblk), 0)
    jlane = lax.broadcasted_iota(jnp.int32, (nb_rows, nblk), 1)
    ranks = []
    for qi in range(t):
        colv = imp_t[0:nb_rows, qi:qi + 1]
        rowv = imp[qi:qi + 1, :]
        before = (colv > rowv) | ((colv == rowv) & (irow < jlane))
        ranks.append(jnp.sum(before.astype(F32), axis=0, keepdims=True))
    rank = jnp.concatenate(ranks, axis=0)
    sel = ((rank < n_sel) & (imp > -jnp.inf)).astype(F32)
    sel4 = jnp.concatenate([sel] * NSA_H, axis=0)

    bpp = PAGE_SIZE // SLC_BLOCK
    plane = lax.broadcasted_iota(jnp.int32, (rows, PAGE_SIZE), 1)
    parts = []
    for p in range(n_pages + 1):
        scp = sc_ref[:, p * PAGE_SIZE:(p + 1) * PAGE_SIZE]
        mp = jnp.where(plane < SLC_BLOCK, sel4[:, bpp * p:bpp * p + 1], sel4[:, bpp * p + 1:bpp * p + 2]) > 0.5
        if p == n_pages:
            mp = mp & (past + plane <= tpos)
        parts.append(jnp.where(mp, scp, NEG))
    sc = jnp.concatenate(parts, axis=1)
    e = jnp.exp(sc - jnp.max(sc, axis=1, keepdims=True))
    acc = jnp.zeros((rows, NSA_HD), F32)
    for p in range(n_pages + 1):
        acc = acc + _dot_nt(e[:, p * PAGE_SIZE:(p + 1) * PAGE_SIZE].astype(BF16),
                            xt_ref[slot, p, 3 * NSA_HD:4 * NSA_HD, :].astype(BF16))
    o_slc = acc / jnp.sum(e, axis=1, keepdims=True)

    y_ref[0] = _gated_mix(dtg_ref[...], o_cmp, o_slc, o_win, t).astype(BF16)


def _nsa_dec(page_table, cache_t, q, dtg, kvc, kvw, win_t, w1, pe, w2, layer, n_phys, nb, t):
    n_pages = page_table.shape[1]
    xrows = (n_pages + 1) * PAGE_SIZE
    assert t <= PAGE_SIZE and n_pages % 8 == 0
    grid_spec = pltpu.PrefetchScalarGridSpec(
        num_scalar_prefetch=1,
        grid=(nb,),
        in_specs=[
            pl.BlockSpec(memory_space=pl.ANY),
            pl.BlockSpec((t, NSA_W), lambda b, pt: (b, 0)),
            pl.BlockSpec((t, 128), lambda b, pt: (b, 0)),
            pl.BlockSpec((t, 4 * NSA_HD), lambda b, pt: (b, 0)),
            pl.BlockSpec((t, 2 * NSA_HD), lambda b, pt: (b, 0)),
            pl.BlockSpec((1, 2 * NSA_HD, WINDOW), lambda b, pt: (layer * nb + b, 0, 0)),
            pl.BlockSpec(w1.shape, lambda b, pt: (0, 0, 0)),
            pl.BlockSpec(pe.shape, lambda b, pt: (0, 0, 0)),
            pl.BlockSpec(w2.shape, lambda b, pt: (0, 0)),
        ],
        out_specs=pl.BlockSpec((1, t, NSA_W), lambda b, pt: (b, 0, 0)),
        scratch_shapes=[
            pltpu.VMEM((2, n_pages + 1, 4 * NSA_HD, PAGE_SIZE), F32),
            pltpu.VMEM((xrows, 128), F32),
            pltpu.VMEM((xrows // CMP_STRIDE + 8, 4 * CMP_HID), F32),
            pltpu.VMEM((2 * NSA_HD, WINDOW + 128), F32),
            pltpu.VMEM((NSA_H * t, xrows), F32),
            pltpu.SemaphoreType.DMA((2,)),
        ],
    )
    return pl.pallas_call(
        functools.partial(_nsa_dec_body, layer, n_phys, n_pages, nb, t),
        grid_spec=grid_spec,
        out_shape=jax.ShapeDtypeStruct((nb, t, NSA_W), BF16),
        compiler_params=_cparams(("arbitrary",)),
        name="nsa_dec",
    )(page_table, cache_t, q, dtg, kvc, kvw, win_t, w1, pe, w2).reshape(nb * t, NSA_W)


def _out_proj_body(ys_ref, yp_ref, yn_ref, x_ref, w_ref, npost_ref, nffn_ref, x1_ref, h2_ref):
    tm = x_ref.shape[0]
    n_split = 2 if tm % 32 == 0 else 1
    for i in range(n_split):
        r = slice(i * tm // n_split, (i + 1) * tm // n_split)
        mix = (_dot(ys_ref[r, :], w_ref[0, 0:SSD_W, :]) + _dot(yp_ref[r, :], w_ref[0, SSD_W:SSD_W + POOL_W, :])
               + _dot(yn_ref[r, :], w_ref[0, SSD_W + POOL_W:D_MODEL, :]))
        x1 = x_ref[r, :] + _rmsnorm(mix, npost_ref[...])
        x1_ref[r, :] = x1
        h2_ref[r, :] = _rmsnorm(x1, nffn_ref[...]).astype(BF16)


def _out_proj(ys, yp, yn, x2d, w_out, layer, npost, nffn, tm):
    m = x2d.shape[0]

    def tile(w):
        return pl.BlockSpec((tm, w), lambda i: (i, 0))

    vec = pl.BlockSpec((1, D_MODEL), lambda i: (0, 0))
    return pl.pallas_call(
        _out_proj_body,
        grid=(m // tm,),
        in_specs=[tile(SSD_W), tile(POOL_W), tile(NSA_W), tile(D_MODEL),
                  pl.BlockSpec((1, D_MODEL, D_MODEL), lambda i: (layer, 0, 0)), vec, vec],
        out_specs=[tile(D_MODEL), tile(D_MODEL)],
        out_shape=[jax.ShapeDtypeStruct((m, D_MODEL), F32), jax.ShapeDtypeStruct((m, D_MODEL), BF16)],
        compiler_params=_cparams(("arbitrary",)),
        name="out_proj",
    )(ys, yp, yn, x2d, w_out, npost, nffn)


def _ffn_body(carry_mode, tiles_per_seq, t, tf, *refs):
    if carry_mode:
        (h2_ref, x1_ref, wg_ref, wv_ref, wd_ref, cw_ref, cb_ref, npost_ref,
         x2_ref, gsave_ref, act_ref, gext_ref, carry_ref) = refs
    else:
        (h2_ref, x1_ref, wg_ref, wv_ref, wd_ref, cw_ref, cb_ref, npost_ref, prev_ref,
         x2_ref, gsave_ref, act_ref) = refs
    tm = h2_ref.shape[0]
    h2 = h2_ref[...]
    if carry_mode:
        @pl.when(pl.program_id(0) % tiles_per_seq == 0)
        def _():
            carry_ref[...] = jnp.zeros(carry_ref.shape, F32)
    else:
        r = lax.broadcasted_iota(jnp.int32, (tm, tf), 0) % t
    for f in range(D_FF // tf):
        fs = slice(f * tf, (f + 1) * tf)
        g = _dot(h2, wg_ref[0, :, fs])
        v = _dot(h2, wv_ref[0, :, fs])
        if carry_mode:
            buf = f % 2
            gext_ref[buf, 0:8, :] = carry_ref[:, fs]
            gext_ref[buf, 8:8 + tm, :] = g
            carry_ref[:, fs] = g[tm - 8:tm]
            gsave_ref[0, :, fs] = g[tm - 8:tm]
            g1 = gext_ref[buf, 7:7 + tm, :]
            g2 = gext_ref[buf, 6:6 + tm, :]
        else:
            prev = prev_ref[:, fs]
            g1 = jnp.where(r >= 1, pltpu.roll(g, 1, 0), pltpu.roll(prev, tm - 1, 0))
            g2 = jnp.where(r >= 2, pltpu.roll(g, 2, 0), prev)
            gsave_ref[:, fs] = g
        cw = cw_ref[:, fs]
        gc = g * cw[2:3, :] + g1 * cw[1:2, :] + g2 * cw[0:1, :] + cb_ref[:, fs]
        act_ref[:, fs] = (_gelu_tanh(gc) * v).astype(BF16)
    out = _dot(act_ref[...], wd_ref[0])
    x2_ref[...] = x1_ref[...] + _rmsnorm(out, npost_ref[...])


def _ffn(h2, x1, wg, wv, wd, layer, cw, cb, npost, tm, tf, carry_mode, tiles_per_seq, t, prev_rows=None):
    m = h2.shape[0]
    nm = m // tm
    resident = pl.Buffered(1)
    in_specs = [
        pl.BlockSpec((tm, D_MODEL), lambda i: (i, 0)),
        pl.BlockSpec((tm, D_MODEL), lambda i: (i, 0)),
        pl.BlockSpec((1, D_MODEL, D_FF), lambda i: (layer, 0, 0), pipeline_mode=resident),
        pl.BlockSpec((1, D_MODEL, D_FF), lambda i: (layer, 0, 0), pipeline_mode=resident),
        pl.BlockSpec((1, D_FF, D_MODEL), lambda i: (layer, 0, 0), pipeline_mode=resident),
        pl.BlockSpec((FFN_CONV, D_FF), lambda i: (0, 0)),
        pl.BlockSpec((1, D_FF), lambda i: (0, 0)),
        pl.BlockSpec((1, D_MODEL), lambda i: (0, 0)),
    ]
    args = [h2, x1, wg, wv, wd, cw, cb, npost]
    scratch = [pltpu.VMEM((tm, D_FF), BF16)]
    if carry_mode:
        gsave_shape = jax.ShapeDtypeStruct((nm, 8, D_FF), F32)
        gsave_spec = pl.BlockSpec((1, 8, D_FF), lambda i: (i, 0, 0))
        scratch += [pltpu.VMEM((2, tm + 8, tf), F32), pltpu.VMEM((8, D_FF), F32)]
    else:
        gsave_shape = jax.ShapeDtypeStruct((m, D_FF), F32)
        gsave_spec = pl.BlockSpec((tm, D_FF), lambda i: (i, 0))
        in_specs.append(pl.BlockSpec((tm, D_FF), lambda i: (i, 0)))
        args.append(prev_rows)
    return pl.pallas_call(
        functools.partial(_ffn_body, carry_mode, tiles_per_seq, t, tf),
        grid=(nm,),
        in_specs=in_specs,
        out_specs=[pl.BlockSpec((tm, D_MODEL), lambda i: (i, 0)), gsave_spec],
        out_shape=[jax.ShapeDtypeStruct((m, D_MODEL), F32), gsave_shape],
        scratch_shapes=scratch,
        compiler_params=_cparams(("arbitrary",)),
        name="ffn",
    )(*args)


def _prep_params(w_in, ssd_conv_w, ssd_conv_b, ssd_dt_bias, ssd_a_log, ssd_d, ssd_norm, pool_w, pool_scale,
                 nsa_pe_k, nsa_pe_v, nsa_w1_k, nsa_w1_v, nsa_w2_k, nsa_w2_v, w_out, ffn_w_gate, ffn_w_val,
                 ffn_w_down):
    depth = w_in.shape[0]
    w_in_p = jnp.concatenate(
        [w_in[..., 0:OFF_XBC], w_in[..., OFF_DT:OFF_KV], w_in[..., OFF_XBC:OFF_DT], w_in[..., OFF_KV:IN_W],
         jnp.zeros((depth, D_MODEL, IN_W_PAD - IN_W), F32)], axis=-1).astype(BF16)
    pad_h = ((0, 0), (0, 128 - SSD_H))
    dtb = jnp.pad(ssd_dt_bias, pad_h)[:, None, :]
    alog = jnp.pad(ssd_a_log, pad_h)[:, None, :]
    dsk = jnp.repeat(ssd_d, SSD_HD, axis=1)[:, None, :]
    n_g = len(POOL_WINDOWS)
    eye = jnp.eye(n_g, dtype=F32)
    pw_bd = (pool_w[:, :, :, None, :] * eye[None, :, None, :, None]).reshape(depth, POOL_W, POOL_W).astype(BF16)
    zk = jnp.zeros((depth, CMP_STRIDE, NSA_HD, CMP_HID), F32)
    top = jnp.concatenate([nsa_w1_k[:, :CMP_STRIDE], zk, nsa_w1_k[:, CMP_STRIDE:], zk], axis=-1)
    bot = jnp.concatenate([zk, nsa_w1_v[:, :CMP_STRIDE], zk, nsa_w1_v[:, CMP_STRIDE:]], axis=-1)
    w1 = jnp.concatenate([top, bot], axis=2).astype(BF16).reshape(depth, CMP_STRIDE // 2, 256, 4 * CMP_HID)
    pe_a = jnp.concatenate([nsa_pe_k[:, :CMP_STRIDE], nsa_pe_v[:, :CMP_STRIDE]], axis=-1)
    pe_b = jnp.concatenate([nsa_pe_k[:, CMP_STRIDE:], nsa_pe_v[:, CMP_STRIDE:]], axis=-1)
    pe = jnp.concatenate([pe_a[:, :, None], pe_b[:, :, None], jnp.zeros((depth, CMP_STRIDE, 6, 128), F32)], axis=2)
    pe = pe.reshape(depth, CMP_STRIDE // 2, 2, 8, 128).transpose(0, 1, 3, 2, 4).reshape(depth, CMP_STRIDE // 2, 8, 256)
    zw = jnp.zeros((depth, CMP_HID, NSA_HD), F32)
    w2 = jnp.concatenate([jnp.concatenate([nsa_w2_k, zw], axis=-1), jnp.concatenate([zw, nsa_w2_v], axis=-1)],
                         axis=1).astype(BF16)
    return dict(w_in=w_in_p, cw=ssd_conv_w, cb=ssd_conv_b[:, None, :], dtb=dtb, alog=alog, dsk=dsk,
                nrm=ssd_norm[:, None, :], pw=pw_bd, ps=pool_scale[:, None, :], w1=w1, pe=pe, w2=w2,
                w_out=w_out.astype(BF16), wg=ffn_w_gate.astype(BF16), wv=ffn_w_val.astype(BF16),
                wd=ffn_w_down.astype(BF16))


def _mixer_weights(p, l):
    return (p['cw'][l], p['cb'][l], p['dtb'][l], p['alog'][l], p['dsk'][l], p['nrm'][l], p['pw'][l], p['ps'][l])


def kernel(x_prompt, x_sample, cache_nsa_kv, page_table, state_nsa_win, state_ssd_conv, state_ssm, state_pool, state_ffn_conv, norm_mix_pre, w_in, ssd_conv_w, ssd_conv_b, ssd_dt_bias, ssd_a_log, ssd_d, ssd_norm, pool_w, pool_scale, nsa_pe_k, nsa_pe_v, nsa_w1_k, nsa_w1_v, nsa_w2_k, nsa_w2_v, w_out, norm_mix_post, norm_ffn_pre, ffn_w_gate, ffn_w_val, ffn_conv_w, ffn_conv_b, ffn_w_down, norm_ffn_post):
    depth = w_in.shape[0]
    bp, tp, _ = x_prompt.shape
    bs, ts, _ = x_sample.shape
    n_phys = cache_nsa_kv.shape[1]
    n_pages = page_table.shape[1]
    past = n_pages * PAGE_SIZE
    assert tp % 512 == 0 and tp >= WINDOW and ts == 8 and state_nsa_win.shape[2] == WINDOW
    p = _prep_params(w_in, ssd_conv_w, ssd_conv_b, ssd_dt_bias, ssd_a_log, ssd_d, ssd_norm, pool_w, pool_scale,
                     nsa_pe_k, nsa_pe_v, nsa_w1_k, nsa_w1_v, nsa_w2_k, nsa_w2_v, w_out, ffn_w_gate, ffn_w_val,
                     ffn_w_down)
    ms = bs * ts
    tab_p = _rope_table(jnp.arange(tp, dtype=jnp.int32))
    tab_s = jnp.tile(_rope_table(past + jnp.arange(ts, dtype=jnp.int32)), (bs, 1))
    cache = jnp.transpose(cache_nsa_kv, (0, 1, 3, 4, 2)).reshape(depth * n_phys, 4 * NSA_HD, PAGE_SIZE)
    win_state = jnp.transpose(state_nsa_win, (0, 1, 3, 4, 2)).reshape(depth * bs, 2 * NSA_HD, WINDOW)

    tm_p = 512
    tm_f = 512
    tf = 512
    nc_p = tp // CHUNK
    zeros_c = jnp.zeros((bp, 8, SSD_CONV_DIM), F32)
    zeros_s = jnp.zeros((bp, SSD_H, SSD_HD, SSD_N), F32)
    zeros_p = jnp.zeros((bp, 16, POOL_W), F32)

    xp = x_prompt.reshape(bp * tp, D_MODEL)
    xs = x_sample.reshape(ms, D_MODEL)
    outs_p, outs_s = [], []
    for l in range(depth):
        nw = norm_mix_pre[l][None]
        npost = norm_mix_post[l][None]
        nffn = norm_ffn_pre[l][None]
        nfpost = norm_ffn_post[l][None]
        mw = _mixer_weights(p, l)
        fcw, fcb = ffn_conv_w[l], ffn_conv_b[l][None]

        y_ssd, y_pool, ssm_new, q, kvc, kvw, dtg, ctail, ptail = _ssd_pool(
            xp, zeros_c, zeros_s, zeros_p, mw, bp, nc_p, CHUNK, 0, 2, fused_in=(nw, p['w_in'], l, tab_p))
        ckv = _nsa_cmp(kvc, p['w1'][l], p['pe'][l], p['w2'][l], bp, tp)
        y_nsa = _nsa_attn(q, dtg, ckv, kvc, kvw, bp, tp)
        x1, h2 = _out_proj(y_ssd, y_pool, y_nsa, xp, p['w_out'], l, npost, nffn, tm_p)
        xp, gsave = _ffn(h2, x1, p['wg'], p['wv'], p['wd'], l, fcw, fcb, nfpost, tm_f, tf, True, tp // tm_f, tp)
        outs_p.append((
            kvc.reshape(bp, tp, 4, NSA_HD),
            kvw.reshape(bp, tp, 2, NSA_HD)[:, tp - WINDOW:],
            ctail[:, 8 - (SSD_CONV - 1):],
            ssm_new,
            ptail[:, 16 - POOL_KEEP:],
            gsave.reshape(bp, tp // tm_f, 8, D_FF)[:, -1, 8 - (FFN_CONV - 1):],
        ))

        z, xbc, u, q, kvc, kvw, dtg = _in_proj(xs, nw, p['w_in'], l, tab_s, ms)
        cprev = jnp.pad(state_ssd_conv[l], ((0, 0), (8 - (SSD_CONV - 1), 0), (0, 0)))
        pprev = jnp.pad(state_pool[l], ((0, 0), (16 - POOL_KEEP, 0), (0, 0)))
        y_ssd, y_pool, ssm_new = _ssd_pool((xbc, z, dtg, u), cprev, state_ssm[l], pprev, mw, bs, 1, ts, past, 4)
        y_nsa = _nsa_dec(page_table, cache, q, dtg, kvc, kvw, win_state, p['w1'][l], p['pe'][l], p['w2'][l],
                         l, n_phys, bs, ts)
        x1, h2 = _out_proj(y_ssd, y_pool, y_nsa, xs, p['w_out'], l, npost, nffn, ms)
        fprev = state_ffn_conv[l]
        fprev_rows = jnp.pad(fprev, ((0, 0), (0, ts - (FFN_CONV - 1)), (0, 0))).reshape(ms, D_FF)
        xs, gsave = _ffn(h2, x1, p['wg'], p['wv'], p['wd'], l, fcw, fcb, nfpost, ms, tf, False, 1, ts, fprev_rows)
        kvw3 = kvw.reshape(bs, ts, 2, NSA_HD)
        xbc3 = xbc.reshape(bs, ts, SSD_CONV_DIM)
        u3 = u.reshape(bs, ts, POOL_W)
        g3 = gsave.reshape(bs, ts, D_FF)
        outs_s.append((
            kvc.reshape(bs, ts, 4, NSA_HD),
            jnp.concatenate([state_nsa_win[l], kvw3], axis=1)[:, -WINDOW:],
            jnp.concatenate([state_ssd_conv[l], xbc3], axis=1)[:, -(SSD_CONV - 1):],
            ssm_new,
            jnp.concatenate([state_pool[l], u3], axis=1)[:, -POOL_KEEP:],
            jnp.concatenate([state_ffn_conv[l], g3], axis=1)[:, -(FFN_CONV - 1):],
        ))

    def stk(outs, i):
        return jnp.stack([o[i] for o in outs])

    return (xp.reshape(bp, tp, D_MODEL), xs.reshape(bs, ts, D_MODEL), stk(outs_p, 0), stk(outs_s, 0),
            stk(outs_p, 1), stk(outs_s, 1), stk(outs_p, 2), stk(outs_s, 2), stk(outs_p, 3), stk(outs_s, 3),
            stk(outs_p, 4), stk(outs_s, 4), stk(outs_p, 5), stk(outs_s, 5))
```
